```python
import jax, jax.numpy as jnp
from jax import lax
import numpy as np

D_MODEL = 4096
BATCH = 4
SEQ = 2048
DEPTH = 2
DEC_BATCH = 8
DEC_SEQ = 4
PAST_LEN = 16384
PAGE_SIZE = 128

N_META = 16
N_EVEN = (DEPTH + 1) // 2
N_ODD = DEPTH // 2
D_FF = 4 * D_MODEL
RWKV_WIDTH = D_MODEL // 2
RWKV_HEAD = 64
RWKV_HEADS = RWKV_WIDTH // RWKV_HEAD
W_LORA = max(32, round(RWKV_WIDTH ** 0.5 * 1.8 / 32) * 32)
A_LORA = max(32, round(RWKV_WIDTH ** 0.5 * 1.8 / 32) * 32)
G_LORA = max(32, round(RWKV_WIDTH ** 0.6 * 0.6 / 32) * 32)
RWKV_PROJ = 3 * RWKV_WIDTH + W_LORA + A_LORA + G_LORA
FOX_WIDTH = D_MODEL - RWKV_WIDTH
FOX_HEAD = 128
FOX_HEADS = FOX_WIDTH // FOX_HEAD
FOX_PROJ = 3 * FOX_WIDTH + FOX_HEADS
IN_EVEN = RWKV_PROJ + FOX_PROJ
Q_BLOCK = 128
FORGET_BIAS = 7.0
CONV_DIM = D_MODEL
CONV_WIDTH = 3
RMS_EPS = 1e-6
GN_EPS = 64e-5
NEG_INF = -1e30

kernel_name = 'hybrid_rwkv7_fox_shortconv_decode_step'


def _rmsnorm(x, g):
    x32 = x.astype(jnp.float32)
    y = x32 * lax.rsqrt(jnp.mean(x32 * x32, axis=-1, keepdims=True) + RMS_EPS)
    return (y * g.astype(jnp.float32)).astype(x.dtype)


def _rwkv7(p, prev_row, S0, mu, w0, w2, a0, a2, g2, k_k, k_a, r_k, gn_g, gn_b):
    f32 = jnp.float32
    B, T, _ = p.shape
    dt = p.dtype
    p_prev = jnp.concatenate([prev_row[:, None, :].astype(dt), p[:, :-1]], axis=1)
    ps = (p + (p_prev - p) * mu).astype(f32)
    c = RWKV_WIDTH
    r = ps[..., :c]
    k = ps[..., c:2 * c]
    v = ps[..., 2 * c:3 * c]
    o1 = 3 * c
    xw = ps[..., o1:o1 + W_LORA]
    o2 = o1 + W_LORA
    xa = ps[..., o2:o2 + A_LORA]
    o3 = o2 + A_LORA
    xg = ps[..., o3:o3 + G_LORA]
    w = -jax.nn.softplus(-(w0.astype(f32) + jnp.tanh(xw) @ w2.astype(f32))) - 0.5
    decay = jnp.exp(-jnp.exp(w))
    a = jax.nn.sigmoid(a0.astype(f32) + xa @ a2.astype(f32))
    g = jax.nn.sigmoid(xg) @ g2.astype(f32)
    heads = lambda t: t.reshape(B, T, RWKV_HEADS, RWKV_HEAD)
    kk = heads(k * k_k.astype(f32))
    kk = kk / jnp.maximum(jnp.sqrt(jnp.sum(kk * kk, axis=-1, keepdims=True)), 1e-12)
    k = k * (1.0 + (a - 1.0) * k_a.astype(f32))
    r, k, v, decay, a = heads(r), heads(k), heads(v), heads(decay), heads(a)

    def step(S, inp):
        r_t, w_t, k_t, v_t, kk_t, a_t = inp
        sa = jnp.einsum('bhvk,bhk->bhv', S, -kk_t)
        S = (S * w_t[:, :, None, :] + sa[..., None] * (kk_t * a_t)[:, :, None, :]
             + v_t[..., None] * k_t[:, :, None, :])
        return S, jnp.einsum('bhvk,bhk->bhv', S, r_t)

    xs = tuple(jnp.moveaxis(t, 1, 0) for t in (r, decay, k, v, kk, a))
    S_T, o = lax.scan(step, S0.astype(f32), xs)
    o = jnp.moveaxis(o, 0, 1)
    mean = jnp.mean(o, axis=-1, keepdims=True)
    var = jnp.mean(jnp.square(o - mean), axis=-1, keepdims=True)
    o = ((o - mean) * lax.rsqrt(var + GN_EPS)).reshape(B, T, c) * gn_g.astype(f32) + gn_b.astype(f32)
    bonus = jnp.sum(r * k * r_k.astype(f32), axis=-1, keepdims=True) * v
    out = (o + bonus.reshape(B, T, c)) * g
    return out.astype(dt), S_T.astype(S0.dtype), p[:, -1]


def _fox_prompt(q, k, v, lf):
    f32 = jnp.float32
    B, L, H, Dh = q.shape
    scale = Dh ** -0.5
    c = jnp.cumsum(lf, axis=1)
    c_k = c.transpose(0, 2, 1)
    pos = jnp.arange(L)

    def block(args):
        q_blk, c_blk, qpos = args
        s = jnp.einsum('bqhd,bkhd->bhqk', q_blk, k).astype(f32) * scale
        s = s + c_blk.transpose(0, 2, 1)[..., :, None] - c_k[..., None, :]
        s = jnp.where(qpos[:, None] >= pos[None, :], s, NEG_INF)
        p = jax.nn.softmax(s, axis=-1)
        return jnp.einsum('bhqk,bkhd->bqhd', p.astype(v.dtype), v)

    o_meta = block((q[:, :N_META], c[:, :N_META], pos[:N_META]))
    n_blk = (L - N_META) // Q_BLOCK
    qr = jnp.moveaxis(q[:, N_META:].reshape(B, n_blk, Q_BLOCK, H, Dh), 1, 0)
    cr = jnp.moveaxis(c[:, N_META:].reshape(B, n_blk, Q_BLOCK, H), 1, 0)
    pr = pos[N_META:].reshape(n_blk, Q_BLOCK)
    o_real = lax.map(block, (qr, cr, pr))
    o_real = jnp.moveaxis(o_real, 0, 1).reshape(B, L - N_META, H, Dh)
    return jnp.concatenate([o_meta, o_real], axis=1)


def _fox_sample(q, k_new, v_new, lf_new, k_pool, v_pool, lf_pool, page_table):
    f32 = jnp.float32
    Bd, T, H, Dh = q.shape
    n_pages = page_table.shape[1]
    scale = Dh ** -0.5
    lf_past = lf_pool[page_table].reshape(Bd, n_pages * PAGE_SIZE, H).astype(f32)
    tail = lax.cumsum(lf_past, axis=1, reverse=True) - lf_past
    c_new = jnp.cumsum(lf_new, axis=1).transpose(0, 2, 1)
    tail_pages = jnp.moveaxis(tail.reshape(Bd, n_pages, PAGE_SIZE, H), 1, 0)

    def page_step(carry, inp):
        m, l, acc = carry
        phys, tail_p = inp
        kp = k_pool[phys]
        vp = v_pool[phys]
        s = jnp.einsum('bqhd,bkhd->bhqk', q, kp).astype(f32) * scale
        s = s + tail_p.transpose(0, 2, 1)[:, :, None, :] + c_new[..., None]
        m_new = jnp.maximum(m, jnp.max(s, axis=-1))
        corr = jnp.exp(m - m_new)
        pe = jnp.exp(s - m_new[..., None])
        acc = acc * corr[..., None] + jnp.einsum('bhqk,bkhd->bhqd', pe, vp.astype(f32))
        return (m_new, l * corr + jnp.sum(pe, axis=-1), acc), None

    init = (jnp.full((Bd, H, T), NEG_INF, f32), jnp.zeros((Bd, H, T), f32),
            jnp.zeros((Bd, H, T, Dh), f32))
    (m, l, acc), _ = lax.scan(page_step, init, (page_table.T, tail_pages))
    s_new = jnp.einsum('bqhd,bkhd->bhqk', q, k_new).astype(f32) * scale
    s_new = s_new + c_new[..., :, None] - c_new[..., None, :]
    s_new = jnp.where(jnp.tril(jnp.ones((T, T), dtype=bool)), s_new, NEG_INF)
    m_f = jnp.maximum(m, jnp.max(s_new, axis=-1))
    corr = jnp.exp(m - m_f)
    p_new = jnp.exp(s_new - m_f[..., None])
    l = l * corr + jnp.sum(p_new, axis=-1)
    acc = acc * corr[..., None] + jnp.einsum('bhqk,bkhd->bhqd', p_new, v_new.astype(f32))
    o = acc / l[..., None]
    return o.transpose(0, 2, 1, 3).astype(q.dtype)


def _short_conv(p, buf, conv_w):
    b_gate = p[..., :CONV_DIM]
    c_gate = p[..., CONV_DIM:2 * CONV_DIM]
    hin = p[..., 2 * CONV_DIM:]
    u = c_gate * hin
    T = u.shape[1]
    u_ext = jnp.concatenate([buf.astype(u.dtype), u], axis=1)
    y = conv_w[0] * u_ext[:, 0:T]
    for i in range(1, CONV_WIDTH):
        y = y + conv_w[i] * u_ext[:, i:i + T]
    return b_gate * y, u_ext[:, T:]


def setup_inputs(seed: int = 0) -> dict:
    key = jax.random.key(seed)
    keys = list(jax.random.split(key, 40))
    f32 = jnp.float32
    n_pages = PAST_LEN // PAGE_SIZE
    n_phys = (DEC_BATCH * n_pages * 5 + 3) // 4

    def nrm(shape, scale=1.0):
        return scale * jax.random.normal(keys.pop(), shape, f32)

    page_table = jax.random.permutation(keys.pop(), n_phys)[:DEC_BATCH * n_pages]
    page_table = page_table.reshape(DEC_BATCH, n_pages).astype(jnp.int32)
    cache_logf = jax.nn.log_sigmoid(FORGET_BIAS + nrm((N_EVEN, n_phys, PAGE_SIZE, FOX_HEADS), 0.5))
    return {
        'x_prompt': nrm((BATCH, SEQ, D_MODEL)),
        'x_sample': nrm((DEC_BATCH, DEC_SEQ, D_MODEL)),
        'state_rwkv': nrm((N_EVEN, DEC_BATCH, RWKV_HEADS, RWKV_HEAD, RWKV_HEAD), 0.5),
        'state_rwkv_shift': nrm((N_EVEN, DEC_BATCH, RWKV_PROJ)),
        'cache_fox_k': nrm((N_EVEN, n_phys, PAGE_SIZE, FOX_HEADS, FOX_HEAD)),
        'cache_fox_v': nrm((N_EVEN, n_phys, PAGE_SIZE, FOX_HEADS, FOX_HEAD)),
        'cache_fox_logf': cache_logf,
        'state_conv': nrm((N_ODD, DEC_BATCH, CONV_WIDTH - 1, CONV_DIM)),
        'page_table': page_table,
        'meta_tokens': nrm((N_META, D_MODEL)),
        'norm_mix': 1.0 + nrm((DEPTH, D_MODEL), 0.05),
        'norm_mlp': 1.0 + nrm((DEPTH, D_MODEL), 0.05),
        'norm_final': 1.0 + nrm((D_MODEL,), 0.05),
        'w_in_even': nrm((N_EVEN, D_MODEL, IN_EVEN), D_MODEL ** -0.5),
        'b_forget': FORGET_BIAS + nrm((N_EVEN, FOX_HEADS), 0.5),
        'rwkv_mu': jax.random.uniform(keys.pop(), (N_EVEN, RWKV_PROJ), f32),
        'rwkv_w0': -1.0 + nrm((N_EVEN, RWKV_WIDTH), 0.5),
        'rwkv_w2': nrm((N_EVEN, W_LORA, RWKV_WIDTH), W_LORA ** -0.5),
        'rwkv_a0': nrm((N_EVEN, RWKV_WIDTH), 0.1),
        'rwkv_a2': nrm((N_EVEN, A_LORA, RWKV_WIDTH), A_LORA ** -0.5),
        'rwkv_g2': nrm((N_EVEN, G_LORA, RWKV_WIDTH), G_LORA ** -0.5),
        'rwkv_k_k': 0.85 + nrm((N_EVEN, RWKV_WIDTH), 0.05),
        'rwkv_k_a': 1.0 + nrm((N_EVEN, RWKV_WIDTH), 0.05),
        'rwkv_r_k': nrm((N_EVEN, RWKV_HEADS, RWKV_HEAD), 0.1),
        'rwkv_gn_g': 1.0 + nrm((N_EVEN, RWKV_WIDTH), 0.05),
        'rwkv_gn_b': nrm((N_EVEN, RWKV_WIDTH), 0.01),
        'w_out_even': nrm((N_EVEN, D_MODEL, D_MODEL), D_MODEL ** -0.5),
        'w_in_odd': nrm((N_ODD, D_MODEL, 3 * CONV_DIM), D_MODEL ** -0.5),
        'conv_w': nrm((N_ODD, CONV_WIDTH, CONV_DIM), CONV_WIDTH ** -0.5),
        'w_out_odd': nrm((N_ODD, CONV_DIM, D_MODEL), CONV_DIM ** -0.5),
        'w_up': nrm((DEPTH, D_MODEL, D_FF), D_MODEL ** -0.5),
        'w_down': nrm((DEPTH, D_FF, D_MODEL), D_FF ** -0.5),
    }


def reference(x_prompt, x_sample, state_rwkv, state_rwkv_shift, cache_fox_k, cache_fox_v,
              cache_fox_logf, state_conv, page_table, meta_tokens, norm_mix, norm_mlp, norm_final,
              w_in_even, b_forget, rwkv_mu, rwkv_w0, rwkv_w2, rwkv_a0, rwkv_a2, rwkv_g2, rwkv_k_k,
              rwkv_k_a, rwkv_r_k, rwkv_gn_g, rwkv_gn_b, w_out_even, w_in_odd, conv_w, w_out_odd,
              w_up, w_down):

    def trunk(h, S0, shift0, conv0, fox_attend):
        B, T, _ = h.shape
        S_out, sh_out, k_out, v_out, lf_out, conv_out = [], [], [], [], [], []
        for layer in range(DEPTH):
            hn = _rmsnorm(h, norm_mix[layer])
            if layer % 2 == 0:
                e = layer // 2
                proj = hn @ w_in_even[e]
                p_rwkv = proj[..., :RWKV_PROJ]
                p_fox = proj[..., RWKV_PROJ:]
                o_rwkv, S_T, sh_T = _rwkv7(p_rwkv, shift0[e], S0[e], rwkv_mu[e], rwkv_w0[e],
                                           rwkv_w2[e], rwkv_a0[e], rwkv_a2[e], rwkv_g2[e],
                                           rwkv_k_k[e], rwkv_k_a[e], rwkv_r_k[e],
                                           rwkv_gn_g[e], rwkv_gn_b[e])
                q = p_fox[..., :FOX_WIDTH].reshape(B, T, FOX_HEADS, FOX_HEAD)
                k = p_fox[..., FOX_WIDTH:2 * FOX_WIDTH].reshape(B, T, FOX_HEADS, FOX_HEAD)
                v = p_fox[..., 2 * FOX_WIDTH:3 * FOX_WIDTH].reshape(B, T, FOX_HEADS, FOX_HEAD)
                lf = jax.nn.log_sigmoid((p_fox[..., 3 * FOX_WIDTH:] + b_forget[e]).astype(jnp.float32))
                o_fox = fox_attend(e, q, k, v, lf).reshape(B, T, FOX_WIDTH).astype(h.dtype)
                mix = jnp.concatenate([o_rwkv, o_fox], axis=-1) @ w_out_even[e]
                S_out.append(S_T)
                sh_out.append(sh_T)
                k_out.append(k)
                v_out.append(v)
                lf_out.append(lf.astype(h.dtype))
            else:
                o = layer // 2
                proj = hn @ w_in_odd[o]
                y, buf = _short_conv(proj, conv0[o], conv_w[o])
                mix = y @ w_out_odd[o]
                conv_out.append(buf)
            h = h + mix
            hn = _rmsnorm(h, norm_mlp[layer])
            h = h + jnp.square(jax.nn.relu(hn @ w_up[layer])) @ w_down[layer]
        return (_rmsnorm(h, norm_final), jnp.stack(S_out), jnp.stack(sh_out), jnp.stack(k_out),
                jnp.stack(v_out), jnp.stack(lf_out), jnp.stack(conv_out))

    Bp = x_prompt.shape[0]
    dt = x_prompt.dtype
    h0 = jnp.concatenate([jnp.broadcast_to(meta_tokens[None].astype(dt), (Bp, N_META, D_MODEL)),
                          x_prompt], axis=1)
    zS = jnp.zeros((N_EVEN, Bp, RWKV_HEADS, RWKV_HEAD, RWKV_HEAD), dt)
    zsh = jnp.zeros((N_EVEN, Bp, RWKV_PROJ), dt)
    zconv = jnp.zeros((N_ODD, Bp, CONV_WIDTH - 1, CONV_DIM), dt)
    yp, p_S, p_sh, p_k, p_v, p_lf, p_conv = trunk(
        h0, zS, zsh, zconv, lambda e, q, k, v, lf: _fox_prompt(q, k, v, lf))
    y_prompt = yp[:, N_META:]

    y_sample, s_S, s_sh, s_k, s_v, s_lf, s_conv = trunk(
        x_sample, state_rwkv, state_rwkv_shift, state_conv,
        lambda e, q, k, v, lf: _fox_sample(q, k, v, lf, cache_fox_k[e], cache_fox_v[e],
                                           cache_fox_logf[e], page_table))
    return (y_prompt, y_sample, p_S, p_sh, p_k, p_v, p_lf, p_conv, s_S, s_sh, s_k, s_v, s_lf, s_conv)
```

```python
import functools
import math

import jax
import jax.numpy as jnp
from jax import lax
from jax.experimental import pallas as pl
from jax.experimental.pallas import tpu as pltpu

F32 = jnp.float32
BF16 = jnp.bfloat16
HI = lax.Precision.HIGHEST

RMS_EPS = 1e-6
GN_EPS = 64e-5
NEG_INF = -1e30
N_META = 16
RWKV_HEAD = 64
FOX_HEAD = 128
PAGE_SIZE = 128
LANES = 128
VMEM_LIMIT = 60 * 1024 * 1024


def _largest_divisor(n, cap, mult):
    best = None
    for d in range(mult, min(n, cap) + 1, mult):
        if n % d == 0:
            best = d
    return n if best is None else best


def _params(*sem):
    return pltpu.CompilerParams(dimension_semantics=sem, vmem_limit_bytes=VMEM_LIMIT)


def _dot(a, b, precision=None):
    return jnp.dot(a, b, precision=precision, preferred_element_type=F32)


def _dot_nt(a, b, precision=None):
    return lax.dot_general(a, b, (((1,), (1,)), ((), ())), precision=precision,
                           preferred_element_type=F32)


def _iota(shape, dim):
    return lax.broadcasted_iota(jnp.int32, shape, dim)


def _rmsnorm_kernel(x_ref, g_ref, o_ref):
    x = x_ref[...]
    ms = jnp.mean(x * x, axis=-1, keepdims=True)
    o_ref[...] = (x * lax.rsqrt(ms + RMS_EPS) * g_ref[...]).astype(o_ref.dtype)


def _rmsnorm(x2d, g, out_dtype):
    m, d = x2d.shape
    tr = _largest_divisor(m, 512, 16)
    return pl.pallas_call(
        _rmsnorm_kernel,
        grid=(m // tr,),
        in_specs=[pl.BlockSpec((tr, d), lambda i: (i, 0)),
                  pl.BlockSpec((1, d), lambda i: (0, 0))],
        out_specs=pl.BlockSpec((tr, d), lambda i: (i, 0)),
        out_shape=jax.ShapeDtypeStruct((m, d), out_dtype),
        compiler_params=_params("parallel"),
        name="rmsnorm",
    )(x2d, g.reshape(1, d))


def _final_norm_prompt_kernel(a_ref, b_ref, g_ref, o_ref):
    x = jnp.concatenate([a_ref[0, N_META:, :], b_ref[0]], axis=0)
    ms = jnp.mean(x * x, axis=-1, keepdims=True)
    o_ref[0] = x * lax.rsqrt(ms + RMS_EPS) * g_ref[...]


def _final_norm_prompt(h3, g):
    b, l, d = h3.shape
    s = l - N_META
    tr = _largest_divisor(s, 256, N_META)
    sub = tr // N_META
    return pl.pallas_call(
        _final_norm_prompt_kernel,
        grid=(b, s // tr),
        in_specs=[pl.BlockSpec((1, tr, d), lambda i, j: (i, j, 0)),
                  pl.BlockSpec((1, N_META, d), lambda i, j: (i, (j + 1) * sub, 0)),
                  pl.BlockSpec((1, d), lambda i, j: (0, 0))],
        out_specs=pl.BlockSpec((1, tr, d), lambda i, j: (i, j, 0)),
        out_shape=jax.ShapeDtypeStruct((b, s, d), F32),
        compiler_params=_params("parallel", "parallel"),
        name="final_norm_prompt",
    )(h3, h3, g.reshape(1, d))


def _mm_kernel(*refs, nk, epilogue):
    if epilogue == "resid":
        x_ref, w_ref, r_ref, o_ref = refs[:4]
    else:
        x_ref, w_ref, o_ref = refs[:3]
        r_ref = None
    acc_ref = refs[-1] if nk > 1 else None

    def finish(acc):
        if epilogue == "relu2":
            acc = jnp.square(jnp.maximum(acc, 0.0))
        elif epilogue == "resid":
            acc = r_ref[...] + acc
        o_ref[...] = acc.astype(o_ref.dtype)

    part = _dot(x_ref[...], w_ref[...].astype(BF16))
    if nk == 1:
        finish(part)
    else:
        k = pl.program_id(2)

        @pl.when(k == 0)
        def _():
            acc_ref[...] = part

        @pl.when(jnp.logical_and(k > 0, k < nk - 1))
        def _():
            acc_ref[...] += part

        @pl.when(k == nk - 1)
        def _():
            finish(acc_ref[...] + part)


def _matmul(x, w, *, n_lo=0, n_cols=None, tn, tk, epilogue="plain", resid=None, out_dtype=F32):
    m, kdim = x.shape
    n_cols = w.shape[1] - n_lo if n_cols is None else n_cols
    tn = _largest_divisor(n_cols, tn, LANES)
    tk = _largest_divisor(kdim, tk, LANES)
    assert n_lo % tn == 0
    tm = _largest_divisor(m, 2064, 16)
    nk = kdim // tk
    off = n_lo // tn
    in_specs = [pl.BlockSpec((tm, tk), lambda i, j, k: (i, k)),
                pl.BlockSpec((tk, tn), lambda i, j, k: (k, j + off))]
    args = [x, w]
    if epilogue == "resid":
        in_specs.append(pl.BlockSpec((tm, tn), lambda i, j, k: (i, j)))
        args.append(resid)
    return pl.pallas_call(
        functools.partial(_mm_kernel, nk=nk, epilogue=epilogue),
        grid=(m // tm, n_cols // tn, nk),
        in_specs=in_specs,
        out_specs=pl.BlockSpec((tm, tn), lambda i, j, k: (i, j)),
        out_shape=jax.ShapeDtypeStruct((m, n_cols), out_dtype),
        scratch_shapes=[pltpu.VMEM((tm, tn), F32)] if nk > 1 else [],
        compiler_params=_params("parallel", "parallel", "arbitrary"),
        name="matmul_" + epilogue,
    )(*args)


def _rwkv_kernel(pr_ref, pk_ref, pv_ref, px_ref, sr_ref, sk_ref, sv_ref, sx_ref,
                 mur_ref, muk_ref, muv_ref, mux_ref, w0_ref, a0_ref, kk_ref, ka_ref, rk_ref,
                 gg_ref, gb_ref, w2_ref, a2_ref, g2_ref, s0_ref,
                 o_ref, st_ref,
                 s_scr, br_scr, bk_scr, bv_scr, bx_scr, *, c, n_valid, n_chunks):
    ci = pl.program_id(2)
    half = RWKV_HEAD

    @pl.when(ci == 0)
    def _():
        s_scr[...] = s0_ref[0, 0]
        br_scr[7:8, :] = sr_ref[0]
        bk_scr[7:8, :] = sk_ref[0]
        bv_scr[7:8, :] = sv_ref[0]
        bx_scr[7:8, :] = sx_ref[0]

    def shifted(p_ref, buf, mu_ref):
        p = p_ref[0]
        buf[8:8 + c, :] = p
        prev = buf[7:7 + c, :]
        buf[7:8, :] = p[c - 1:c, :]
        return p + (prev - p) * mu_ref[...]

    r = shifted(pr_ref, br_scr, mur_ref)
    k = shifted(pk_ref, bk_scr, muk_ref)
    v = shifted(pv_ref, bv_scr, muv_ref)
    x = shifted(px_ref, bx_scr, mux_ref)

    lane = _iota((LANES, LANES), 1)
    row = _iota((LANES, LANES), 0)
    ones_bd = ((lane // half) == (row // half)).astype(F32)

    def seg_sum(t):
        return _dot(t, ones_bd, HI)

    wl = w0_ref[...] + _dot(jnp.tanh(x), w2_ref[...], HI)
    wl = -jax.nn.softplus(-wl) - 0.5
    logw = -jnp.exp(wl)
    a = jax.nn.sigmoid(a0_ref[...] + _dot(x, a2_ref[...], HI))
    g = _dot(jax.nn.sigmoid(x), g2_ref[...], HI)
    kk = k * kk_ref[...]
    kk = kk / jnp.maximum(jnp.sqrt(seg_sum(kk * kk)), 1e-12)
    k = k * (1.0 + (a - 1.0) * ka_ref[...])
    be = kk * a
    if n_valid < c:
        ok = _iota((c, LANES), 0) < n_valid
        logw = jnp.where(ok, logw, 0.0)
        kk = jnp.where(ok, kk, 0.0)
        be = jnp.where(ok, be, 0.0)
        k = jnp.where(ok, k, 0.0)
        v = jnp.where(ok, v, 0.0)

    ti = _iota((c, c), 0)
    si = _iota((c, c), 1)
    incl = si <= ti
    strict = si < ti
    cum = _dot(incl.astype(F32), logw, HI)
    tot = cum[c - 1:c, :]
    ah = -kk * jnp.exp(cum - logw)
    rh = r * jnp.exp(cum)
    ieg = jnp.exp(-cum)
    bc = be * ieg
    kc = k * ieg
    etail = jnp.exp(tot - cum)
    bt = be * etail
    kt = k * etail

    lane_c = _iota((1, LANES), 1)
    masks = [(lane_c < half).astype(F32), (lane_c >= half).astype(F32)]
    ar = jnp.concatenate([ah, rh], axis=0)
    eye = (si == ti).astype(F32)
    n_doub = max(1, math.ceil(math.log2(c)))
    s_prev = s_scr[...]
    y0 = _dot_nt(ar, s_prev, HI)
    lakv = jnp.zeros((c, LANES), F32)
    mrkv = jnp.zeros((c, LANES), F32)
    tmats, mrbs = [], []
    for mh in masks:
        arh = ar * mh
        g1 = _dot_nt(arh, bc, HI)
        g2 = _dot_nt(arh, kc, HI)
        xm = jnp.where(strict, g1[:c], 0.0)
        tm = eye + xm
        xp = xm
        for _ in range(n_doub - 1):
            xp = _dot(xp, xp, HI)
            tm = tm + _dot(tm, xp, HI)
        tmats.append(tm)
        mrbs.append(jnp.where(incl, g1[c:], 0.0))
        vh = v * mh
        lakv = lakv + _dot(jnp.where(strict, g2[:c], 0.0), vh, HI)
        mrkv = mrkv + _dot(jnp.where(incl, g2[c:], 0.0), vh, HI)
    y = y0[:c] + lakv
    p = _dot(tmats[0], y * masks[0], HI) + _dot(tmats[1], y * masks[1], HI)
    o = (y0[c:] + mrkv + _dot(mrbs[0], p * masks[0], HI) + _dot(mrbs[1], p * masks[1], HI))
    pv = jnp.concatenate([p, v], axis=0)
    bkt = jnp.concatenate([bt, kt], axis=0)
    eye_l = (lane == row).astype(F32)
    pv_t = _dot_nt(eye_l, pv, HI)
    s_new = (s_prev * jnp.exp(tot) + _dot(pv_t, bkt, HI)) * ones_bd
    s_scr[...] = s_new

    @pl.when(ci == n_chunks - 1)
    def _():
        st_ref[0, 0] = s_new

    inv_n = 1.0 / half
    mean = seg_sum(o) * inv_n
    d = o - mean
    var = seg_sum(d * d) * inv_n
    on = d * lax.rsqrt(var + GN_EPS) * gg_ref[...] + gb_ref[...]
    bonus = seg_sum(r * k * rk_ref[...]) * v
    o_ref[0] = ((on + bonus) * g).astype(o_ref.dtype)


def _rwkv(proj3, shift0, s0, mu, w0, w2, a0, a2, g2, k_k, k_a, r_k, gn_g, gn_b, *, c, n_valid):
    b, lp, _ = proj3.shape
    cw = w0.shape[-1]
    npair = cw // LANES
    heads = cw // RWKV_HEAD
    lw = mu.shape[-1] - 3 * cw
    assert lw % LANES == 0 and (3 * cw) % lw == 0 and lp % c == 0
    n_chunks = lp // c
    wl_, al_, gl_ = w2.shape[0], a2.shape[0], g2.shape[0]
    w2p = jnp.zeros((lw, cw), F32).at[:wl_].set(w2)
    a2p = jnp.zeros((lw, cw), F32).at[wl_:wl_ + al_].set(a2)
    g2p = jnp.zeros((lw, cw), F32).at[wl_ + al_:].set(g2)
    s0p = s0.reshape(b, npair, 2, RWKV_HEAD, RWKV_HEAD)
    z = jnp.zeros_like(s0p[:, :, 0])
    s0bd = jnp.concatenate([jnp.concatenate([s0p[:, :, 0], z], axis=-1),
                            jnp.concatenate([z, s0p[:, :, 1]], axis=-1)], axis=-2)
    sh3 = shift0.reshape(b, 1, -1)
    mu2 = mu.reshape(1, -1)
    row = lambda t: t.reshape(1, cw)
    xoff = (3 * cw) // lw

    seg = lambda s: pl.BlockSpec((1, c, LANES), lambda i, j, t: (i, t, s * npair + j))
    sseg = lambda s: pl.BlockSpec((1, 1, LANES), lambda i, j, t: (i, 0, s * npair + j))
    mseg = lambda s: pl.BlockSpec((1, LANES), lambda i, j, t: (0, s * npair + j))
    vec = pl.BlockSpec((1, LANES), lambda i, j, t: (0, j))
    lora = pl.BlockSpec((lw, LANES), lambda i, j, t: (0, j))
    in_specs = [seg(0), seg(1), seg(2), pl.BlockSpec((1, c, lw), lambda i, j, t: (i, t, xoff)),
                sseg(0), sseg(1), sseg(2), pl.BlockSpec((1, 1, lw), lambda i, j, t: (i, 0, xoff)),
                mseg(0), mseg(1), mseg(2), pl.BlockSpec((1, lw), lambda i, j, t: (0, xoff)),
                vec, vec, vec, vec, vec, vec, vec, lora, lora, lora,
                pl.BlockSpec((1, 1, LANES, LANES), lambda i, j, t: (i, j, 0, 0))]
    o, st = pl.pallas_call(
        functools.partial(_rwkv_kernel, c=c, n_valid=n_valid, n_chunks=n_chunks),
        grid=(b, npair, n_chunks),
        in_specs=in_specs,
        out_specs=[pl.BlockSpec((1, c, LANES), lambda i, j, t: (i, t, j)),
                   pl.BlockSpec((1, 1, LANES, LANES), lambda i, j, t: (i, j, 0, 0))],
        out_shape=[jax.ShapeDtypeStruct((b, lp, cw), BF16),
                   jax.ShapeDtypeStruct((b, npair, LANES, LANES), F32)],
        scratch_shapes=[pltpu.VMEM((LANES, LANES), F32),
                        pltpu.VMEM((c + 8, LANES), F32), pltpu.VMEM((c + 8, LANES), F32),
                        pltpu.VMEM((c + 8, LANES), F32), pltpu.VMEM((c + 8, lw), F32)],
        compiler_params=_params("parallel", "parallel", "arbitrary"),
        name="rwkv7_chunk",
    )(proj3, proj3, proj3, proj3, sh3, sh3, sh3, sh3, mu2, mu2, mu2, mu2,
      row(w0), row(a0), row(k_k), row(k_a), row(r_k), row(gn_g), row(gn_b), w2p, a2p, g2p, s0bd)
    st = jnp.stack([st[:, :, :RWKV_HEAD, :RWKV_HEAD], st[:, :, RWKV_HEAD:, RWKV_HEAD:]], axis=2)
    return o, st.reshape(b, heads, RWKV_HEAD, RWKV_HEAD)


def _logf_kernel(z_ref, bf_ref, lf_ref, cc_ref, cr_ref, *, rb):
    l, h = z_ref.shape[1], z_ref.shape[2]
    ti = _iota((rb, rb), 0)
    si = _iota((rb, rb), 1)
    tri = (si <= ti).astype(F32)
    carry = jnp.zeros((1, h), F32)
    for i in range(l // rb):
        sl = slice(i * rb, (i + 1) * rb)
        lf = jax.nn.log_sigmoid(z_ref[0, sl, :] + bf_ref[...])
        lf_ref[0, sl, :] = lf
        cblk = carry + _dot(tri, lf, HI)
        cc_ref[0, sl, :] = cblk
        carry = cblk[rb - 1:rb, :]
    eye = (_iota((h, h), 0) == _iota((h, h), 1)).astype(F32)
    cr_ref[0] = _dot_nt(eye, cc_ref[0], HI)


def _logf(z3, b_forget):
    b, l, h = z3.shape
    rb = _largest_divisor(l, 512, 8)
    blk = pl.BlockSpec((1, l, h), lambda i: (i, 0, 0))
    return pl.pallas_call(
        functools.partial(_logf_kernel, rb=rb),
        grid=(b,),
        in_specs=[blk, pl.BlockSpec((1, h), lambda i: (0, 0))],
        out_specs=[blk, blk, pl.BlockSpec((1, h, l), lambda i: (i, 0, 0))],
        out_shape=[jax.ShapeDtypeStruct((b, l, h), F32), jax.ShapeDtypeStruct((b, l, h), F32),
                   jax.ShapeDtypeStruct((b, h, l), F32)],
        compiler_params=_params("parallel"),
        name="fox_logf_cumsum",
    )(z3, b_forget.reshape(1, h))


def _fox_prompt_kernel(q_ref, k_ref, v_ref, cc_ref, cr_ref, o_ref, *, tq):
    l = q_ref.shape[1]
    nh = cc_ref.shape[2]
    h = pl.program_id(1)
    scale = FOX_HEAD ** -0.5
    sel = (_iota((nh, LANES), 0) == h).astype(F32)
    cq_all = _dot(cc_ref[0], sel, HI)
    ck = cr_ref[0, pl.ds(h, 1), :]
    kb = k_ref[0].astype(BF16)
    vb = v_ref[0].astype(BF16)
    for i in range(l // tq):
        nk = (i + 1) * tq
        rows = slice(i * tq, nk)
        s = _dot_nt(q_ref[0, rows, :].astype(BF16), kb[:nk]) * scale
        s = s + cq_all[rows, 0:1] - ck[:, :nk]
        qpos = _iota((tq, nk), 0) + i * tq
        kpos = _iota((tq, nk), 1)
        s = jnp.where(qpos >= kpos, s, NEG_INF)
        m = jnp.max(s, axis=-1, keepdims=True)
        p = jnp.exp(s - m)
        den = jnp.sum(p, axis=-1, keepdims=True)
        o = _dot(p.astype(BF16), vb[:nk]) / den
        o_ref[0, rows, :] = o.astype(o_ref.dtype)


def _fox_prompt(proj3, q_off, c_col, c_row):
    b, l, _ = proj3.shape
    nh = c_col.shape[2]
    tq = _largest_divisor(l, 384, 8)
    blk = lambda s: pl.BlockSpec((1, l, LANES), lambda i, j: (i, 0, q_off + s * nh + j))
    return pl.pallas_call(
        functools.partial(_fox_prompt_kernel, tq=tq),
        grid=(b, nh),
        in_specs=[blk(0), blk(1), blk(2),
                  pl.BlockSpec((1, l, nh), lambda i, j: (i, 0, 0)),
                  pl.BlockSpec((1, nh, l), lambda i, j: (i, 0, 0))],
        out_specs=pl.BlockSpec((1, l, LANES), lambda i, j: (i, 0, j)),
        out_shape=jax.ShapeDtypeStruct((b, l, nh * LANES), BF16),
        compiler_params=_params("parallel", "parallel"),
        name="fox_prompt_attention",
    )(proj3, proj3, proj3, c_col, c_row)


def _fox_sample_kernel(pt_ref, q_ref, kn_ref, vn_ref, cc_ref, cr_ref, kp_ref, vp_ref, lfp_ref,
                       o_ref, wq_scr, m_scr, l_scr, acc_scr, suf_scr, *, t_new, n_pages):
    pi = pl.program_id(1)
    nh = cc_ref.shape[2]
    width = nh * FOX_HEAD
    rows = t_new * nh
    scale = FOX_HEAD ** -0.5
    head_of_row = _iota((rows, width), 0) % nh
    head_of_col = _iota((rows, width), 1) // FOX_HEAD
    diag = head_of_row == head_of_col

    @pl.when(pi == 0)
    def _():
        qrep = jnp.concatenate(
            [jnp.broadcast_to(q_ref[0, t:t + 1, :], (nh, width)) for t in range(t_new)], axis=0)
        wq_scr[...] = jnp.where(diag, qrep, 0.0).astype(BF16)
        m_scr[...] = jnp.full(m_scr.shape, NEG_INF, F32)
        l_scr[...] = jnp.zeros(l_scr.shape, F32)
        acc_scr[...] = jnp.zeros(acc_scr.shape, F32)
        suf_scr[...] = jnp.zeros(suf_scr.shape, F32)

    cnew = jnp.concatenate([cr_ref[0, :, t:t + 1] for t in range(t_new)], axis=0)

    def online(s, vals):
        m_old = m_scr[...]
        m_new = jnp.maximum(m_old, jnp.max(s, axis=-1, keepdims=True))
        corr = jnp.exp(m_old - m_new)
        pe = jnp.exp(s - m_new)
        l_scr[...] = l_scr[...] * corr + jnp.sum(pe, axis=-1, keepdims=True)
        acc_scr[...] = acc_scr[...] * corr + _dot(pe.astype(BF16), vals)
        m_scr[...] = m_new

    eye_h = (_iota((nh, nh), 0) == _iota((nh, nh), 1)).astype(F32)
    lf_t = _dot_nt(eye_h, lfp_ref[0], HI)
    upper = (_iota((PAGE_SIZE, PAGE_SIZE), 0) > _iota((PAGE_SIZE, PAGE_SIZE), 1)).astype(F32)
    tail = suf_scr[...] + _dot(lf_t, upper, HI)
    suf_scr[...] = suf_scr[...] + jnp.sum(lf_t, axis=-1, keepdims=True)
    tail_rows = jnp.concatenate([tail] * t_new, axis=0)
    kpage = kp_ref[0].astype(BF16)
    vpage = vp_ref[0].astype(BF16)
    s = _dot_nt(wq_scr[...], kpage) * scale + tail_rows + cnew
    online(s, vpage)

    @pl.when(pi == n_pages - 1)
    def _():
        tp = kn_ref.shape[1]
        sn = _dot_nt(wq_scr[...], kn_ref[0].astype(BF16)) * scale
        ckey = jnp.concatenate([cr_ref[0]] * t_new, axis=0)
        sn = sn + cnew - ckey
        tq = _iota((rows, tp), 0) // nh
        tk = _iota((rows, tp), 1)
        sn = jnp.where(tk <= tq, sn, NEG_INF)
        online(sn, vn_ref[0].astype(BF16))
        res = jnp.where(diag, acc_scr[...] / l_scr[...], 0.0)
        o_ref[0] = jnp.sum(res.reshape(t_new, nh, width), axis=1).astype(o_ref.dtype)


def _fox_sample(q, k_new, v_new, c_col, c_row, k_pool, v_pool, lf_pool, page_table):
    bd, t_new, width = q.shape
    nh = c_col.shape[2]
    n_pages = page_table.shape[1]
    tp = 8 * pl.cdiv(t_new, 8)
    pad = lambda a: jnp.pad(a, ((0, 0), (0, tp - t_new), (0, 0)))
    c_row = jnp.pad(c_row, ((0, 0), (0, 0), (0, tp - t_new)))
    n_phys = k_pool.shape[0]
    k_pool = k_pool.reshape(n_phys, PAGE_SIZE, width)
    v_pool = v_pool.reshape(n_phys, PAGE_SIZE, width)
    rows = t_new * nh
    last = n_pages - 1
    full = lambda shape: pl.BlockSpec((1,) + shape, lambda i, p, pt: (i, 0, 0))
    page = lambda w: pl.BlockSpec((1, PAGE_SIZE, w), lambda i, p, pt: (pt[i, last - p], 0, 0))
    grid_spec = pltpu.PrefetchScalarGridSpec(
        num_scalar_prefetch=1,
        grid=(bd, n_pages),
        in_specs=[full((t_new, width)), full((tp, width)), full((tp, width)),
                  full((t_new, nh)), full((nh, tp)), page(width), page(width), page(nh)],
        out_specs=full((t_new, width)),
        scratch_shapes=[pltpu.VMEM((rows, width), BF16), pltpu.VMEM((rows, 1), F32),
                        pltpu.VMEM((rows, 1), F32), pltpu.VMEM((rows, width), F32),
                        pltpu.VMEM((nh, 1), F32)],
    )
    return pl.pallas_call(
        functools.partial(_fox_sample_kernel, t_new=t_new, n_pages=n_pages),
        grid_spec=grid_spec,
        out_shape=jax.ShapeDtypeStruct((bd, t_new, width), BF16),
        compiler_params=_params("parallel", "arbitrary"),
        name="fox_sample_attention",
    )(page_table, q, pad(k_new), pad(v_new), c_col, c_row, k_pool, v_pool, lf_pool)


def _conv_kernel(bg_ref, cg_ref, h_ref, w_ref, buf_ref, y_ref, nb_ref, u_scr, *, tt, n_t):
    ti = pl.program_id(2)

    @pl.when(ti == 0)
    def _():
        u_scr[6:8, :] = buf_ref[0]

    u = cg_ref[0] * h_ref[0]
    u_scr[8:8 + tt, :] = u
    y = (w_ref[0:1, :] * u_scr[6:6 + tt, :] + w_ref[1:2, :] * u_scr[7:7 + tt, :]
         + w_ref[2:3, :] * u)
    y_ref[0] = (bg_ref[0] * y).astype(y_ref.dtype)
    tail = u_scr[6 + tt:8 + tt, :]
    u_scr[6:8, :] = tail

    @pl.when(ti == n_t - 1)
    def _():
        nb_ref[0] = tail


def _short_conv(proj3, buf0, conv_w):
    b, l, d3 = proj3.shape
    d = d3 // 3
    tc = _largest_divisor(d, 512, LANES)
    tt = _largest_divisor(l, 704, 8)
    n_t = l // tt
    nc = d // tc
    seg = lambda s: pl.BlockSpec((1, tt, tc), lambda i, j, t: (i, t, s * nc + j))
    return pl.pallas_call(
        functools.partial(_conv_kernel, tt=tt, n_t=n_t),
        grid=(b, nc, n_t),
        in_specs=[seg(0), seg(1), seg(2),
                  pl.BlockSpec((conv_w.shape[0], tc), lambda i, j, t: (0, j)),
                  pl.BlockSpec((1, 2, tc), lambda i, j, t: (i, 0, j))],
        out_specs=[pl.BlockSpec((1, tt, tc), lambda i, j, t: (i, t, j)),
                   pl.BlockSpec((1, 2, tc), lambda i, j, t: (i, 0, j))],
        out_shape=[jax.ShapeDtypeStruct((b, l, d), BF16), jax.ShapeDtypeStruct((b, 2, d), F32)],
        scratch_shapes=[pltpu.VMEM((tt + 8, tc), F32)],
        compiler_params=_params("parallel", "parallel", "arbitrary"),
        name="short_conv",
    )(proj3, proj3, proj3, conv_w, buf0)


def _trunk(h3, s0, shift0, conv0, fox_ctx, wts, *, prompt):
    (norm_mix, norm_mlp, w_in_even, b_forget, rwkv_mu, rwkv_w0, rwkv_w2, rwkv_a0, rwkv_a2, rwkv_g2,
     rwkv_k_k, rwkv_k_a, rwkv_r_k, rwkv_gn_g, rwkv_gn_b, w_out_even, w_in_odd, conv_w, w_out_odd,
     w_up, w_down) = wts
    b, l, d = h3.shape
    m = b * l
    h = h3.reshape(m, d)
    cw = rwkv_w0.shape[-1]
    rproj = rwkv_mu.shape[-1]
    fw = d - cw
    nh = fw // FOX_HEAD
    n_main = rproj + 3 * fw

    hn = _rmsnorm(h, norm_mix[0], BF16)
    w_in = w_in_even[0]
    proj = _matmul(hn, w_in, n_cols=n_main, tn=256, tk=4096)
    z = _matmul(hn, w_in[:, n_main:], tn=nh, tk=4096)
    proj3 = proj.reshape(b, l, n_main)
    lf, c_col, c_row = _logf(z.reshape(b, l, nh), b_forget[0])
    q_off = rproj // LANES
    if prompt:
        o_r, s_t = _rwkv(proj3, shift0[0], s0[0], rwkv_mu[0], rwkv_w0[0], rwkv_w2[0], rwkv_a0[0],
                         rwkv_a2[0], rwkv_g2[0], rwkv_k_k[0], rwkv_k_a[0], rwkv_r_k[0],
                         rwkv_gn_g[0], rwkv_gn_b[0], c=48, n_valid=48)
        o_f = _fox_prompt(proj3, q_off, c_col, c_row)
    else:
        lp = 8 * pl.cdiv(l, 8)
        projp = jnp.pad(proj3[:, :, :rproj], ((0, 0), (0, lp - l), (0, 0)))
        o_r, s_t = _rwkv(projp, shift0[0], s0[0], rwkv_mu[0], rwkv_w0[0], rwkv_w2[0], rwkv_a0[0],
                         rwkv_a2[0], rwkv_g2[0], rwkv_k_k[0], rwkv_k_a[0], rwkv_r_k[0],
                         rwkv_gn_g[0], rwkv_gn_b[0], c=lp, n_valid=l)
        o_r = o_r[:, :l]
        k_pool, v_pool, lf_pool, page_table = fox_ctx
        o_f = _fox_sample(proj3[:, :, rproj:rproj + fw], proj3[:, :, rproj + fw:rproj + 2 * fw],
                          proj3[:, :, rproj + 2 * fw:], c_col, c_row, k_pool[0], v_pool[0],
                          lf_pool[0], page_table)
    sh_t = proj3[:, l - 1, :rproj]
    k_out = proj3[:, :, rproj + fw:rproj + 2 * fw].reshape(b, l, nh, FOX_HEAD)
    v_out = proj3[:, :, rproj + 2 * fw:].reshape(b, l, nh, FOX_HEAD)
    mix_in = jnp.concatenate([o_r, o_f], axis=-1).reshape(m, d)
    h = _matmul(mix_in, w_out_even[0], tn=512, tk=2048, epilogue="resid", resid=h)
    hn = _rmsnorm(h, norm_mlp[0], BF16)
    u = _matmul(hn, w_up[0], tn=512, tk=2048, epilogue="relu2", out_dtype=BF16)
    h = _matmul(u, w_down[0], tn=512, tk=2048, epilogue="resid", resid=h)

    hn = _rmsnorm(h, norm_mix[1], BF16)
    proj1 = _matmul(hn, w_in_odd[0], tn=512, tk=2048)
    y, buf = _short_conv(proj1.reshape(b, l, 3 * d), conv0[0], conv_w[0])
    h = _matmul(y.reshape(m, d), w_out_odd[0], tn=512, tk=2048, epilogue="resid", resid=h)
    hn = _rmsnorm(h, norm_mlp[1], BF16)
    u = _matmul(hn, w_up[1], tn=512, tk=2048, epilogue="relu2", out_dtype=BF16)
    h = _matmul(u, w_down[1], tn=512, tk=2048, epilogue="resid", resid=h)
    return (h.reshape(b, l, d), s_t[None], sh_t[None], k_out[None], v_out[None], lf[None], buf[None])


def kernel(x_prompt, x_sample, state_rwkv, state_rwkv_shift, cache_fox_k, cache_fox_v, cache_fox_logf, state_conv, page_table, meta_tokens, norm_mix, norm_mlp, norm_final, w_in_even, b_forget, rwkv_mu, rwkv_w0, rwkv_w2, rwkv_a0, rwkv_a2, rwkv_g2, rwkv_k_k, rwkv_k_a, rwkv_r_k, rwkv_gn_g, rwkv_gn_b, w_out_even, w_in_odd, conv_w, w_out_odd, w_up, w_down):
    wts = (norm_mix, norm_mlp, w_in_even, b_forget, rwkv_mu, rwkv_w0, rwkv_w2, rwkv_a0, rwkv_a2,
           rwkv_g2, rwkv_k_k, rwkv_k_a, rwkv_r_k, rwkv_gn_g, rwkv_gn_b, w_out_even, w_in_odd, conv_w,
           w_out_odd, w_up, w_down)
    bp, _, d = x_prompt.shape
    n_even, _, heads, hd, _ = state_rwkv.shape
    n_odd = state_conv.shape[0]
    rproj = state_rwkv_shift.shape[-1]

    h0 = jnp.concatenate([jnp.broadcast_to(meta_tokens[None], (bp, N_META, d)), x_prompt], axis=1)
    zs = jnp.zeros((n_even, bp, heads, hd, hd), F32)
    zsh = jnp.zeros((n_even, bp, rproj), F32)
    zconv = jnp.zeros((n_odd, bp, state_conv.shape[2], d), F32)
    hp, p_s, p_sh, p_k, p_v, p_lf, p_conv = _trunk(h0, zs, zsh, zconv, None, wts, prompt=True)
    y_prompt = _final_norm_prompt(hp, norm_final)

    hs, s_s, s_sh, s_k, s_v, s_lf, s_conv = _trunk(
        x_sample, state_rwkv, state_rwkv_shift, state_conv,
        (cache_fox_k, cache_fox_v, cache_fox_logf, page_table), wts, prompt=False)
    bs, ts, _ = x_sample.shape
    y_sample = _rmsnorm(hs.reshape(bs * ts, d), norm_final, F32).reshape(bs, ts, d)
    return (y_prompt, y_sample, p_s, p_sh, p_k, p_v, p_lf, p_conv, s_s, s_sh, s_k, s_v, s_lf, s_conv)
```

```python
import functools
import math

import jax
import jax.numpy as jnp
from jax import lax
from jax.experimental import pallas as pl
from jax.experimental.pallas import tpu as pltpu

F32 = jnp.float32
BF16 = jnp.bfloat16
HI = lax.Precision.HIGHEST

RMS_EPS = 1e-6
GN_EPS = 64e-5
NEG_INF = -1e30
N_META = 16
RWKV_HEAD = 64
FOX_HEAD = 128
PAGE_SIZE = 128
LANES = 128
VMEM_LIMIT = 60 * 1024 * 1024


def _largest_divisor(n, cap, mult):
    best = None
    for d in range(mult, min(n, cap) + 1, mult):
        if n % d == 0:
            best = d
    return n if best is None else best


def _params(*sem):
    return pltpu.CompilerParams(dimension_semantics=sem, vmem_limit_bytes=VMEM_LIMIT)


def _dot(a, b, precision=None):
    return jnp.dot(a, b, precision=precision, preferred_element_type=F32)


def _dot_nt(a, b, precision=None):
    return lax.dot_general(a, b, (((1,), (1,)), ((), ())), precision=precision,
                           preferred_element_type=F32)


def _dot_tn(a, b, precision=None):
    return lax.dot_general(a, b, (((0,), (0,)), ((), ())), precision=precision,
                           preferred_element_type=F32)


def _iota(shape, dim):
    return lax.broadcasted_iota(jnp.int32, shape, dim)


def _split(x, pieces=2):
    out = []
    for i in range(pieces):
        p = x.astype(BF16)
        out.append(p)
        if i + 1 < pieces:
            x = x - p.astype(F32)
    return out


def _dot3(a, b, f=_dot):
    return f(a[0], b[0]) + (f(a[0], b[1]) + f(a[1], b[0]))


def _dot_exact(pieces, other, f=_dot):
    acc = f(pieces[0], other)
    for p in pieces[1:]:
        acc = acc + f(p, other)
    return acc


def _rmsnorm_kernel(x_ref, g_ref, o_ref):
    x = x_ref[...]
    ms = jnp.mean(x * x, axis=-1, keepdims=True)
    o_ref[...] = (x * lax.rsqrt(ms + RMS_EPS) * g_ref[...]).astype(o_ref.dtype)


def _rmsnorm(x2d, g, out_dtype):
    m, d = x2d.shape
    tr = _largest_divisor(m, 512, 16)
    return pl.pallas_call(
        _rmsnorm_kernel,
        grid=(m // tr,),
        in_specs=[pl.BlockSpec((tr, d), lambda i: (i, 0)),
                  pl.BlockSpec((1, d), lambda i: (0, 0))],
        out_specs=pl.BlockSpec((tr, d), lambda i: (i, 0)),
        out_shape=jax.ShapeDtypeStruct((m, d), out_dtype),
        compiler_params=_params("parallel"),
        name="rmsnorm",
    )(x2d, g.reshape(1, d))


def _final_norm_prompt_kernel(a_ref, b_ref, g_ref, o_ref):
    x = jnp.concatenate([a_ref[0, N_META:, :], b_ref[0]], axis=0)
    ms = jnp.mean(x * x, axis=-1, keepdims=True)
    o_ref[0] = x * lax.rsqrt(ms + RMS_EPS) * g_ref[...]


def _final_norm_prompt(h3, g):
    b, l, d = h3.shape
    s = l - N_META
    tr = _largest_divisor(s, 256, N_META)
    sub = tr // N_META
    return pl.pallas_call(
        _final_norm_prompt_kernel,
        grid=(b, s // tr),
        in_specs=[pl.BlockSpec((1, tr, d), lambda i, j: (i, j, 0)),
                  pl.BlockSpec((1, N_META, d), lambda i, j: (i, (j + 1) * sub, 0)),
                  pl.BlockSpec((1, d), lambda i, j: (0, 0))],
        out_specs=pl.BlockSpec((1, tr, d), lambda i, j: (i, j, 0)),
        out_shape=jax.ShapeDtypeStruct((b, s, d), F32),
        compiler_params=_params("parallel", "parallel"),
        name="final_norm_prompt",
    )(h3, h3, g.reshape(1, d))


def _mm_kernel(*refs, nk, epilogue):
    if epilogue == "resid":
        x_ref, w_ref, r_ref, o_ref = refs[:4]
    else:
        x_ref, w_ref, o_ref = refs[:3]
        r_ref = None
    acc_ref = refs[-1] if nk > 1 else None

    def finish(acc):
        if epilogue == "relu2":
            acc = jnp.square(jnp.maximum(acc, 0.0))
        elif epilogue == "resid":
            acc = r_ref[...] + acc
        o_ref[...] = acc.astype(o_ref.dtype)

    part = _dot(x_ref[...], w_ref[...].astype(BF16))
    if nk == 1:
        finish(part)
    else:
        k = pl.program_id(2)

        @pl.when(k == 0)
        def _():
            acc_ref[...] = part

        @pl.when(jnp.logical_and(k > 0, k < nk - 1))
        def _():
            acc_ref[...] += part

        @pl.when(k == nk - 1)
        def _():
            finish(acc_ref[...] + part)


def _matmul(x, w, layer, *, n_lo=0, n_cols=None, tn, tk, epilogue="plain", resid=None,
            out_dtype=F32):
    m, kdim = x.shape
    n_cols = w.shape[2] - n_lo if n_cols is None else n_cols
    tn = _largest_divisor(math.gcd(n_cols, n_lo), tn, LANES)
    tk = _largest_divisor(kdim, tk, LANES)
    assert n_lo % tn == 0 and n_cols % tn == 0
    tm = _largest_divisor(m, 2064, 16)
    nk = kdim // tk
    off = n_lo // tn
    in_specs = [pl.BlockSpec((tm, tk), lambda i, j, k: (i, k)),
                pl.BlockSpec((None, tk, tn), lambda i, j, k: (layer, k, j + off))]
    args = [x, w]
    if epilogue == "resid":
        in_specs.append(pl.BlockSpec((tm, tn), lambda i, j, k: (i, j)))
        args.append(resid)
    return pl.pallas_call(
        functools.partial(_mm_kernel, nk=nk, epilogue=epilogue),
        grid=(m // tm, n_cols // tn, nk),
        in_specs=in_specs,
        out_specs=pl.BlockSpec((tm, tn), lambda i, j, k: (i, j)),
        out_shape=jax.ShapeDtypeStruct((m, n_cols), out_dtype),
        scratch_shapes=[pltpu.VMEM((tm, tn), F32)] if nk > 1 else [],
        compiler_params=_params("parallel", "parallel", "arbitrary"),
        name="matmul_" + epilogue,
    )(*args)


def _rwkv_kernel(pr_ref, pk_ref, pv_ref, px_ref, sr_ref, sk_ref, sv_ref, sx_ref,
                 mur_ref, muk_ref, muv_ref, mux_ref, w0_ref, a0_ref, kk_ref, ka_ref, rk_ref,
                 gg_ref, gb_ref, w2_ref, a2_ref, g2_ref, s0_ref,
                 o_ref, st_ref,
                 s_scr, br_scr, bk_scr, bv_scr, bx_scr, *, c, g, nb, n_valid, n_chunks):
    ci = pl.program_id(2)
    half = RWKV_HEAD
    w = g * LANES
    nh = 2 * g
    n = nh * c

    @pl.when(ci == 0)
    def _():
        s_scr[...] = s0_ref[...]
        br_scr[:, 7:8, :] = sr_ref[...]
        bk_scr[:, 7:8, :] = sk_ref[...]
        bv_scr[:, 7:8, :] = sv_ref[...]
        bx_scr[:, 7:8, :] = sx_ref[...]

    ones_w = ((_iota((w, w), 0) // half) == (_iota((w, w), 1) // half)).astype(BF16)
    ones_p = ((_iota((LANES, LANES), 0) // half) == (_iota((LANES, LANES), 1) // half)).astype(F32)
    tri = (_iota((c, c), 1) <= _iota((c, c), 0)).astype(BF16)
    lane_c = _iota((1, LANES), 1)
    m2 = [(lane_c < half).astype(F32), (lane_c >= half).astype(F32)]
    ri = _iota((n, n), 0)
    cj = _iota((n, n), 1)
    rblk = sum((ri >= b * c).astype(jnp.int32) for b in range(1, nh))
    cblk = sum((cj >= b * c).astype(jnp.int32) for b in range(1, nh))
    same = rblk == cblk
    strict = jnp.logical_and(same, (cj - cblk * c) < (ri - rblk * c))
    incl = jnp.logical_and(same, (cj - cblk * c) <= (ri - rblk * c))
    eye = (ri == cj).astype(F32)
    w2b = w2_ref[...].astype(BF16)
    a2b = a2_ref[...].astype(BF16)
    g2b = g2_ref[...].astype(BF16)
    pair = lambda t, p: t[:, p * LANES:(p + 1) * LANES]

    def seg_sum(t):
        return _dot_exact(_split(t, 3), ones_w)

    def stack(t):
        return jnp.concatenate([pair(t, p) * m2[q] for p in range(g) for q in range(2)], axis=0)

    def sequence(bi):
        def shifted(p_ref, buf, mu_ref):
            p = p_ref[bi]
            buf[bi, 8:8 + c, :] = p
            prev = buf[bi, 7:7 + c, :]
            buf[bi, 7:8, :] = p[c - 1:c, :]
            return p + (prev - p) * mu_ref[...]

        r = shifted(pr_ref, br_scr, mur_ref)
        k = shifted(pk_ref, bk_scr, muk_ref)
        v = shifted(pv_ref, bv_scr, muv_ref)
        x = shifted(px_ref, bx_scr, mux_ref)

        wl = w0_ref[...] + _dot(jnp.tanh(x).astype(BF16), w2b)
        wl = -jax.nn.softplus(-wl) - 0.5
        logw = -jnp.exp(wl)
        a = jax.nn.sigmoid(a0_ref[...] + _dot(x.astype(BF16), a2b))
        gate = _dot(jax.nn.sigmoid(x).astype(BF16), g2b)
        kk = k * kk_ref[...]
        kk = kk / jnp.maximum(jnp.sqrt(seg_sum(kk * kk)), 1e-12)
        k = k * (1.0 + (a - 1.0) * ka_ref[...])
        be = kk * a
        if n_valid < c:
            ok = _iota((c, w), 0) < n_valid
            logw = jnp.where(ok, logw, 0.0)
            kk = jnp.where(ok, kk, 0.0)
            be = jnp.where(ok, be, 0.0)
            k = jnp.where(ok, k, 0.0)
            v = jnp.where(ok, v, 0.0)

        cum = _dot_exact(_split(logw, 3), tri, lambda p, t: _dot(t, p))
        tot = cum[c - 1:c, :]
        ah = -kk * jnp.exp(cum - logw)
        rh = r * jnp.exp(cum)
        ieg = jnp.exp(-cum)
        bc = be * ieg
        kc = k * ieg
        etail = jnp.exp(tot - cum)
        bt = be * etail
        kt = k * etail

        ars = _split(jnp.concatenate([stack(ah), stack(rh)], axis=0))
        vs = stack(v)
        gb = _dot3(ars, _split(stack(bc)), _dot_nt)
        gk = _dot3(ars, _split(stack(kc)), _dot_nt)
        xm = jnp.where(strict, gb[:n], 0.0)
        tm = eye + xm
        xp = _split(xm)
        for _ in range(max(1, math.ceil(math.log2(c))) - 1):
            xp = _split(_dot3(xp, xp))
            tm = tm + _dot3(_split(tm), xp)
        lm = jnp.concatenate([jnp.where(strict, gk[:n], 0.0), jnp.where(incl, gk[n:], 0.0)], axis=0)
        lmv = _dot3(_split(lm), _split(vs))
        mrb = jnp.where(incl, gb[n:], 0.0)

        y0 = [_dot3(_split(jnp.concatenate([pair(ah, p), pair(rh, p)], axis=0)),
                    _split(s_scr[bi, p]), _dot_nt) for p in range(g)]
        ys = jnp.concatenate([y0[p][:c] * m2[q] for p in range(g) for q in range(2)], axis=0) + lmv[:n]
        ps = _dot3(_split(tm), _split(ys))
        os_ = _dot3(_split(mrb), _split(ps)) + lmv[n:]
        o_parts = []
        for p in range(g):
            lo, mid, hi = 2 * p * c, (2 * p + 1) * c, (2 * p + 2) * c
            o_parts.append(y0[p][c:] + os_[lo:mid] + os_[mid:hi])
            pv = jnp.concatenate([ps[lo:mid] + ps[mid:hi], pair(v, p)], axis=0)
            bk = jnp.concatenate([pair(bt, p), pair(kt, p)], axis=0)
            upd = _dot3(_split(pv), _split(bk), _dot_tn)
            s_scr[bi, p] = (s_scr[bi, p] * jnp.exp(pair(tot, p)) + upd) * ones_p

        o = jnp.concatenate(o_parts, axis=1) if g > 1 else o_parts[0]
        inv_n = 1.0 / half
        mean = seg_sum(o) * inv_n
        d = o - mean
        var = seg_sum(d * d) * inv_n
        on = d * lax.rsqrt(var + GN_EPS) * gg_ref[...] + gb_ref[...]
        bonus = seg_sum(r * k * rk_ref[...]) * v
        o_ref[bi] = ((on + bonus) * gate).astype(o_ref.dtype)

    for bi in range(nb):
        sequence(bi)

    @pl.when(ci == n_chunks - 1)
    def _():
        st_ref[...] = s_scr[...]


def _rwkv(proj3, shift0, s0, mu, w0, w2, a0, a2, g2, k_k, k_a, r_k, gn_g, gn_b, *, c, n_valid,
          o_width):
    b, lp, _ = proj3.shape
    cw = w0.shape[-1]
    npair = cw // LANES
    heads = cw // RWKV_HEAD
    g = 2 if npair % 2 == 0 else 1
    w = g * LANES
    ng = npair // g
    lw = mu.shape[-1] - 3 * cw
    assert lw % LANES == 0 and (3 * cw) % lw == 0 and lp % c == 0
    n_chunks = lp // c
    wl_, al_ = w2.shape[0], a2.shape[0]
    w2p = jnp.zeros((lw, cw), F32).at[:wl_].set(w2)
    a2p = jnp.zeros((lw, cw), F32).at[wl_:wl_ + al_].set(a2)
    g2p = jnp.zeros((lw, cw), F32).at[wl_ + al_:].set(g2)
    s0p = s0.reshape(b, npair, 2, RWKV_HEAD, RWKV_HEAD)
    z = jnp.zeros_like(s0p[:, :, 0])
    s0bd = jnp.concatenate([jnp.concatenate([s0p[:, :, 0], z], axis=-1),
                            jnp.concatenate([z, s0p[:, :, 1]], axis=-1)], axis=-2)
    sh3 = shift0.reshape(b, 1, -1)
    mu2 = mu.reshape(1, -1)
    row = lambda t: t.reshape(1, cw)
    xoff = (3 * cw) // lw

    nb = 2 if b % 2 == 0 else 1
    seg = lambda s: pl.BlockSpec((nb, c, w), lambda i, j, t: (i, t, s * ng + j))
    sseg = lambda s: pl.BlockSpec((nb, 1, w), lambda i, j, t: (i, 0, s * ng + j))
    mseg = lambda s: pl.BlockSpec((1, w), lambda i, j, t: (0, s * ng + j))
    vec = pl.BlockSpec((1, w), lambda i, j, t: (0, j))
    lora = pl.BlockSpec((lw, w), lambda i, j, t: (0, j))
    state = pl.BlockSpec((nb, g, LANES, LANES), lambda i, j, t: (i, j, 0, 0))
    in_specs = [seg(0), seg(1), seg(2), pl.BlockSpec((nb, c, lw), lambda i, j, t: (i, t, xoff)),
                sseg(0), sseg(1), sseg(2), pl.BlockSpec((nb, 1, lw), lambda i, j, t: (i, 0, xoff)),
                mseg(0), mseg(1), mseg(2), pl.BlockSpec((1, lw), lambda i, j, t: (0, xoff)),
                vec, vec, vec, vec, vec, vec, vec, lora, lora, lora, state]
    o, st = pl.pallas_call(
        functools.partial(_rwkv_kernel, c=c, g=g, nb=nb, n_valid=n_valid, n_chunks=n_chunks),
        grid=(b // nb, ng, n_chunks),
        in_specs=in_specs,
        out_specs=[pl.BlockSpec((nb, c, w), lambda i, j, t: (i, t, j)), state],
        out_shape=[jax.ShapeDtypeStruct((b, lp, o_width), BF16),
                   jax.ShapeDtypeStruct((b, npair, LANES, LANES), F32)],
        scratch_shapes=[pltpu.VMEM((nb, g, LANES, LANES), F32),
                        pltpu.VMEM((nb, c + 8, w), F32), pltpu.VMEM((nb, c + 8, w), F32),
                        pltpu.VMEM((nb, c + 8, w), F32), pltpu.VMEM((nb, c + 8, lw), F32)],
        compiler_params=_params("parallel", "parallel", "arbitrary"),
        name="rwkv7_chunk",
    )(proj3, proj3, proj3, proj3, sh3, sh3, sh3, sh3, mu2, mu2, mu2, mu2,
      row(w0), row(a0), row(k_k), row(k_a), row(r_k), row(gn_g), row(gn_b), w2p, a2p, g2p, s0bd)
    st = jnp.stack([st[:, :, :RWKV_HEAD, :RWKV_HEAD], st[:, :, RWKV_HEAD:, RWKV_HEAD:]], axis=2)
    return o, st.reshape(b, heads, RWKV_HEAD, RWKV_HEAD)


def _logf_kernel(z_ref, bf_ref, lf_ref, cc_ref, cr_ref, *, rb):
    l, h = z_ref.shape[1], z_ref.shape[2]
    ti = _iota((rb, rb), 0)
    si = _iota((rb, rb), 1)
    tri = (si <= ti).astype(F32)
    carry = jnp.zeros((1, h), F32)
    for i in range(l // rb):
        sl = slice(i * rb, (i + 1) * rb)
        lf = jax.nn.log_sigmoid(z_ref[0, sl, :] + bf_ref[...])
        lf_ref[0, sl, :] = lf
        cblk = carry + _dot(tri, lf, HI)
        cc_ref[0, sl, :] = cblk
        carry = cblk[rb - 1:rb, :]
    eye = (_iota((h, h), 0) == _iota((h, h), 1)).astype(F32)
    cr_ref[0] = _dot_nt(eye, cc_ref[0], HI)


def _logf(z3, b_forget):
    b, l, h = z3.shape
    rb = _largest_divisor(l, 512, 8)
    blk = pl.BlockSpec((1, l, h), lambda i: (i, 0, 0))
    return pl.pallas_call(
        functools.partial(_logf_kernel, rb=rb),
        grid=(b,),
        in_specs=[blk, pl.BlockSpec((1, h), lambda i: (0, 0))],
        out_specs=[blk, blk, pl.BlockSpec((1, h, l), lambda i: (i, 0, 0))],
        out_shape=[jax.ShapeDtypeStruct((b, l, h), F32), jax.ShapeDtypeStruct((b, l, h), F32),
                   jax.ShapeDtypeStruct((b, h, l), F32)],
        compiler_params=_params("parallel"),
        name="fox_logf_cumsum",
    )(z3, b_forget.reshape(1, h))


def _fox_prompt_kernel(q_ref, k_ref, v_ref, cc_ref, cr_ref, buf_ref, o_ref, *, tq):
    del buf_ref
    l = q_ref.shape[1]
    nh = cc_ref.shape[2]
    h = pl.program_id(1)
    scale = FOX_HEAD ** -0.5
    sel = (_iota((nh, LANES), 0) == h).astype(F32)
    cq_all = _dot(cc_ref[0], sel, HI)
    ck = cr_ref[0, pl.ds(h, 1), :]
    kb = k_ref[0].astype(BF16)
    vb = v_ref[0].astype(BF16)
    for i in range(l // tq):
        nk = (i + 1) * tq
        rows = slice(i * tq, nk)
        s = _dot_nt(q_ref[0, rows, :].astype(BF16), kb[:nk]) * scale
        s = s + cq_all[rows, 0:1] - ck[:, :nk]
        qpos = _iota((tq, nk), 0) + i * tq
        kpos = _iota((tq, nk), 1)
        s = jnp.where(qpos >= kpos, s, NEG_INF)
        m = jnp.max(s, axis=-1, keepdims=True)
        p = jnp.exp(s - m)
        den = jnp.sum(p, axis=-1, keepdims=True)
        o = _dot(p.astype(BF16), vb[:nk]) / den
        o_ref[0, rows, :] = o.astype(o_ref.dtype)


def _fox_prompt(proj3, q_off, k3, v3, c_col, c_row, o_buf, o_off):
    b, l, _ = proj3.shape
    nh = c_col.shape[2]
    tq = _largest_divisor(l, 384, 8)
    blk = lambda off: pl.BlockSpec((1, l, LANES), lambda i, j: (i, 0, off + j))
    return pl.pallas_call(
        functools.partial(_fox_prompt_kernel, tq=tq),
        grid=(b, nh),
        in_specs=[blk(q_off), blk(0), blk(0),
                  pl.BlockSpec((1, l, nh), lambda i, j: (i, 0, 0)),
                  pl.BlockSpec((1, nh, l), lambda i, j: (i, 0, 0)),
                  pl.BlockSpec(memory_space=pl.ANY)],
        out_specs=blk(o_off),
        out_shape=jax.ShapeDtypeStruct(o_buf.shape, o_buf.dtype),
        input_output_aliases={5: 0},
        compiler_params=_params("parallel", "parallel"),
        name="fox_prompt_attention",
    )(proj3, k3, v3, c_col, c_row, o_buf)


def _fox_sample_kernel(pt_ref, q_ref, kn_ref, vn_ref, cc_ref, cr_ref, *rest, t_new, n_steps, pg):
    kp = rest[:pg]
    vp = rest[pg:2 * pg]
    lfp = rest[2 * pg:3 * pg]
    o_ref, wq_scr, up_scr, m_scr, l_scr, acc_scr, suf_scr = rest[3 * pg:]
    pi = pl.program_id(1)
    nh = cc_ref.shape[2]
    width = nh * FOX_HEAD
    rows = t_new * nh
    span = pg * PAGE_SIZE
    scale = FOX_HEAD ** -0.5
    head_of_row = _iota((rows, width), 0) % nh
    head_of_col = _iota((rows, width), 1) // FOX_HEAD
    diag = head_of_row == head_of_col

    @pl.when(pi == 0)
    def _():
        qrep = jnp.concatenate(
            [jnp.broadcast_to(q_ref[0, t:t + 1, :], (nh, width)) for t in range(t_new)], axis=0)
        wq_scr[...] = jnp.where(diag, qrep, 0.0).astype(BF16)
        up_scr[...] = (_iota((span, span), 0) > _iota((span, span), 1)).astype(BF16)
        m_scr[...] = jnp.full(m_scr.shape, NEG_INF, F32)
        l_scr[...] = jnp.zeros(l_scr.shape, F32)
        acc_scr[...] = jnp.zeros(acc_scr.shape, F32)
        suf_scr[...] = jnp.zeros(suf_scr.shape, F32)

    cnew = jnp.concatenate([cr_ref[0, :, t:t + 1] for t in range(t_new)], axis=0)

    def online(s, vals):
        m_old = m_scr[...]
        m_new = jnp.maximum(m_old, jnp.max(s, axis=-1, keepdims=True))
        corr = jnp.exp(m_old - m_new)
        pe = jnp.exp(s - m_new)
        l_scr[...] = l_scr[...] * corr + jnp.sum(pe, axis=-1, keepdims=True)
        pe = pe.astype(BF16)
        acc = acc_scr[...] * corr
        for i, val in enumerate(vals):
            n_i = val.shape[0]
            acc = acc + _dot(pe[:, i * n_i:(i + 1) * n_i], val)
        acc_scr[...] = acc
        m_scr[...] = m_new

    eye_h = (_iota((nh, nh), 0) == _iota((nh, nh), 1)).astype(BF16)
    lf = jnp.concatenate([r[0] for r in lfp], axis=0)
    lf_t = [_dot_nt(eye_h, p) for p in _split(lf, 3)]
    tail = suf_scr[...] + _dot_exact([p.astype(BF16) for p in lf_t], up_scr[...])
    suf_scr[...] = suf_scr[...] + jnp.sum(lf_t[0] + (lf_t[1] + lf_t[2]), axis=-1, keepdims=True)
    tail_rows = jnp.concatenate([tail] * t_new, axis=0)
    wq = wq_scr[...]
    qk = [_dot_nt(wq, r[0].astype(BF16)) for r in kp]
    s = (jnp.concatenate(qk, axis=1) if pg > 1 else qk[0]) * scale + tail_rows + cnew
    online(s, [r[0].astype(BF16) for r in vp])

    @pl.when(pi == n_steps - 1)
    def _():
        tp = kn_ref.shape[1]
        sn = _dot_nt(wq, kn_ref[0].astype(BF16)) * scale
        ckey = jnp.concatenate([cr_ref[0]] * t_new, axis=0)
        sn = sn + cnew - ckey
        tq = _iota((rows, tp), 0) // nh
        tk = _iota((rows, tp), 1)
        sn = jnp.where(tk <= tq, sn, NEG_INF)
        online(sn, [vn_ref[0].astype(BF16)])
        res = jnp.where(diag, acc_scr[...] / l_scr[...], 0.0)
        o_ref[0] = jnp.sum(res.reshape(t_new, nh, width), axis=1).astype(o_ref.dtype)


def _fox_sample(q, k_new, v_new, c_col, c_row, k_pool, v_pool, lf_pool, page_table):
    bd, t_new, width = q.shape
    nh = c_col.shape[2]
    n_pages = page_table.shape[1]
    pg = 4 if n_pages % 4 == 0 else 1
    n_steps = n_pages // pg
    tp = 8 * pl.cdiv(t_new, 8)
    pad = lambda a: jnp.pad(a, ((0, 0), (0, tp - t_new), (0, 0)))
    c_row = jnp.pad(c_row, ((0, 0), (0, 0), (0, tp - t_new)))
    n_phys = k_pool.shape[0]
    k_pool = k_pool.reshape(n_phys, PAGE_SIZE, width)
    v_pool = v_pool.reshape(n_phys, PAGE_SIZE, width)
    rows = t_new * nh
    span = pg * PAGE_SIZE
    full = lambda shape: pl.BlockSpec((1,) + shape, lambda i, p, pt: (i, 0, 0))

    def page(wd, slot):
        return pl.BlockSpec((1, PAGE_SIZE, wd),
                            lambda i, p, pt: (pt[i, n_pages - (p + 1) * pg + slot], 0, 0))

    slots = range(pg)
    grid_spec = pltpu.PrefetchScalarGridSpec(
        num_scalar_prefetch=1,
        grid=(bd, n_steps),
        in_specs=([full((t_new, width)), full((tp, width)), full((tp, width)),
                   full((t_new, nh)), full((nh, tp))]
                  + [page(width, s) for s in slots] + [page(width, s) for s in slots]
                  + [page(nh, s) for s in slots]),
        out_specs=full((t_new, width)),
        scratch_shapes=[pltpu.VMEM((rows, width), BF16), pltpu.VMEM((span, span), BF16),
                        pltpu.VMEM((rows, 1), F32), pltpu.VMEM((rows, 1), F32),
                        pltpu.VMEM((rows, width), F32), pltpu.VMEM((nh, 1), F32)],
    )
    return pl.pallas_call(
        functools.partial(_fox_sample_kernel, t_new=t_new, n_steps=n_steps, pg=pg),
        grid_spec=grid_spec,
        out_shape=jax.ShapeDtypeStruct((bd, t_new, width), BF16),
        compiler_params=_params("parallel", "arbitrary"),
        name="fox_sample_attention",
    )(page_table, q, pad(k_new), pad(v_new), c_col, c_row,
      *([k_pool] * pg), *([v_pool] * pg), *([lf_pool] * pg))


def _conv_kernel(bg_ref, cg_ref, h_ref, w_ref, buf_ref, y_ref, nb_ref, u_scr, *, tt, n_t):
    ti = pl.program_id(2)

    @pl.when(ti == 0)
    def _():
        u_scr[6:8, :] = buf_ref[0]

    u = cg_ref[0] * h_ref[0]
    u_scr[8:8 + tt, :] = u
    y = (w_ref[0:1, :] * u_scr[6:6 + tt, :] + w_ref[1:2, :] * u_scr[7:7 + tt, :]
         + w_ref[2:3, :] * u)
    y_ref[0] = (bg_ref[0] * y).astype(y_ref.dtype)
    tail = u_scr[6 + tt:8 + tt, :]
    u_scr[6:8, :] = tail

    @pl.when(ti == n_t - 1)
    def _():
        nb_ref[0] = tail


def _short_conv(proj3, buf0, conv_w):
    b, l, d3 = proj3.shape
    d = d3 // 3
    tc = _largest_divisor(d, 512, LANES)
    tt = _largest_divisor(l, 704, 8)
    n_t = l // tt
    nc = d // tc
    seg = lambda s: pl.BlockSpec((1, tt, tc), lambda i, j, t: (i, t, s * nc + j))
    return pl.pallas_call(
        functools.partial(_conv_kernel, tt=tt, n_t=n_t),
        grid=(b, nc, n_t),
        in_specs=[seg(0), seg(1), seg(2),
                  pl.BlockSpec((conv_w.shape[0], tc), lambda i, j, t: (0, j)),
                  pl.BlockSpec((1, 2, tc), lambda i, j, t: (i, 0, j))],
        out_specs=[pl.BlockSpec((1, tt, tc), lambda i, j, t: (i, t, j)),
                   pl.BlockSpec((1, 2, tc), lambda i, j, t: (i, 0, j))],
        out_shape=[jax.ShapeDtypeStruct((b, l, d), BF16), jax.ShapeDtypeStruct((b, 2, d), F32)],
        scratch_shapes=[pltpu.VMEM((tt + 8, tc), F32)],
        compiler_params=_params("parallel", "parallel", "arbitrary"),
        name="short_conv",
    )(proj3, proj3, proj3, conv_w, buf0)


def _trunk(h3, s0, shift0, conv0, fox_ctx, wts, *, prompt):
    (norm_mix, norm_mlp, w_in_even, b_forget, rwkv_mu, rwkv_w0, rwkv_w2, rwkv_a0, rwkv_a2, rwkv_g2,
     rwkv_k_k, rwkv_k_a, rwkv_r_k, rwkv_gn_g, rwkv_gn_b, w_out_even, w_in_odd, conv_w, w_out_odd,
     w_up, w_down) = wts
    b, l, d = h3.shape
    m = b * l
    h = h3.reshape(m, d)
    cw = rwkv_w0.shape[-1]
    rproj = rwkv_mu.shape[-1]
    fw = d - cw
    nh = fw // FOX_HEAD
    n_main = rproj + 3 * fw
    rw = (rwkv_mu[0], rwkv_w0[0], rwkv_w2[0], rwkv_a0[0], rwkv_a2[0], rwkv_g2[0], rwkv_k_k[0],
          rwkv_k_a[0], rwkv_r_k[0], rwkv_gn_g[0], rwkv_gn_b[0])

    hn = _rmsnorm(h, norm_mix[0], BF16)
    proj3 = _matmul(hn, w_in_even, 0, n_cols=rproj + fw, tn=256, tk=4096).reshape(b, l, rproj + fw)
    k3 = _matmul(hn, w_in_even, 0, n_lo=rproj + fw, n_cols=fw, tn=256, tk=4096).reshape(b, l, fw)
    v3 = _matmul(hn, w_in_even, 0, n_lo=rproj + 2 * fw, n_cols=fw, tn=256, tk=4096).reshape(b, l, fw)
    z = _matmul(hn, w_in_even[:, :, n_main:], 0, tn=nh, tk=4096)
    lf, c_col, c_row = _logf(z.reshape(b, l, nh), b_forget[0])
    if prompt:
        o_r, s_t = _rwkv(proj3, shift0[0], s0[0], *rw, c=48, n_valid=48, o_width=d)
        mix_in = _fox_prompt(proj3, rproj // LANES, k3, v3, c_col, c_row, o_r, cw // LANES)
    else:
        lp = 8 * pl.cdiv(l, 8)
        projp = jnp.pad(proj3[:, :, :rproj], ((0, 0), (0, lp - l), (0, 0)))
        o_r, s_t = _rwkv(projp, shift0[0], s0[0], *rw, c=lp, n_valid=l, o_width=cw)
        k_pool, v_pool, lf_pool, page_table = fox_ctx
        o_f = _fox_sample(proj3[:, :, rproj:], k3, v3, c_col, c_row, k_pool[0], v_pool[0],
                          lf_pool[0], page_table)
        mix_in = jnp.concatenate([o_r[:, :l], o_f], axis=-1)
    sh_t = proj3[:, l - 1, :rproj]
    k_out = k3.reshape(b, l, nh, FOX_HEAD)
    v_out = v3.reshape(b, l, nh, FOX_HEAD)
    h = _matmul(mix_in.reshape(m, d), w_out_even, 0, tn=512, tk=2048, epilogue="resid", resid=h)
    hn = _rmsnorm(h, norm_mlp[0], BF16)
    u = _matmul(hn, w_up, 0, tn=512, tk=2048, epilogue="relu2", out_dtype=BF16)
    h = _matmul(u, w_down, 0, tn=512, tk=2048, epilogue="resid", resid=h)

    hn = _rmsnorm(h, norm_mix[1], BF16)
    proj1 = _matmul(hn, w_in_odd, 0, tn=512, tk=2048)
    y, buf = _short_conv(proj1.reshape(b, l, 3 * d), conv0[0], conv_w[0])
    h = _matmul(y.reshape(m, d), w_out_odd, 0, tn=512, tk=2048, epilogue="resid", resid=h)
    hn = _rmsnorm(h, norm_mlp[1], BF16)
    u = _matmul(hn, w_up, 1, tn=512, tk=2048, epilogue="relu2", out_dtype=BF16)
    h = _matmul(u, w_down, 1, tn=512, tk=2048, epilogue="resid", resid=h)
    return (h.reshape(b, l, d), s_t[None], sh_t[None], k_out[None], v_out[None], lf[None], buf[None])


def kernel(x_prompt, x_sample, state_rwkv, state_rwkv_shift, cache_fox_k, cache_fox_v, cache_fox_logf, state_conv, page_table, meta_tokens, norm_mix, norm_mlp, norm_final, w_in_even, b_forget, rwkv_mu, rwkv_w0, rwkv_w2, rwkv_a0, rwkv_a2, rwkv_g2, rwkv_k_k, rwkv_k_a, rwkv_r_k, rwkv_gn_g, rwkv_gn_b, w_out_even, w_in_odd, conv_w, w_out_odd, w_up, w_down):
    wts = (norm_mix, norm_mlp, w_in_even, b_forget, rwkv_mu, rwkv_w0, rwkv_w2, rwkv_a0, rwkv_a2,
           rwkv_g2, rwkv_k_k, rwkv_k_a, rwkv_r_k, rwkv_gn_g, rwkv_gn_b, w_out_even, w_in_odd, conv_w,
           w_out_odd, w_up, w_down)
    bp, _, d = x_prompt.shape
    n_even, _, heads, hd, _ = state_rwkv.shape
    n_odd = state_conv.shape[0]
    rproj = state_rwkv_shift.shape[-1]

    h0 = jnp.concatenate([jnp.broadcast_to(meta_tokens[None], (bp, N_META, d)), x_prompt], axis=1)
    zs = jnp.zeros((n_even, bp, heads, hd, hd), F32)
    zsh = jnp.zeros((n_even, bp, rproj), F32)
    zconv = jnp.zeros((n_odd, bp, state_conv.shape[2], d), F32)
    hp, p_s, p_sh, p_k, p_v, p_lf, p_conv = _trunk(h0, zs, zsh, zconv, None, wts, prompt=True)
    y_prompt = _final_norm_prompt(hp, norm_final)

    hs, s_s, s_sh, s_k, s_v, s_lf, s_conv = _trunk(
        x_sample, state_rwkv, state_rwkv_shift, state_conv,
        (cache_fox_k, cache_fox_v, cache_fox_logf, page_table), wts, prompt=False)
    bs, ts, _ = x_sample.shape
    y_sample = _rmsnorm(hs.reshape(bs * ts, d), norm_final, F32).reshape(bs, ts, d)
    return (y_prompt, y_sample, p_s, p_sh, p_k, p_v, p_lf, p_conv, s_s, s_sh, s_k, s_v, s_lf, s_conv)
```

```python
import functools
import math

import jax
import jax.numpy as jnp
from jax import lax
from jax.experimental import pallas as pl
from jax.experimental.pallas import tpu as pltpu

F32 = jnp.float32
BF16 = jnp.bfloat16
HI = lax.Precision.HIGHEST

RMS_EPS = 1e-6
GN_EPS = 64e-5
NEG_INF = -1e30
N_META = 16
RWKV_HEAD = 64
FOX_HEAD = 128
PAGE_SIZE = 128
LANES = 128
VMEM_LIMIT = 60 * 1024 * 1024


def _largest_divisor(n, cap, mult):
    best = None
    for d in range(mult, min(n, cap) + 1, mult):
        if n % d == 0:
            best = d
    return n if best is None else best


def _params(*sem, flags=None):
    return pltpu.CompilerParams(dimension_semantics=sem, vmem_limit_bytes=VMEM_LIMIT, flags=flags)


def _dot(a, b, precision=None):
    return jnp.dot(a, b, precision=precision, preferred_element_type=F32)


def _dot_nt(a, b, precision=None):
    return lax.dot_general(a, b, (((1,), (1,)), ((), ())), precision=precision,
                           preferred_element_type=F32)


def _dot_tn(a, b, precision=None):
    return lax.dot_general(a, b, (((0,), (0,)), ((), ())), precision=precision,
                           preferred_element_type=F32)


def _iota(shape, dim):
    return lax.broadcasted_iota(jnp.int32, shape, dim)


def _split(x, pieces=2):
    out = []
    for i in range(pieces):
        p = x.astype(BF16)
        out.append(p)
        if i + 1 < pieces:
            x = x - p.astype(F32)
    return out


def _dot3(a, b, f=_dot):
    return f(a[0], b[0]) + (f(a[0], b[1]) + f(a[1], b[0]))


def _dot_exact(pieces, other, f=_dot):
    acc = f(pieces[0], other)
    for p in pieces[1:]:
        acc = acc + f(p, other)
    return acc


def _rmsnorm_kernel(x_ref, g_ref, o_ref):
    x = x_ref[...]
    ms = jnp.mean(x * x, axis=-1, keepdims=True)
    o_ref[...] = (x * lax.rsqrt(ms + RMS_EPS) * g_ref[...]).astype(o_ref.dtype)


def _rmsnorm(x2d, g, out_dtype):
    m, d = x2d.shape
    tr = _largest_divisor(m, 512, 16)
    return pl.pallas_call(
        _rmsnorm_kernel,
        grid=(m // tr,),
        in_specs=[pl.BlockSpec((tr, d), lambda i: (i, 0)),
                  pl.BlockSpec((1, d), lambda i: (0, 0))],
        out_specs=pl.BlockSpec((tr, d), lambda i: (i, 0)),
        out_shape=jax.ShapeDtypeStruct((m, d), out_dtype),
        compiler_params=_params("parallel"),
        name="rmsnorm",
    )(x2d, g.reshape(1, d))


def _final_norm_prompt_kernel(a_ref, b_ref, g_ref, o_ref):
    x = jnp.concatenate([a_ref[0, N_META:, :], b_ref[0]], axis=0)
    ms = jnp.mean(x * x, axis=-1, keepdims=True)
    o_ref[0] = x * lax.rsqrt(ms + RMS_EPS) * g_ref[...]


def _final_norm_prompt(h3, g):
    b, l, d = h3.shape
    s = l - N_META
    tr = _largest_divisor(s, 256, N_META)
    sub = tr // N_META
    return pl.pallas_call(
        _final_norm_prompt_kernel,
        grid=(b, s // tr),
        in_specs=[pl.BlockSpec((1, tr, d), lambda i, j: (i, j, 0)),
                  pl.BlockSpec((1, N_META, d), lambda i, j: (i, (j + 1) * sub, 0)),
                  pl.BlockSpec((1, d), lambda i, j: (0, 0))],
        out_specs=pl.BlockSpec((1, tr, d), lambda i, j: (i, j, 0)),
        out_shape=jax.ShapeDtypeStruct((b, s, d), F32),
        compiler_params=_params("parallel", "parallel"),
        name="final_norm_prompt",
    )(h3, h3, g.reshape(1, d))


def _mm_kernel(*refs, nk, epilogue):
    if epilogue == "resid":
        x_ref, w_ref, r_ref, o_ref = refs[:4]
    else:
        x_ref, w_ref, o_ref = refs[:3]
        r_ref = None
    acc_ref = refs[-1] if nk > 1 else None

    def finish(acc):
        if epilogue == "relu2":
            acc = jnp.square(jnp.maximum(acc, 0.0))
        elif epilogue == "resid":
            acc = r_ref[...] + acc
        o_ref[...] = acc.astype(o_ref.dtype)

    part = _dot(x_ref[...], w_ref[...].astype(BF16))
    if nk == 1:
        finish(part)
    else:
        k = pl.program_id(2)

        @pl.when(k == 0)
        def _():
            acc_ref[...] = part

        @pl.when(jnp.logical_and(k > 0, k < nk - 1))
        def _():
            acc_ref[...] += part

        @pl.when(k == nk - 1)
        def _():
            finish(acc_ref[...] + part)


def _matmul(x, w, layer, *, n_lo=0, n_cols=None, tn, tk, epilogue="plain", resid=None,
            out_dtype=F32):
    m, kdim = x.shape
    n_cols = w.shape[2] - n_lo if n_cols is None else n_cols
    tn = _largest_divisor(math.gcd(n_cols, n_lo), tn, LANES)
    tk = _largest_divisor(kdim, tk, LANES)
    assert n_lo % tn == 0 and n_cols % tn == 0
    tm = _largest_divisor(m, 2064, 16)
    nk = kdim // tk
    off = n_lo // tn
    in_specs = [pl.BlockSpec((tm, tk), lambda i, j, k: (i, k)),
                pl.BlockSpec((None, tk, tn), lambda i, j, k: (layer, k, j + off))]
    args = [x, w]
    if epilogue == "resid":
        in_specs.append(pl.BlockSpec((tm, tn), lambda i, j, k: (i, j)))
        args.append(resid)
    return pl.pallas_call(
        functools.partial(_mm_kernel, nk=nk, epilogue=epilogue),
        grid=(m // tm, n_cols // tn, nk),
        in_specs=in_specs,
        out_specs=pl.BlockSpec((tm, tn), lambda i, j, k: (i, j)),
        out_shape=jax.ShapeDtypeStruct((m, n_cols), out_dtype),
        scratch_shapes=[pltpu.VMEM((tm, tn), F32)] if nk > 1 else [],
        compiler_params=_params("parallel", "parallel", "arbitrary"),
        name="matmul_" + epilogue,
    )(*args)


def _rwkv_kernel(pr_ref, pk_ref, pv_ref, px_ref, sr_ref, sk_ref, sv_ref, sx_ref,
                 mur_ref, muk_ref, muv_ref, mux_ref, w0_ref, a0_ref, kk_ref, ka_ref, rk_ref,
                 gg_ref, gb_ref, w2_ref, a2_ref, g2_ref, s0_ref,
                 o_ref, st_ref,
                 s_scr, br_scr, bk_scr, bv_scr, bx_scr, *, c, g, nb, n_valid, n_chunks):
    ci = pl.program_id(2)
    half = RWKV_HEAD
    w = g * LANES
    nh = 2 * g
    n = nh * c

    @pl.when(ci == 0)
    def _():
        s_scr[...] = s0_ref[...]
        br_scr[:, 7:8, :] = sr_ref[...]
        bk_scr[:, 7:8, :] = sk_ref[...]
        bv_scr[:, 7:8, :] = sv_ref[...]
        bx_scr[:, 7:8, :] = sx_ref[...]

    ones_w = ((_iota((w, w), 0) // half) == (_iota((w, w), 1) // half)).astype(BF16)
    ones_p = ((_iota((LANES, LANES), 0) // half) == (_iota((LANES, LANES), 1) // half)).astype(F32)
    tri = (_iota((c, c), 1) <= _iota((c, c), 0)).astype(BF16)
    lane_c = _iota((1, LANES), 1)
    m2 = [(lane_c < half).astype(F32), (lane_c >= half).astype(F32)]
    ri = _iota((n, n), 0)
    cj = _iota((n, n), 1)
    rblk = sum((ri >= b * c).astype(jnp.int32) for b in range(1, nh))
    cblk = sum((cj >= b * c).astype(jnp.int32) for b in range(1, nh))
    same = rblk == cblk
    strict = jnp.logical_and(same, (cj - cblk * c) < (ri - rblk * c))
    incl = jnp.logical_and(same, (cj - cblk * c) <= (ri - rblk * c))
    eye = (ri == cj).astype(F32)
    w2b = w2_ref[...].astype(BF16)
    a2b = a2_ref[...].astype(BF16)
    g2b = g2_ref[...].astype(BF16)
    pair = lambda t, p: t[:, p * LANES:(p + 1) * LANES]

    def seg_sum(t):
        return _dot_exact(_split(t, 3), ones_w)

    def stack(t):
        return jnp.concatenate([pair(t, p) * m2[q] for p in range(g) for q in range(2)], axis=0)

    def sequence(bi):
        def shifted(p_ref, buf, mu_ref):
            p = p_ref[bi]
            buf[bi, 8:8 + c, :] = p
            prev = buf[bi, 7:7 + c, :]
            buf[bi, 7:8, :] = p[c - 1:c, :]
            return p + (prev - p) * mu_ref[...]

        r = shifted(pr_ref, br_scr, mur_ref)
        k = shifted(pk_ref, bk_scr, muk_ref)
        v = shifted(pv_ref, bv_scr, muv_ref)
        x = shifted(px_ref, bx_scr, mux_ref)
        yield

        wl = w0_ref[...] + _dot(jnp.tanh(x).astype(BF16), w2b)
        wl = -jax.nn.softplus(-wl) - 0.5
        logw = -jnp.exp(wl)
        a = jax.nn.sigmoid(a0_ref[...] + _dot(x.astype(BF16), a2b))
        gate = _dot(jax.nn.sigmoid(x).astype(BF16), g2b)
        kk = k * kk_ref[...]
        yield
        kk = kk / jnp.maximum(jnp.sqrt(seg_sum(kk * kk)), 1e-12)
        k = k * (1.0 + (a - 1.0) * ka_ref[...])
        be = kk * a
        if n_valid < c:
            ok = _iota((c, w), 0) < n_valid
            logw = jnp.where(ok, logw, 0.0)
            kk = jnp.where(ok, kk, 0.0)
            be = jnp.where(ok, be, 0.0)
            k = jnp.where(ok, k, 0.0)
            v = jnp.where(ok, v, 0.0)

        yield
        cum = _dot_exact(_split(logw, 3), tri, lambda p, t: _dot(t, p))
        yield
        tot = cum[c - 1:c, :]
        ah = -kk * jnp.exp(cum - logw)
        rh = r * jnp.exp(cum)
        ieg = jnp.exp(-cum)
        bc = be * ieg
        kc = k * ieg
        etail = jnp.exp(tot - cum)
        bt = be * etail
        kt = k * etail

        ars = _split(jnp.concatenate([stack(ah), stack(rh)], axis=0))
        vs = stack(v)
        yield
        gb = _dot3(ars, _split(stack(bc)), _dot_nt)
        yield
        gk = _dot3(ars, _split(stack(kc)), _dot_nt)
        xm = jnp.where(strict, gb[:n], 0.0)
        tm = eye + xm
        xp = _split(xm)
        yield
        for _ in range(max(1, math.ceil(math.log2(c))) - 1):
            xp = _split(_dot3(xp, xp))
            yield
            tm = tm + _dot3(_split(tm), xp)
            yield
        lm = jnp.concatenate([jnp.where(strict, gk[:n], 0.0), jnp.where(incl, gk[n:], 0.0)], axis=0)
        lmv = _dot3(_split(lm), _split(vs))
        mrb = jnp.where(incl, gb[n:], 0.0)
        yield

        y0 = [_dot3(_split(jnp.concatenate([pair(ah, p), pair(rh, p)], axis=0)),
                    _split(s_scr[bi, p]), _dot_nt) for p in range(g)]
        yield
        ys = jnp.concatenate([y0[p][:c] * m2[q] for p in range(g) for q in range(2)], axis=0) + lmv[:n]
        ps = _dot3(_split(tm), _split(ys))
        yield
        os_ = _dot3(_split(mrb), _split(ps)) + lmv[n:]
        yield
        o_parts = []
        for p in range(g):
            lo, mid, hi = 2 * p * c, (2 * p + 1) * c, (2 * p + 2) * c
            o_parts.append(y0[p][c:] + os_[lo:mid] + os_[mid:hi])
            pv = jnp.concatenate([ps[lo:mid] + ps[mid:hi], pair(v, p)], axis=0)
            bk = jnp.concatenate([pair(bt, p), pair(kt, p)], axis=0)
            upd = _dot3(_split(pv), _split(bk), _dot_tn)
            s_scr[bi, p] = (s_scr[bi, p] * jnp.exp(pair(tot, p)) + upd) * ones_p
            yield

        o = jnp.concatenate(o_parts, axis=1) if g > 1 else o_parts[0]
        inv_n = 1.0 / half
        mean = seg_sum(o) * inv_n
        yield
        d = o - mean
        var = seg_sum(d * d) * inv_n
        yield
        on = d * lax.rsqrt(var + GN_EPS) * gg_ref[...] + gb_ref[...]
        bonus = seg_sum(r * k * rk_ref[...]) * v
        o_ref[bi] = ((on + bonus) * gate).astype(o_ref.dtype)

    live = [sequence(bi) for bi in range(nb)]
    while live:
        live = [s for s in live if next(s, True) is None]

    @pl.when(ci == n_chunks - 1)
    def _():
        st_ref[...] = s_scr[...]


def _rwkv(proj3, shift0, s0, mu, w0, w2, a0, a2, g2, k_k, k_a, r_k, gn_g, gn_b, *, c, n_valid,
          o_width):
    b, lp, _ = proj3.shape
    cw = w0.shape[-1]
    npair = cw // LANES
    heads = cw // RWKV_HEAD
    g = 2 if npair % 2 == 0 else 1
    w = g * LANES
    ng = npair // g
    lw = mu.shape[-1] - 3 * cw
    assert lw % LANES == 0 and (3 * cw) % lw == 0 and lp % c == 0
    n_chunks = lp // c
    wl_, al_ = w2.shape[0], a2.shape[0]
    w2p = jnp.zeros((lw, cw), F32).at[:wl_].set(w2)
    a2p = jnp.zeros((lw, cw), F32).at[wl_:wl_ + al_].set(a2)
    g2p = jnp.zeros((lw, cw), F32).at[wl_ + al_:].set(g2)
    s0p = s0.reshape(b, npair, 2, RWKV_HEAD, RWKV_HEAD)
    z = jnp.zeros_like(s0p[:, :, 0])
    s0bd = jnp.concatenate([jnp.concatenate([s0p[:, :, 0], z], axis=-1),
                            jnp.concatenate([z, s0p[:, :, 1]], axis=-1)], axis=-2)
    sh3 = shift0.reshape(b, 1, -1)
    mu2 = mu.reshape(1, -1)
    row = lambda t: t.reshape(1, cw)
    xoff = (3 * cw) // lw

    nb = _largest_divisor(b, 4, 1)
    seg = lambda s: pl.BlockSpec((nb, c, w), lambda i, j, t: (i, t, s * ng + j))
    sseg = lambda s: pl.BlockSpec((nb, 1, w), lambda i, j, t: (i, 0, s * ng + j))
    mseg = lambda s: pl.BlockSpec((1, w), lambda i, j, t: (0, s * ng + j))
    vec = pl.BlockSpec((1, w), lambda i, j, t: (0, j))
    lora = pl.BlockSpec((lw, w), lambda i, j, t: (0, j))
    state = pl.BlockSpec((nb, g, LANES, LANES), lambda i, j, t: (i, j, 0, 0))
    in_specs = [seg(0), seg(1), seg(2), pl.BlockSpec((nb, c, lw), lambda i, j, t: (i, t, xoff)),
                sseg(0), sseg(1), sseg(2), pl.BlockSpec((nb, 1, lw), lambda i, j, t: (i, 0, xoff)),
                mseg(0), mseg(1), mseg(2), pl.BlockSpec((1, lw), lambda i, j, t: (0, xoff)),
                vec, vec, vec, vec, vec, vec, vec, lora, lora, lora, state]
    o, st = pl.pallas_call(
        functools.partial(_rwkv_kernel, c=c, g=g, nb=nb, n_valid=n_valid, n_chunks=n_chunks),
        grid=(b // nb, ng, n_chunks),
        in_specs=in_specs,
        out_specs=[pl.BlockSpec((nb, c, w), lambda i, j, t: (i, t, j)), state],
        out_shape=[jax.ShapeDtypeStruct((b, lp, o_width), BF16),
                   jax.ShapeDtypeStruct((b, npair, LANES, LANES), F32)],
        scratch_shapes=[pltpu.VMEM((nb, g, LANES, LANES), F32),
                        pltpu.VMEM((nb, c + 8, w), F32), pltpu.VMEM((nb, c + 8, w), F32),
                        pltpu.VMEM((nb, c + 8, w), F32), pltpu.VMEM((nb, c + 8, lw), F32)],
        compiler_params=_params("parallel", "parallel", "arbitrary"),
        name="rwkv7_chunk",
    )(proj3, proj3, proj3, proj3, sh3, sh3, sh3, sh3, mu2, mu2, mu2, mu2,
      row(w0), row(a0), row(k_k), row(k_a), row(r_k), row(gn_g), row(gn_b), w2p, a2p, g2p, s0bd)
    st = jnp.stack([st[:, :, :RWKV_HEAD, :RWKV_HEAD], st[:, :, RWKV_HEAD:, RWKV_HEAD:]], axis=2)
    return o, st.reshape(b, heads, RWKV_HEAD, RWKV_HEAD)


def _logf_kernel(z_ref, bf_ref, lf_ref, cc_ref, cr_ref, *, rb):
    l, h = z_ref.shape[1], z_ref.shape[2]
    ti = _iota((rb, rb), 0)
    si = _iota((rb, rb), 1)
    tri = (si <= ti).astype(F32)
    carry = jnp.zeros((1, h), F32)
    for i in range(l // rb):
        sl = slice(i * rb, (i + 1) * rb)
        lf = jax.nn.log_sigmoid(z_ref[0, sl, :] + bf_ref[...])
        lf_ref[0, sl, :] = lf
        cblk = carry + _dot(tri, lf, HI)
        cc_ref[0, sl, :] = cblk
        carry = cblk[rb - 1:rb, :]
    eye = (_iota((h, h), 0) == _iota((h, h), 1)).astype(F32)
    cr_ref[0] = _dot_nt(eye, cc_ref[0], HI)


def _logf(z3, b_forget):
    b, l, h = z3.shape
    rb = _largest_divisor(l, 512, 8)
    blk = pl.BlockSpec((1, l, h), lambda i: (i, 0, 0))
    return pl.pallas_call(
        functools.partial(_logf_kernel, rb=rb),
        grid=(b,),
        in_specs=[blk, pl.BlockSpec((1, h), lambda i: (0, 0))],
        out_specs=[blk, blk, pl.BlockSpec((1, h, l), lambda i: (i, 0, 0))],
        out_shape=[jax.ShapeDtypeStruct((b, l, h), F32), jax.ShapeDtypeStruct((b, l, h), F32),
                   jax.ShapeDtypeStruct((b, h, l), F32)],
        compiler_params=_params("parallel"),
        name="fox_logf_cumsum",
    )(z3, b_forget.reshape(1, h))


def _fox_prompt_kernel(q_ref, k_ref, v_ref, cc_ref, cr_ref, buf_ref, o_ref, *, tq):
    del buf_ref
    l = q_ref.shape[1]
    nh = cc_ref.shape[2]
    h = pl.program_id(1)
    scale = FOX_HEAD ** -0.5
    sel = (_iota((nh, LANES), 0) == h).astype(F32)
    cq_all = _dot(cc_ref[0], sel, HI)
    ck = cr_ref[0, pl.ds(h, 1), :]
    kb = k_ref[0].astype(BF16)
    vb = v_ref[0].astype(BF16)
    for i in range(l // tq):
        nk = (i + 1) * tq
        rows = slice(i * tq, nk)
        s = _dot_nt(q_ref[0, rows, :].astype(BF16), kb[:nk]) * scale
        s = s + cq_all[rows, 0:1] - ck[:, :nk]
        qpos = _iota((tq, nk), 0) + i * tq
        kpos = _iota((tq, nk), 1)
        s = jnp.where(qpos >= kpos, s, NEG_INF)
        m = jnp.max(s, axis=-1, keepdims=True)
        p = jnp.exp(s - m)
        den = jnp.sum(p, axis=-1, keepdims=True)
        o = _dot(p.astype(BF16), vb[:nk]) / den
        o_ref[0, rows, :] = o.astype(o_ref.dtype)


def _fox_prompt(proj3, q_off, k3, v3, c_col, c_row, o_buf, o_off):
    b, l, _ = proj3.shape
    nh = c_col.shape[2]
    tq = _largest_divisor(l, 384, 8)
    blk = lambda off: pl.BlockSpec((1, l, LANES), lambda i, j: (i, 0, off + j))
    return pl.pallas_call(
        functools.partial(_fox_prompt_kernel, tq=tq),
        grid=(b, nh),
        in_specs=[blk(q_off), blk(0), blk(0),
                  pl.BlockSpec((1, l, nh), lambda i, j: (i, 0, 0)),
                  pl.BlockSpec((1, nh, l), lambda i, j: (i, 0, 0)),
                  pl.BlockSpec(memory_space=pl.ANY)],
        out_specs=blk(o_off),
        out_shape=jax.ShapeDtypeStruct(o_buf.shape, o_buf.dtype),
        input_output_aliases={5: 0},
        compiler_params=_params("parallel", "parallel"),
        name="fox_prompt_attention",
    )(proj3, k3, v3, c_col, c_row, o_buf)


def _page_tail_kernel(lf_ref, tail_ref, tot_ref):
    pp = lf_ref.shape[0]
    later = (_iota((PAGE_SIZE, PAGE_SIZE), 1) > _iota((PAGE_SIZE, PAGE_SIZE), 0)).astype(BF16)
    for i in range(pp):
        lf = lf_ref[i]
        tail_ref[i] = _dot_exact(_split(lf, 3), later, lambda p, u: _dot(u, p))
        tot_ref[i] = jnp.broadcast_to(jnp.sum(lf, axis=0, keepdims=True), lf.shape)


def _page_tails(lf_pool):
    n_phys, _, nh = lf_pool.shape
    pp = _largest_divisor(n_phys, 32, 1)
    blk = pl.BlockSpec((pp, PAGE_SIZE, nh), lambda i: (i, 0, 0))
    shape = jax.ShapeDtypeStruct(lf_pool.shape, F32)
    return pl.pallas_call(
        _page_tail_kernel,
        grid=(n_phys // pp,),
        in_specs=[blk],
        out_specs=[blk, blk],
        out_shape=[shape, shape],
        compiler_params=_params("parallel"),
        name="fox_page_tails",
    )(lf_pool)


def _fox_sample_kernel(pt_ref, q_ref, kn_ref, vn_ref, cq_ref, ck_ref, *rest, nh, n_steps, pg):
    kp = rest[:pg]
    vp = rest[pg:2 * pg]
    tl = rest[2 * pg:3 * pg]
    tt = rest[3 * pg:4 * pg]
    o_ref, m_scr, l_scr, acc_scr, suf_scr = rest[4 * pg:]
    pi = pl.program_id(1)
    rows = q_ref.shape[1]
    cols = PAGE_SIZE * nh
    scale = FOX_HEAD ** -0.5

    @pl.when(pi == 0)
    def _():
        m_scr[...] = jnp.full(m_scr.shape, NEG_INF, F32)
        l_scr[...] = jnp.zeros(l_scr.shape, F32)
        acc_scr[...] = jnp.zeros(acc_scr.shape, F32)
        suf_scr[...] = jnp.zeros(suf_scr.shape, F32)

    qb = q_ref[0].astype(BF16)
    cnew = cq_ref[0]

    def online(scores, vals):
        m_old = m_scr[...]
        m_new = m_old
        for s in scores:
            m_new = jnp.maximum(m_new, jnp.max(s, axis=-1, keepdims=True))
        corr = jnp.exp(m_old - m_new)
        den = l_scr[...] * corr
        acc = acc_scr[...] * corr
        for s, val in zip(scores, vals):
            pe = jnp.exp(s - m_new)
            den = den + jnp.sum(pe, axis=-1, keepdims=True)
            acc = acc + _dot(pe.astype(BF16), val)
        l_scr[...] = den
        acc_scr[...] = acc
        m_scr[...] = m_new

    own_head = (_iota((rows, cols), 0) % nh) == (_iota((rows, cols), 1) % nh)
    suf = suf_scr[...]
    bias = [None] * pg
    for gi in reversed(range(pg)):
        bias[gi] = suf + tl[gi][0]
        suf = suf + tt[gi][0]
    suf_scr[...] = suf
    scores = []
    for gi in range(pg):
        kflat = kp[gi][0].reshape(cols, FOX_HEAD).astype(BF16)
        s = _dot_nt(qb, kflat) * scale + bias[gi] + cnew
        scores.append(jnp.where(own_head, s, NEG_INF))
    online(scores, [r[0].reshape(cols, FOX_HEAD).astype(BF16) for r in vp])

    @pl.when(pi == n_steps - 1)
    def _():
        sn = _dot_nt(qb, kn_ref[0].astype(BF16)) * scale + cnew - ck_ref[0]
        ri = _iota((rows, rows), 0)
        cj = _iota((rows, rows), 1)
        ok = jnp.logical_and((ri % nh) == (cj % nh), (cj // nh) <= (ri // nh))
        online([jnp.where(ok, sn, NEG_INF)], [vn_ref[0].astype(BF16)])
        o_ref[0] = (acc_scr[...] / l_scr[...]).astype(o_ref.dtype)


def _fox_sample(q, k_new, v_new, c_col, k_pool, v_pool, lf_pool, page_table):
    bd, t_new, width = q.shape
    nh = c_col.shape[2]
    n_pages = page_table.shape[1]
    n_phys = k_pool.shape[0]
    pg = 4 if n_pages % 4 == 0 else 1
    n_steps = n_pages // pg
    rows = t_new * nh
    cols = PAGE_SIZE * nh
    tail, tot = _page_tails(lf_pool)
    tail = tail.reshape(n_phys, 1, cols)
    tot = tot.reshape(n_phys, 1, cols)
    by_head = lambda a: a.reshape(bd, rows, FOX_HEAD)
    full = lambda shape: pl.BlockSpec((1,) + shape, lambda i, p, pt: (i, 0, 0))

    def page(shape, slot):
        zeros = (0,) * len(shape)
        return pl.BlockSpec((1,) + shape,
                            lambda i, p, pt: (pt[i, n_pages - (p + 1) * pg + slot],) + zeros)

    slots = range(pg)
    kv_page = (PAGE_SIZE, nh, FOX_HEAD)
    grid_spec = pltpu.PrefetchScalarGridSpec(
        num_scalar_prefetch=1,
        grid=(bd, n_steps),
        in_specs=([full((rows, FOX_HEAD)), full((rows, FOX_HEAD)), full((rows, FOX_HEAD)),
                   full((rows, 1)), full((1, rows))]
                  + [page(kv_page, s) for s in slots] + [page(kv_page, s) for s in slots]
                  + [page((1, cols), s) for s in slots] + [page((1, cols), s) for s in slots]),
        out_specs=full((rows, FOX_HEAD)),
        scratch_shapes=[pltpu.VMEM((rows, 1), F32), pltpu.VMEM((rows, 1), F32),
                        pltpu.VMEM((rows, FOX_HEAD), F32), pltpu.VMEM((1, cols), F32)],
    )
    out = pl.pallas_call(
        functools.partial(_fox_sample_kernel, nh=nh, n_steps=n_steps, pg=pg),
        grid_spec=grid_spec,
        out_shape=jax.ShapeDtypeStruct((bd, rows, FOX_HEAD), BF16),
        compiler_params=_params("parallel", "arbitrary"),
        name="fox_sample_attention",
    )(page_table, by_head(q), by_head(k_new), by_head(v_new),
      c_col.reshape(bd, rows, 1), c_col.reshape(bd, 1, rows),
      *([k_pool] * pg), *([v_pool] * pg), *([tail] * pg), *([tot] * pg))
    return out.reshape(bd, t_new, width)


def _conv_kernel(bg_ref, cg_ref, h_ref, w_ref, buf_ref, y_ref, nb_ref, u_scr, *, tt, n_t):
    ti = pl.program_id(2)

    @pl.when(ti == 0)
    def _():
        u_scr[6:8, :] = buf_ref[0]

    u = cg_ref[0] * h_ref[0]
    u_scr[8:8 + tt, :] = u
    y = (w_ref[0:1, :] * u_scr[6:6 + tt, :] + w_ref[1:2, :] * u_scr[7:7 + tt, :]
         + w_ref[2:3, :] * u)
    y_ref[0] = (bg_ref[0] * y).astype(y_ref.dtype)
    tail = u_scr[6 + tt:8 + tt, :]
    u_scr[6:8, :] = tail

    @pl.when(ti == n_t - 1)
    def _():
        nb_ref[0] = tail


def _short_conv(proj3, buf0, conv_w):
    b, l, d3 = proj3.shape
    d = d3 // 3
    tc = _largest_divisor(d, 512, LANES)
    tt = _largest_divisor(l, 704, 8)
    n_t = l // tt
    nc = d // tc
    seg = lambda s: pl.BlockSpec((1, tt, tc), lambda i, j, t: (i, t, s * nc + j))
    return pl.pallas_call(
        functools.partial(_conv_kernel, tt=tt, n_t=n_t),
        grid=(b, nc, n_t),
        in_specs=[seg(0), seg(1), seg(2),
                  pl.BlockSpec((conv_w.shape[0], tc), lambda i, j, t: (0, j)),
                  pl.BlockSpec((1, 2, tc), lambda i, j, t: (i, 0, j))],
        out_specs=[pl.BlockSpec((1, tt, tc), lambda i, j, t: (i, t, j)),
                   pl.BlockSpec((1, 2, tc), lambda i, j, t: (i, 0, j))],
        out_shape=[jax.ShapeDtypeStruct((b, l, d), BF16), jax.ShapeDtypeStruct((b, 2, d), F32)],
        scratch_shapes=[pltpu.VMEM((tt + 8, tc), F32)],
        compiler_params=_params("parallel", "parallel", "arbitrary"),
        name="short_conv",
    )(proj3, proj3, proj3, conv_w, buf0)


def _trunk(h3, s0, shift0, conv0, fox_ctx, wts, *, prompt):
    (norm_mix, norm_mlp, w_in_even, b_forget, rwkv_mu, rwkv_w0, rwkv_w2, rwkv_a0, rwkv_a2, rwkv_g2,
     rwkv_k_k, rwkv_k_a, rwkv_r_k, rwkv_gn_g, rwkv_gn_b, w_out_even, w_in_odd, conv_w, w_out_odd,
     w_up, w_down) = wts
    b, l, d = h3.shape
    m = b * l
    h = h3.reshape(m, d)
    cw = rwkv_w0.shape[-1]
    rproj = rwkv_mu.shape[-1]
    fw = d - cw
    nh = fw // FOX_HEAD
    n_main = rproj + 3 * fw
    rw = (rwkv_mu[0], rwkv_w0[0], rwkv_w2[0], rwkv_a0[0], rwkv_a2[0], rwkv_g2[0], rwkv_k_k[0],
          rwkv_k_a[0], rwkv_r_k[0], rwkv_gn_g[0], rwkv_gn_b[0])

    hn = _rmsnorm(h, norm_mix[0], BF16)
    proj3 = _matmul(hn, w_in_even, 0, n_cols=rproj + fw, tn=256, tk=4096).reshape(b, l, rproj + fw)
    k3 = _matmul(hn, w_in_even, 0, n_lo=rproj + fw, n_cols=fw, tn=256, tk=4096).reshape(b, l, fw)
    v3 = _matmul(hn, w_in_even, 0, n_lo=rproj + 2 * fw, n_cols=fw, tn=256, tk=4096).reshape(b, l, fw)
    z = _matmul(hn, w_in_even[:, :, n_main:], 0, tn=nh, tk=4096)
    lf, c_col, c_row = _logf(z.reshape(b, l, nh), b_forget[0])
    if prompt:
        o_r, s_t = _rwkv(proj3, shift0[0], s0[0], *rw, c=48, n_valid=48, o_width=d)
        mix_in = _fox_prompt(proj3, rproj // LANES, k3, v3, c_col, c_row, o_r, cw // LANES)
    else:
        lp = 8 * pl.cdiv(l, 8)
        projp = jnp.pad(proj3[:, :, :rproj], ((0, 0), (0, lp - l), (0, 0)))
        o_r, s_t = _rwkv(projp, shift0[0], s0[0], *rw, c=lp, n_valid=l, o_width=cw)
        k_pool, v_pool, lf_pool, page_table = fox_ctx
        o_f = _fox_sample(proj3[:, :, rproj:], k3, v3, c_col, k_pool[0], v_pool[0], lf_pool[0],
                          page_table)
        mix_in = jnp.concatenate([o_r[:, :l], o_f], axis=-1)
    sh_t = proj3[:, l - 1, :rproj]
    k_out = k3.reshape(b, l, nh, FOX_HEAD)
    v_out = v3.reshape(b, l, nh, FOX_HEAD)
    h = _matmul(mix_in.reshape(m, d), w_out_even, 0, tn=256, tk=4096, epilogue="resid", resid=h)
    hn = _rmsnorm(h, norm_mlp[0], BF16)
    u = _matmul(hn, w_up, 0, tn=256, tk=4096, epilogue="relu2", out_dtype=BF16)
    h = _matmul(u, w_down, 0, tn=512, tk=2048, epilogue="resid", resid=h)

    hn = _rmsnorm(h, norm_mix[1], BF16)
    proj1 = _matmul(hn, w_in_odd, 0, tn=256, tk=4096)
    y, buf = _short_conv(proj1.reshape(b, l, 3 * d), conv0[0], conv_w[0])
    h = _matmul(y.reshape(m, d), w_out_odd, 0, tn=256, tk=4096, epilogue="resid", resid=h)
    hn = _rmsnorm(h, norm_mlp[1], BF16)
    u = _matmul(hn, w_up, 1, tn=256, tk=4096, epilogue="relu2", out_dtype=BF16)
    h = _matmul(u, w_down, 1, tn=512, tk=2048, epilogue="resid", resid=h)
    return (h.reshape(b, l, d), s_t[None], sh_t[None], k_out[None], v_out[None], lf[None], buf[None])


def kernel(x_prompt, x_sample, state_rwkv, state_rwkv_shift, cache_fox_k, cache_fox_v, cache_fox_logf, state_conv, page_table, meta_tokens, norm_mix, norm_mlp, norm_final, w_in_even, b_forget, rwkv_mu, rwkv_w0, rwkv_w2, rwkv_a0, rwkv_a2, rwkv_g2, rwkv_k_k, rwkv_k_a, rwkv_r_k, rwkv_gn_g, rwkv_gn_b, w_out_even, w_in_odd, conv_w, w_out_odd, w_up, w_down):
    wts = (norm_mix, norm_mlp, w_in_even, b_forget, rwkv_mu, rwkv_w0, rwkv_w2, rwkv_a0, rwkv_a2,
           rwkv_g2, rwkv_k_k, rwkv_k_a, rwkv_r_k, rwkv_gn_g, rwkv_gn_b, w_out_even, w_in_odd, conv_w,
           w_out_odd, w_up, w_down)
    bp, _, d = x_prompt.shape
    n_even, _, heads, hd, _ = state_rwkv.shape
    n_odd = state_conv.shape[0]
    rproj = state_rwkv_shift.shape[-1]

    h0 = jnp.concatenate([jnp.broadcast_to(meta_tokens[None], (bp, N_META, d)), x_prompt], axis=1)
    zs = jnp.zeros((n_even, bp, heads, hd, hd), F32)
    zsh = jnp.zeros((n_even, bp, rproj), F32)
    zconv = jnp.zeros((n_odd, bp, state_conv.shape[2], d), F32)
    hp, p_s, p_sh, p_k, p_v, p_lf, p_conv = _trunk(h0, zs, zsh, zconv, None, wts, prompt=True)
    y_prompt = _final_norm_prompt(hp, norm_final)

    hs, s_s, s_sh, s_k, s_v, s_lf, s_conv = _trunk(
        x_sample, state_rwkv, state_rwkv_shift, state_conv,
        (cache_fox_k, cache_fox_v, cache_fox_logf, page_table), wts, prompt=False)
    bs, ts, _ = x_sample.shape
    y_sample = _rmsnorm(hs.reshape(bs * ts, d), norm_final, F32).reshape(bs, ts, d)
    return (y_prompt, y_sample, p_s, p_sh, p_k, p_v, p_lf, p_conv, s_s, s_sh, s_k, s_v, s_lf, s_conv)
```

```python
import functools
import math

import jax
import jax.numpy as jnp
from jax import lax
from jax.experimental import pallas as pl
from jax.experimental.pallas import tpu as pltpu

F32 = jnp.float32
BF16 = jnp.bfloat16
HI = lax.Precision.HIGHEST

RMS_EPS = 1e-6
GN_EPS = 64e-5
NEG_INF = -1e30
N_META = 16
RWKV_HEAD = 64
FOX_HEAD = 128
PAGE_SIZE = 128
LANES = 128
VMEM_LIMIT = 60 * 1024 * 1024
BF16_ROWS = 16
RWKV_CHUNK = 48


def _largest_divisor(n, cap, mult):
    best = None
    for d in range(mult, min(n, cap) + 1, mult):
        if n % d == 0:
            best = d
    return n if best is None else best


def _params(*sem, flags=None):
    return pltpu.CompilerParams(dimension_semantics=sem, vmem_limit_bytes=VMEM_LIMIT, flags=flags)


def _dot(a, b, precision=None):
    return jnp.dot(a, b, precision=precision, preferred_element_type=F32)


def _dot_nt(a, b, precision=None):
    return lax.dot_general(a, b, (((1,), (1,)), ((), ())), precision=precision,
                           preferred_element_type=F32)


def _dot_tn(a, b, precision=None):
    return lax.dot_general(a, b, (((0,), (0,)), ((), ())), precision=precision,
                           preferred_element_type=F32)


def _iota(shape, dim):
    return lax.broadcasted_iota(jnp.int32, shape, dim)


def _split(x, pieces=2):
    out = []
    for i in range(pieces):
        p = x.astype(BF16)
        out.append(p)
        if i + 1 < pieces:
            x = x - p.astype(F32)
    return out


def _dot3(a, b, f=_dot):
    axis = 1 if f is _dot_tn else 0
    m = a[0].shape[axis]
    if m % BF16_ROWS:
        return f(a[0], b[0]) + (f(a[0], b[1]) + f(a[1], b[0]))
    t = f(jnp.concatenate([a[0], a[1]], axis=axis), b[0])
    return (t[:m] + t[m:]) + f(a[0], b[1])


def _dot_exact(pieces, other, f=_dot):
    m = pieces[0].shape[0]
    if f is _dot and m % BF16_ROWS == 0:
        t = f(jnp.concatenate(pieces, axis=0), other)
        return sum(t[i * m:(i + 1) * m] for i in range(1, len(pieces))) + t[:m]
    acc = f(pieces[0], other)
    for p in pieces[1:]:
        acc = acc + f(p, other)
    return acc


def _rmsnorm_kernel(x_ref, g_ref, o_ref):
    x = x_ref[...]
    ms = jnp.mean(x * x, axis=-1, keepdims=True)
    o_ref[...] = (x * lax.rsqrt(ms + RMS_EPS) * g_ref[...]).astype(o_ref.dtype)


def _rmsnorm(x2d, g, out_dtype):
    m, d = x2d.shape
    tr = _largest_divisor(m, 512, 16)
    return pl.pallas_call(
        _rmsnorm_kernel,
        grid=(m // tr,),
        in_specs=[pl.BlockSpec((tr, d), lambda i: (i, 0)),
                  pl.BlockSpec((1, d), lambda i: (0, 0))],
        out_specs=pl.BlockSpec((tr, d), lambda i: (i, 0)),
        out_shape=jax.ShapeDtypeStruct((m, d), out_dtype),
        compiler_params=_params("parallel"),
        name="rmsnorm",
    )(x2d, g.reshape(1, d))


def _rmsnorm_pad_kernel(x_ref, g_ref, o_ref, op_ref, *, n_real):
    x = x_ref[0]
    ms = jnp.mean(x * x, axis=-1, keepdims=True)
    y = (x * lax.rsqrt(ms + RMS_EPS) * g_ref[...]).astype(o_ref.dtype)
    o_ref[0] = y
    op_ref[0] = jnp.where(pl.program_id(1) < n_real, y, jnp.zeros_like(y))


def _rmsnorm_pad(h3, g, lp):
    b, l, d = h3.shape
    tr = _largest_divisor(math.gcd(l, lp), 512, BF16_ROWS)
    n_real = l // tr
    real = pl.BlockSpec((1, tr, d), lambda i, j: (i, jnp.minimum(j, n_real - 1), 0))
    return pl.pallas_call(
        functools.partial(_rmsnorm_pad_kernel, n_real=n_real),
        grid=(b, lp // tr),
        in_specs=[real, pl.BlockSpec((1, d), lambda i, j: (0, 0))],
        out_specs=[real, pl.BlockSpec((1, tr, d), lambda i, j: (i, j, 0))],
        out_shape=[jax.ShapeDtypeStruct((b, l, d), BF16), jax.ShapeDtypeStruct((b, lp, d), BF16)],
        compiler_params=_params("parallel", "arbitrary"),
        name="rmsnorm_pad",
    )(h3, g.reshape(1, d))


def _final_norm_prompt_kernel(a_ref, b_ref, g_ref, o_ref):
    x = jnp.concatenate([a_ref[0, N_META:, :], b_ref[0]], axis=0)
    ms = jnp.mean(x * x, axis=-1, keepdims=True)
    o_ref[0] = x * lax.rsqrt(ms + RMS_EPS) * g_ref[...]


def _final_norm_prompt(h3, g):
    b, l, d = h3.shape
    s = l - N_META
    tr = _largest_divisor(s, 256, N_META)
    sub = tr // N_META
    return pl.pallas_call(
        _final_norm_prompt_kernel,
        grid=(b, s // tr),
        in_specs=[pl.BlockSpec((1, tr, d), lambda i, j: (i, j, 0)),
                  pl.BlockSpec((1, N_META, d), lambda i, j: (i, (j + 1) * sub, 0)),
                  pl.BlockSpec((1, d), lambda i, j: (0, 0))],
        out_specs=pl.BlockSpec((1, tr, d), lambda i, j: (i, j, 0)),
        out_shape=jax.ShapeDtypeStruct((b, s, d), F32),
        compiler_params=_params("parallel", "parallel"),
        name="final_norm_prompt",
    )(h3, h3, g.reshape(1, d))


def _mm_kernel(*refs, nk, epilogue):
    if epilogue == "resid":
        x_ref, w_ref, r_ref, o_ref = refs[:4]
    else:
        x_ref, w_ref, o_ref = refs[:3]
        r_ref = None
    acc_ref = refs[-1] if nk > 1 else None

    def finish(acc):
        if epilogue == "relu2":
            acc = jnp.square(jnp.maximum(acc, 0.0))
        elif epilogue == "resid":
            acc = r_ref[...] + acc
        o_ref[...] = acc.astype(o_ref.dtype)

    part = _dot(x_ref[...], w_ref[...].astype(BF16))
    if nk == 1:
        finish(part)
    else:
        k = pl.program_id(2)

        @pl.when(k == 0)
        def _():
            acc_ref[...] = part

        @pl.when(jnp.logical_and(k > 0, k < nk - 1))
        def _():
            acc_ref[...] += part

        @pl.when(k == nk - 1)
        def _():
            finish(acc_ref[...] + part)


def _matmul(x, w, layer, *, n_lo=0, n_cols=None, tn, tk, epilogue="plain", resid=None,
            out_dtype=F32):
    m, kdim = x.shape
    n_cols = w.shape[2] - n_lo if n_cols is None else n_cols
    tn = _largest_divisor(math.gcd(n_cols, n_lo), tn, LANES)
    tk = _largest_divisor(kdim, tk, LANES)
    assert n_lo % tn == 0 and n_cols % tn == 0
    tm = _largest_divisor(m, 2112, BF16_ROWS)
    nk = kdim // tk
    off = n_lo // tn
    in_specs = [pl.BlockSpec((tm, tk), lambda i, j, k: (i, k)),
                pl.BlockSpec((None, tk, tn), lambda i, j, k: (layer, k, j + off))]
    args = [x, w]
    if epilogue == "resid":
        in_specs.append(pl.BlockSpec((tm, tn), lambda i, j, k: (i, j)))
        args.append(resid)
    return pl.pallas_call(
        functools.partial(_mm_kernel, nk=nk, epilogue=epilogue),
        grid=(m // tm, n_cols // tn, nk),
        in_specs=in_specs,
        out_specs=pl.BlockSpec((tm, tn), lambda i, j, k: (i, j)),
        out_shape=jax.ShapeDtypeStruct((m, n_cols), out_dtype),
        scratch_shapes=[pltpu.VMEM((tm, tn), F32)] if nk > 1 else [],
        compiler_params=_params("parallel", "parallel", "arbitrary"),
        name="matmul_" + epilogue,
    )(*args)


def _rwkv_kernel(pr_ref, pk_ref, pv_ref, px_ref, sr_ref, sk_ref, sv_ref, sx_ref,
                 mur_ref, muk_ref, muv_ref, mux_ref, w0_ref, a0_ref, kk_ref, ka_ref, rk_ref,
                 gg_ref, gb_ref, w2_ref, a2_ref, g2_ref, s0_ref,
                 o_ref, st_ref,
                 s_scr, br_scr, bk_scr, bv_scr, bx_scr, *, c, g, nb, l_valid, n_chunks):
    ci = pl.program_id(2)
    half = RWKV_HEAD
    w = g * LANES
    gs = 1
    nh = 2 * gs
    n = nh * c

    @pl.when(ci == 0)
    def _():
        s_scr[...] = s0_ref[...]
        br_scr[:, 7:8, :] = sr_ref[...]
        bk_scr[:, 7:8, :] = sk_ref[...]
        bv_scr[:, 7:8, :] = sv_ref[...]
        bx_scr[:, 7:8, :] = sx_ref[...]

    ones_w = ((_iota((w, w), 0) // half) == (_iota((w, w), 1) // half)).astype(BF16)
    ones_p = ((_iota((LANES, LANES), 0) // half) == (_iota((LANES, LANES), 1) // half)).astype(F32)
    tri = (_iota((c, c), 1) <= _iota((c, c), 0)).astype(BF16)
    lane_c = _iota((1, LANES), 1)
    m2 = [(lane_c < half).astype(F32), (lane_c >= half).astype(F32)]
    ri = _iota((n, n), 0)
    cj = _iota((n, n), 1)
    in_blk = lambda i, b: jnp.logical_and(i >= b * c, i < (b + 1) * c)
    same = jnp.logical_and(in_blk(ri, 0), in_blk(cj, 0))
    for b in range(1, nh):
        same = jnp.logical_or(same, jnp.logical_and(in_blk(ri, b), in_blk(cj, b)))
    strict = jnp.logical_and(same, cj < ri)
    incl = jnp.logical_and(same, cj <= ri)
    eye = (ri == cj).astype(F32)
    n_doub = max(1, math.ceil(math.log2(c)))
    w2b = w2_ref[...].astype(BF16)
    a2b = a2_ref[...].astype(BF16)
    g2b = g2_ref[...].astype(BF16)
    pair = lambda t, p: t[:, p * LANES:(p + 1) * LANES]

    def seg_sum(t):
        return _dot_exact(_split(t, 3), ones_w)

    def sequence(bi):
        def shifted(p_ref, buf, mu_ref):
            p = p_ref[bi]
            buf[bi, 8:8 + c, :] = p
            prev = buf[bi, 7:7 + c, :]
            buf[bi, 7:8, :] = p[c - 1:c, :]
            return p + (prev - p) * mu_ref[...]

        r = shifted(pr_ref, br_scr, mur_ref)
        k = shifted(pk_ref, bk_scr, muk_ref)
        v = shifted(pv_ref, bv_scr, muv_ref)
        x = shifted(px_ref, bx_scr, mux_ref)
        yield

        wl = w0_ref[...] + _dot(jnp.tanh(x).astype(BF16), w2b)
        wl = -jax.nn.softplus(-wl) - 0.5
        logw = -jnp.exp(wl)
        a = jax.nn.sigmoid(a0_ref[...] + _dot(x.astype(BF16), a2b))
        gate = _dot(jax.nn.sigmoid(x).astype(BF16), g2b)
        kk = k * kk_ref[...]
        yield
        kk = kk / jnp.maximum(jnp.sqrt(seg_sum(kk * kk)), 1e-12)
        k = k * (1.0 + (a - 1.0) * ka_ref[...])
        be = kk * a
        if l_valid < c * n_chunks:
            ok = (_iota((c, w), 0) + ci * c) < l_valid
            logw = jnp.where(ok, logw, 0.0)
            kk = jnp.where(ok, kk, 0.0)
            be = jnp.where(ok, be, 0.0)
            k = jnp.where(ok, k, 0.0)
            v = jnp.where(ok, v, 0.0)

        yield
        cum = _dot_exact(_split(logw, 3), tri, lambda p, t: _dot(t, p))
        yield
        tot = cum[c - 1:c, :]
        ah = -kk * jnp.exp(cum - logw)
        rh = r * jnp.exp(cum)
        ieg = jnp.exp(-cum)
        bc = be * ieg
        kc = k * ieg
        etail = jnp.exp(tot - cum)
        bt = be * etail
        kt = k * etail

        o_parts = {}

        def system(grp):
            stack = lambda t: jnp.concatenate(
                [pair(t, p) * m2[q] for p in grp for q in range(2)], axis=0)
            ars = _split(jnp.concatenate([stack(ah), stack(rh)], axis=0))
            vs = stack(v)
            yield
            gb = _dot3(ars, _split(stack(bc)), _dot_nt)
            yield
            gk = _dot3(ars, _split(stack(kc)), _dot_nt)
            xm = jnp.where(strict, gb[:n], 0.0)
            tm = eye + xm
            xp = _split(xm)
            yield
            if n_doub > 1:
                xp = _split(_dot3(xp, xp))
                yield
                for _ in range(n_doub - 2):
                    ts = _split(tm)
                    both = _dot3([jnp.concatenate([xp[i], ts[i]], axis=0) for i in range(2)], xp)
                    tm = tm + both[n:]
                    xp = _split(both[:n])
                    yield
                tm = tm + _dot3(_split(tm), xp)
                yield
            lm = jnp.concatenate([jnp.where(strict, gk[:n], 0.0), jnp.where(incl, gk[n:], 0.0)], axis=0)
            lmv = _dot3(_split(lm), _split(vs))
            mrb = jnp.where(incl, gb[n:], 0.0)
            yield

            y0 = [_dot3(_split(jnp.concatenate([pair(ah, p), pair(rh, p)], axis=0)),
                        _split(s_scr[bi, p]), _dot_nt) for p in grp]
            yield
            ys = jnp.concatenate([y[:c] * m2[q] for y in y0 for q in range(2)], axis=0) + lmv[:n]
            ps = _dot3(_split(tm), _split(ys))
            yield
            os_ = _dot3(_split(mrb), _split(ps)) + lmv[n:]
            yield
            for i, p in enumerate(grp):
                lo, mid, hi = 2 * i * c, (2 * i + 1) * c, (2 * i + 2) * c
                o_parts[p] = y0[i][c:] + os_[lo:mid] + os_[mid:hi]
                pv = jnp.concatenate([ps[lo:mid] + ps[mid:hi], pair(v, p)], axis=0)
                bk = jnp.concatenate([pair(bt, p), pair(kt, p)], axis=0)
                upd = _dot3(_split(pv), _split(bk), _dot_tn)
                s_scr[bi, p] = (s_scr[bi, p] * jnp.exp(pair(tot, p)) + upd) * ones_p
                yield

        systems = [system(list(range(s, s + gs))) for s in range(0, g, gs)]
        while systems:
            systems = [s for s in systems if next(s, True) is None]
            yield

        o = jnp.concatenate([o_parts[p] for p in range(g)], axis=1) if g > 1 else o_parts[0]
        inv_n = 1.0 / half
        mean = seg_sum(o) * inv_n
        yield
        d = o - mean
        var = seg_sum(d * d) * inv_n
        yield
        on = d * lax.rsqrt(var + GN_EPS) * gg_ref[...] + gb_ref[...]
        bonus = seg_sum(r * k * rk_ref[...]) * v
        o_ref[bi] = ((on + bonus) * gate).astype(o_ref.dtype)

    live = [sequence(bi) for bi in range(nb)]
    while live:
        live = [s for s in live if next(s, True) is None]

    @pl.when(ci == n_chunks - 1)
    def _():
        st_ref[...] = s_scr[...]


def _rwkv(proj3, shift0, s0, mu, w0, w2, a0, a2, g2, k_k, k_a, r_k, gn_g, gn_b, *, c, l_valid,
          l_out, o_width):
    b, lp, _ = proj3.shape
    cw = w0.shape[-1]
    npair = cw // LANES
    heads = cw // RWKV_HEAD
    g = 2 if npair % 2 == 0 else 1
    w = g * LANES
    ng = npair // g
    lw = mu.shape[-1] - 3 * cw
    assert lw % LANES == 0 and (3 * cw) % lw == 0 and lp % c == 0
    n_chunks = lp // c
    wl_, al_ = w2.shape[0], a2.shape[0]
    w2p = jnp.zeros((lw, cw), F32).at[:wl_].set(w2)
    a2p = jnp.zeros((lw, cw), F32).at[wl_:wl_ + al_].set(a2)
    g2p = jnp.zeros((lw, cw), F32).at[wl_ + al_:].set(g2)
    s0p = s0.reshape(b, npair, 2, RWKV_HEAD, RWKV_HEAD)
    z = jnp.zeros_like(s0p[:, :, 0])
    s0bd = jnp.concatenate([jnp.concatenate([s0p[:, :, 0], z], axis=-1),
                            jnp.concatenate([z, s0p[:, :, 1]], axis=-1)], axis=-2)
    sh3 = shift0.reshape(b, 1, -1)
    mu2 = mu.reshape(1, -1)
    row = lambda t: t.reshape(1, cw)
    xoff = (3 * cw) // lw

    nb = _largest_divisor(b, 4, 1)
    seg = lambda s: pl.BlockSpec((nb, c, w), lambda i, j, t: (i, t, s * ng + j))
    sseg = lambda s: pl.BlockSpec((nb, 1, w), lambda i, j, t: (i, 0, s * ng + j))
    mseg = lambda s: pl.BlockSpec((1, w), lambda i, j, t: (0, s * ng + j))
    vec = pl.BlockSpec((1, w), lambda i, j, t: (0, j))
    lora = pl.BlockSpec((lw, w), lambda i, j, t: (0, j))
    state = pl.BlockSpec((nb, g, LANES, LANES), lambda i, j, t: (i, j, 0, 0))
    in_specs = [seg(0), seg(1), seg(2), pl.BlockSpec((nb, c, lw), lambda i, j, t: (i, t, xoff)),
                sseg(0), sseg(1), sseg(2), pl.BlockSpec((nb, 1, lw), lambda i, j, t: (i, 0, xoff)),
                mseg(0), mseg(1), mseg(2), pl.BlockSpec((1, lw), lambda i, j, t: (0, xoff)),
                vec, vec, vec, vec, vec, vec, vec, lora, lora, lora, state]
    o, st = pl.pallas_call(
        functools.partial(_rwkv_kernel, c=c, g=g, nb=nb, l_valid=l_valid, n_chunks=n_chunks),
        grid=(b // nb, ng, n_chunks),
        in_specs=in_specs,
        out_specs=[pl.BlockSpec((nb, c, w), lambda i, j, t: (i, t, j)), state],
        out_shape=[jax.ShapeDtypeStruct((b, l_out, o_width), BF16),
                   jax.ShapeDtypeStruct((b, npair, LANES, LANES), F32)],
        scratch_shapes=[pltpu.VMEM((nb, g, LANES, LANES), F32),
                        pltpu.VMEM((nb, c + 8, w), F32), pltpu.VMEM((nb, c + 8, w), F32),
                        pltpu.VMEM((nb, c + 8, w), F32), pltpu.VMEM((nb, c + 8, lw), F32)],
        compiler_params=_params("parallel", "parallel", "arbitrary"),
        name="rwkv7_chunk",
    )(proj3, proj3, proj3, proj3, sh3, sh3, sh3, sh3, mu2, mu2, mu2, mu2,
      row(w0), row(a0), row(k_k), row(k_a), row(r_k), row(gn_g), row(gn_b), w2p, a2p, g2p, s0bd)
    st = jnp.stack([st[:, :, :RWKV_HEAD, :RWKV_HEAD], st[:, :, RWKV_HEAD:, RWKV_HEAD:]], axis=2)
    return o, st.reshape(b, heads, RWKV_HEAD, RWKV_HEAD)


def _logf_kernel(z_ref, bf_ref, lf_ref, cc_ref, cr_ref, *, rb):
    l, h = z_ref.shape[1], z_ref.shape[2]
    ti = _iota((rb, rb), 0)
    si = _iota((rb, rb), 1)
    tri = (si <= ti).astype(F32)
    carry = jnp.zeros((1, h), F32)
    for i in range(l // rb):
        sl = slice(i * rb, (i + 1) * rb)
        lf = jax.nn.log_sigmoid(z_ref[0, sl, :] + bf_ref[...])
        lf_ref[0, sl, :] = lf
        cblk = carry + _dot(tri, lf, HI)
        cc_ref[0, sl, :] = cblk
        carry = cblk[rb - 1:rb, :]
    eye = (_iota((h, h), 0) == _iota((h, h), 1)).astype(F32)
    cr_ref[0] = _dot_nt(eye, cc_ref[0], HI)


def _logf(z3, b_forget):
    b, l, h = z3.shape
    rb = _largest_divisor(l, 512, 8)
    blk = pl.BlockSpec((1, l, h), lambda i: (i, 0, 0))
    return pl.pallas_call(
        functools.partial(_logf_kernel, rb=rb),
        grid=(b,),
        in_specs=[blk, pl.BlockSpec((1, h), lambda i: (0, 0))],
        out_specs=[blk, blk, pl.BlockSpec((1, h, l), lambda i: (i, 0, 0))],
        out_shape=[jax.ShapeDtypeStruct((b, l, h), F32), jax.ShapeDtypeStruct((b, l, h), F32),
                   jax.ShapeDtypeStruct((b, h, l), F32)],
        compiler_params=_params("parallel"),
        name="fox_logf_cumsum",
    )(z3, b_forget.reshape(1, h))


def _fox_prompt_kernel(q_ref, k_ref, v_ref, cc_ref, cr_ref, buf_ref, o_ref, *, tq):
    del buf_ref
    l = q_ref.shape[1]
    nh = cc_ref.shape[2]
    h = pl.program_id(1)
    scale = FOX_HEAD ** -0.5
    sel = (_iota((nh, LANES), 0) == h).astype(F32)
    cq_all = _dot(cc_ref[0], sel, HI)
    ck = cr_ref[0, pl.ds(h, 1), :]
    kb = k_ref[0].astype(BF16)
    vb = v_ref[0].astype(BF16)
    for i in range(l // tq):
        nk = (i + 1) * tq
        rows = slice(i * tq, nk)
        s = _dot_nt(q_ref[0, rows, :].astype(BF16), kb[:nk]) * scale
        s = s + cq_all[rows, 0:1] - ck[:, :nk]
        qpos = _iota((tq, nk), 0) + i * tq
        kpos = _iota((tq, nk), 1)
        s = jnp.where(qpos >= kpos, s, NEG_INF)
        m = jnp.max(s, axis=-1, keepdims=True)
        p = jnp.exp(s - m)
        den = jnp.sum(p, axis=-1, keepdims=True)
        o = _dot(p.astype(BF16), vb[:nk]) / den
        o_ref[0, rows, :] = o.astype(o_ref.dtype)


def _fox_prompt(proj3, q_off, k3, v3, c_col, c_row, o_buf, o_off):
    b, l, _ = k3.shape
    nh = c_col.shape[2]
    tq = _largest_divisor(l, 384, 8)
    blk = lambda off: pl.BlockSpec((1, l, LANES), lambda i, j: (i, 0, off + j))
    return pl.pallas_call(
        functools.partial(_fox_prompt_kernel, tq=tq),
        grid=(b, nh),
        in_specs=[blk(q_off), blk(0), blk(0),
                  pl.BlockSpec((1, l, nh), lambda i, j: (i, 0, 0)),
                  pl.BlockSpec((1, nh, l), lambda i, j: (i, 0, 0)),
                  pl.BlockSpec(memory_space=pl.ANY)],
        out_specs=blk(o_off),
        out_shape=jax.ShapeDtypeStruct(o_buf.shape, o_buf.dtype),
        input_output_aliases={5: 0},
        compiler_params=_params("parallel", "parallel"),
        name="fox_prompt_attention",
    )(proj3, k3, v3, c_col, c_row, o_buf)


def _page_tail_kernel(lf_ref, tail_ref, tot_ref):
    pp = lf_ref.shape[0]
    later = (_iota((PAGE_SIZE, PAGE_SIZE), 1) > _iota((PAGE_SIZE, PAGE_SIZE), 0)).astype(BF16)
    for i in range(pp):
        lf = lf_ref[i]
        tail_ref[i] = _dot_exact(_split(lf, 3), later, lambda p, u: _dot(u, p))
        tot_ref[i] = jnp.broadcast_to(jnp.sum(lf, axis=0, keepdims=True), lf.shape)


def _page_tails(lf_pool):
    n_phys, _, nh = lf_pool.shape
    pp = _largest_divisor(n_phys, 32, 1)
    blk = pl.BlockSpec((pp, PAGE_SIZE, nh), lambda i: (i, 0, 0))
    shape = jax.ShapeDtypeStruct(lf_pool.shape, F32)
    return pl.pallas_call(
        _page_tail_kernel,
        grid=(n_phys // pp,),
        in_specs=[blk],
        out_specs=[blk, blk],
        out_shape=[shape, shape],
        compiler_params=_params("parallel"),
        name="fox_page_tails",
    )(lf_pool)


def _fox_sample_kernel(pt_ref, q_ref, kn_ref, vn_ref, cq_ref, ck_ref, *rest, nh, n_steps, pg):
    kp = rest[:pg]
    vp = rest[pg:2 * pg]
    tl = rest[2 * pg:3 * pg]
    tt = rest[3 * pg:4 * pg]
    o_ref, m_scr, l_scr, acc_scr, suf_scr = rest[4 * pg:]
    pi = pl.program_id(1)
    rows = q_ref.shape[1]
    cols = PAGE_SIZE * nh
    scale = FOX_HEAD ** -0.5

    @pl.when(pi == 0)
    def _():
        m_scr[...] = jnp.full(m_scr.shape, NEG_INF, F32)
        l_scr[...] = jnp.zeros(l_scr.shape, F32)
        acc_scr[...] = jnp.zeros(acc_scr.shape, F32)
        suf_scr[...] = jnp.zeros(suf_scr.shape, F32)

    qb = q_ref[0].astype(BF16)
    cnew = cq_ref[0]

    def online(scores, vals):
        m_old = m_scr[...]
        m_new = m_old
        for s in scores:
            m_new = jnp.maximum(m_new, jnp.max(s, axis=-1, keepdims=True))
        corr = jnp.exp(m_old - m_new)
        den = l_scr[...] * corr
        acc = acc_scr[...] * corr
        for s, val in zip(scores, vals):
            pe = jnp.exp(s - m_new)
            den = den + jnp.sum(pe, axis=-1, keepdims=True)
            acc = acc + _dot(pe.astype(BF16), val)
        l_scr[...] = den
        acc_scr[...] = acc
        m_scr[...] = m_new

    own_head = (_iota((rows, cols), 0) % nh) == (_iota((rows, cols), 1) % nh)
    suf = suf_scr[...]
    bias = [None] * pg
    for gi in reversed(range(pg)):
        bias[gi] = suf + tl[gi][0]
        suf = suf + tt[gi][0]
    suf_scr[...] = suf
    scores = []
    for gi in range(pg):
        kflat = kp[gi][0].reshape(cols, FOX_HEAD).astype(BF16)
        s = _dot_nt(qb, kflat) * scale + bias[gi] + cnew
        scores.append(jnp.where(own_head, s, NEG_INF))
    online(scores, [r[0].reshape(cols, FOX_HEAD).astype(BF16) for r in vp])

    @pl.when(pi == n_steps - 1)
    def _():
        sn = _dot_nt(qb, kn_ref[0].astype(BF16)) * scale + cnew - ck_ref[0]
        ri = _iota((rows, rows), 0)
        cj = _iota((rows, rows), 1)
        ok = jnp.logical_and((ri % nh) == (cj % nh), (cj // nh) <= (ri // nh))
        online([jnp.where(ok, sn, NEG_INF)], [vn_ref[0].astype(BF16)])
        o_ref[0] = (acc_scr[...] / l_scr[...]).astype(o_ref.dtype)


def _fox_sample(q, k_new, v_new, c_col, k_pool, v_pool, lf_pool, page_table):
    bd, t_new, width = q.shape
    nh = c_col.shape[2]
    n_pages = page_table.shape[1]
    n_phys = k_pool.shape[0]
    pg = 4 if n_pages % 4 == 0 else 1
    n_steps = n_pages // pg
    rows = t_new * nh
    cols = PAGE_SIZE * nh
    tail, tot = _page_tails(lf_pool)
    tail = tail.reshape(n_phys, 1, cols)
    tot = tot.reshape(n_phys, 1, cols)
    by_head = lambda a: a.reshape(bd, rows, FOX_HEAD)
    full = lambda shape: pl.BlockSpec((1,) + shape, lambda i, p, pt: (i, 0, 0))

    def page(shape, slot):
        zeros = (0,) * len(shape)
        return pl.BlockSpec((1,) + shape,
                            lambda i, p, pt: (pt[i, n_pages - (p + 1) * pg + slot],) + zeros)

    slots = range(pg)
    kv_page = (PAGE_SIZE, nh, FOX_HEAD)
    grid_spec = pltpu.PrefetchScalarGridSpec(
        num_scalar_prefetch=1,
        grid=(bd, n_steps),
        in_specs=([full((rows, FOX_HEAD)), full((rows, FOX_HEAD)), full((rows, FOX_HEAD)),
                   full((rows, 1)), full((1, rows))]
                  + [page(kv_page, s) for s in slots] + [page(kv_page, s) for s in slots]
                  + [page((1, cols), s) for s in slots] + [page((1, cols), s) for s in slots]),
        out_specs=full((rows, FOX_HEAD)),
        scratch_shapes=[pltpu.VMEM((rows, 1), F32), pltpu.VMEM((rows, 1), F32),
                        pltpu.VMEM((rows, FOX_HEAD), F32), pltpu.VMEM((1, cols), F32)],
    )
    out = pl.pallas_call(
        functools.partial(_fox_sample_kernel, nh=nh, n_steps=n_steps, pg=pg),
        grid_spec=grid_spec,
        out_shape=jax.ShapeDtypeStruct((bd, rows, FOX_HEAD), BF16),
        compiler_params=_params("parallel", "arbitrary"),
        name="fox_sample_attention",
    )(page_table, by_head(q), by_head(k_new), by_head(v_new),
      c_col.reshape(bd, rows, 1), c_col.reshape(bd, 1, rows),
      *([k_pool] * pg), *([v_pool] * pg), *([tail] * pg), *([tot] * pg))
    return out.reshape(bd, t_new, width)


def _conv_kernel(bg_ref, cg_ref, h_ref, w_ref, buf_ref, y_ref, nb_ref, u_scr, *, tt, n_t):
    ti = pl.program_id(2)

    @pl.when(ti == 0)
    def _():
        u_scr[6:8, :] = buf_ref[0]

    u = cg_ref[0] * h_ref[0]
    u_scr[8:8 + tt, :] = u
    y = (w_ref[0:1, :] * u_scr[6:6 + tt, :] + w_ref[1:2, :] * u_scr[7:7 + tt, :]
         + w_ref[2:3, :] * u)
    y_ref[0] = (bg_ref[0] * y).astype(y_ref.dtype)
    tail = u_scr[6 + tt:8 + tt, :]
    u_scr[6:8, :] = tail

    @pl.when(ti == n_t - 1)
    def _():
        nb_ref[0] = tail


def _short_conv(proj3, buf0, conv_w):
    b, l, d3 = proj3.shape
    d = d3 // 3
    tc = _largest_divisor(d, 512, LANES)
    tt = _largest_divisor(l, 704, 8)
    n_t = l // tt
    nc = d // tc
    seg = lambda s: pl.BlockSpec((1, tt, tc), lambda i, j, t: (i, t, s * nc + j))
    return pl.pallas_call(
        functools.partial(_conv_kernel, tt=tt, n_t=n_t),
        grid=(b, nc, n_t),
        in_specs=[seg(0), seg(1), seg(2),
                  pl.BlockSpec((conv_w.shape[0], tc), lambda i, j, t: (0, j)),
                  pl.BlockSpec((1, 2, tc), lambda i, j, t: (i, 0, j))],
        out_specs=[pl.BlockSpec((1, tt, tc), lambda i, j, t: (i, t, j)),
                   pl.BlockSpec((1, 2, tc), lambda i, j, t: (i, 0, j))],
        out_shape=[jax.ShapeDtypeStruct((b, l, d), BF16), jax.ShapeDtypeStruct((b, 2, d), F32)],
        scratch_shapes=[pltpu.VMEM((tt + 8, tc), F32)],
        compiler_params=_params("parallel", "parallel", "arbitrary"),
        name="short_conv",
    )(proj3, proj3, proj3, conv_w, buf0)


def _trunk(h3, s0, shift0, conv0, fox_ctx, wts, *, prompt):
    (norm_mix, norm_mlp, w_in_even, b_forget, rwkv_mu, rwkv_w0, rwkv_w2, rwkv_a0, rwkv_a2, rwkv_g2,
     rwkv_k_k, rwkv_k_a, rwkv_r_k, rwkv_gn_g, rwkv_gn_b, w_out_even, w_in_odd, conv_w, w_out_odd,
     w_up, w_down) = wts
    b, l, d = h3.shape
    m = b * l
    h = h3.reshape(m, d)
    cw = rwkv_w0.shape[-1]
    rproj = rwkv_mu.shape[-1]
    fw = d - cw
    nh = fw // FOX_HEAD
    n_main = rproj + 3 * fw
    rw = (rwkv_mu[0], rwkv_w0[0], rwkv_w2[0], rwkv_a0[0], rwkv_a2[0], rwkv_g2[0], rwkv_k_k[0],
          rwkv_k_a[0], rwkv_r_k[0], rwkv_gn_g[0], rwkv_gn_b[0])

    lp = RWKV_CHUNK * pl.cdiv(l, RWKV_CHUNK) if prompt else l
    if lp > l:
        hn, hn_pad = _rmsnorm_pad(h3, norm_mix[0], lp)
        hn = hn.reshape(m, d)
        hn_pad = hn_pad.reshape(b * lp, d)
    else:
        hn = hn_pad = _rmsnorm(h, norm_mix[0], BF16)
    proj3 = _matmul(hn_pad, w_in_even, 0, n_cols=rproj + fw, tn=256, tk=4096).reshape(b, lp, rproj + fw)
    k3 = _matmul(hn, w_in_even, 0, n_lo=rproj + fw, n_cols=fw, tn=256, tk=4096).reshape(b, l, fw)
    v3 = _matmul(hn, w_in_even, 0, n_lo=rproj + 2 * fw, n_cols=fw, tn=256, tk=4096).reshape(b, l, fw)
    z = _matmul(hn, w_in_even[:, :, n_main:], 0, tn=nh, tk=4096)
    lf, c_col, c_row = _logf(z.reshape(b, l, nh), b_forget[0])
    if prompt:
        o_r, s_t = _rwkv(proj3, shift0[0], s0[0], *rw, c=RWKV_CHUNK, l_valid=l, l_out=l, o_width=d)
        mix_in = _fox_prompt(proj3, rproj // LANES, k3, v3, c_col, c_row, o_r, cw // LANES)
    else:
        lp = 8 * pl.cdiv(l, 8)
        projp = jnp.pad(proj3[:, :, :rproj], ((0, 0), (0, lp - l), (0, 0)))
        o_r, s_t = _rwkv(projp, shift0[0], s0[0], *rw, c=lp, l_valid=l, l_out=lp, o_width=cw)
        k_pool, v_pool, lf_pool, page_table = fox_ctx
        o_f = _fox_sample(proj3[:, :, rproj:], k3, v3, c_col, k_pool[0], v_pool[0], lf_pool[0],
                          page_table)
        mix_in = jnp.concatenate([o_r[:, :l], o_f], axis=-1)
    sh_t = proj3[:, l - 1, :rproj]
    k_out = k3.reshape(b, l, nh, FOX_HEAD)
    v_out = v3.reshape(b, l, nh, FOX_HEAD)
    h = _matmul(mix_in.reshape(m, d), w_out_even, 0, tn=256, tk=4096, epilogue="resid", resid=h)
    hn = _rmsnorm(h, norm_mlp[0], BF16)
    u = _matmul(hn, w_up, 0, tn=256, tk=4096, epilogue="relu2", out_dtype=BF16)
    h = _matmul(u, w_down, 0, tn=512, tk=2048, epilogue="resid", resid=h)

    hn = _rmsnorm(h, norm_mix[1], BF16)
    proj1 = _matmul(hn, w_in_odd, 0, tn=256, tk=4096)
    y, buf = _short_conv(proj1.reshape(b, l, 3 * d), conv0[0], conv_w[0])
    h = _matmul(y.reshape(m, d), w_out_odd, 0, tn=256, tk=4096, epilogue="resid", resid=h)
    hn = _rmsnorm(h, norm_mlp[1], BF16)
    u = _matmul(hn, w_up, 1, tn=256, tk=4096, epilogue="relu2", out_dtype=BF16)
    h = _matmul(u, w_down, 1, tn=512, tk=2048, epilogue="resid", resid=h)
    return (h.reshape(b, l, d), s_t[None], sh_t[None], k_out[None], v_out[None], lf[None], buf[None])


def kernel(x_prompt, x_sample, state_rwkv, state_rwkv_shift, cache_fox_k, cache_fox_v, cache_fox_logf, state_conv, page_table, meta_tokens, norm_mix, norm_mlp, norm_final, w_in_even, b_forget, rwkv_mu, rwkv_w0, rwkv_w2, rwkv_a0, rwkv_a2, rwkv_g2, rwkv_k_k, rwkv_k_a, rwkv_r_k, rwkv_gn_g, rwkv_gn_b, w_out_even, w_in_odd, conv_w, w_out_odd, w_up, w_down):
    wts = (norm_mix, norm_mlp, w_in_even, b_forget, rwkv_mu, rwkv_w0, rwkv_w2, rwkv_a0, rwkv_a2,
           rwkv_g2, rwkv_k_k, rwkv_k_a, rwkv_r_k, rwkv_gn_g, rwkv_gn_b, w_out_even, w_in_odd, conv_w,
           w_out_odd, w_up, w_down)
    bp, _, d = x_prompt.shape
    n_even, _, heads, hd, _ = state_rwkv.shape
    n_odd = state_conv.shape[0]
    rproj = state_rwkv_shift.shape[-1]

    h0 = jnp.concatenate([jnp.broadcast_to(meta_tokens[None], (bp, N_META, d)), x_prompt], axis=1)
    zs = jnp.zeros((n_even, bp, heads, hd, hd), F32)
    zsh = jnp.zeros((n_even, bp, rproj), F32)
    zconv = jnp.zeros((n_odd, bp, state_conv.shape[2], d), F32)
    hp, p_s, p_sh, p_k, p_v, p_lf, p_conv = _trunk(h0, zs, zsh, zconv, None, wts, prompt=True)
    y_prompt = _final_norm_prompt(hp, norm_final)

    hs, s_s, s_sh, s_k, s_v, s_lf, s_conv = _trunk(
        x_sample, state_rwkv, state_rwkv_shift, state_conv,
        (cache_fox_k, cache_fox_v, cache_fox_logf, page_table), wts, prompt=False)
    bs, ts, _ = x_sample.shape
    y_sample = _rmsnorm(hs.reshape(bs * ts, d), norm_final, F32).reshape(bs, ts, d)
    return (y_prompt, y_sample, p_s, p_sh, p_k, p_v, p_lf, p_conv, s_s, s_sh, s_k, s_v, s_lf, s_conv)
```

```python
import functools
import math

import jax
import jax.numpy as jnp
from jax import lax
from jax.experimental import pallas as pl
from jax.experimental.pallas import tpu as pltpu

F32 = jnp.float32
BF16 = jnp.bfloat16
HI = lax.Precision.HIGHEST

RMS_EPS = 1e-6
GN_EPS = 64e-5
NEG_INF = -1e30
N_META = 16
RWKV_HEAD = 64
FOX_HEAD = 128
PAGE_SIZE = 128
LANES = 128
VMEM_LIMIT = 62 * 1024 * 1024
BF16_ROWS = 16
RWKV_CHUNK = 48


def _largest_divisor(n, cap, mult):
    best = None
    for d in range(mult, min(n, cap) + 1, mult):
        if n % d == 0:
            best = d
    return n if best is None else best


def _params(*sem, flags=None):
    return pltpu.CompilerParams(dimension_semantics=sem, vmem_limit_bytes=VMEM_LIMIT, flags=flags)


def _dot(a, b, precision=None):
    return jnp.dot(a, b, precision=precision, preferred_element_type=F32)


def _dot_nt(a, b, precision=None):
    return lax.dot_general(a, b, (((1,), (1,)), ((), ())), precision=precision,
                           preferred_element_type=F32)


def _dot_tn(a, b, precision=None):
    return lax.dot_general(a, b, (((0,), (0,)), ((), ())), precision=precision,
                           preferred_element_type=F32)


def _iota(shape, dim):
    return lax.broadcasted_iota(jnp.int32, shape, dim)


def _split(x, pieces=2):
    out = []
    for i in range(pieces):
        p = x.astype(BF16)
        out.append(p)
        if i + 1 < pieces:
            x = x - p.astype(F32)
    return out


def _dot3(a, b, f=_dot):
    axis = 1 if f is _dot_tn else 0
    m = a[0].shape[axis]
    if m % BF16_ROWS:
        return f(a[0], b[0]) + (f(a[0], b[1]) + f(a[1], b[0]))
    t = f(jnp.concatenate([a[0], a[1]], axis=axis), b[0])
    return (t[:m] + t[m:]) + f(a[0], b[1])


def _dot_exact(pieces, other, f=_dot):
    m = pieces[0].shape[0]
    if f is _dot and m % BF16_ROWS == 0:
        t = f(jnp.concatenate(pieces, axis=0), other)
        return sum(t[i * m:(i + 1) * m] for i in range(1, len(pieces))) + t[:m]
    acc = f(pieces[0], other)
    for p in pieces[1:]:
        acc = acc + f(p, other)
    return acc


def _rmsnorm_kernel(x_ref, g_ref, o_ref):
    x = x_ref[...]
    ms = jnp.mean(x * x, axis=-1, keepdims=True)
    o_ref[...] = (x * lax.rsqrt(ms + RMS_EPS) * g_ref[...]).astype(o_ref.dtype)


def _rmsnorm(x2d, g, out_dtype):
    m, d = x2d.shape
    tr = _largest_divisor(m, 512, 16)
    return pl.pallas_call(
        _rmsnorm_kernel,
        grid=(m // tr,),
        in_specs=[pl.BlockSpec((tr, d), lambda i: (i, 0)),
                  pl.BlockSpec((1, d), lambda i: (0, 0))],
        out_specs=pl.BlockSpec((tr, d), lambda i: (i, 0)),
        out_shape=jax.ShapeDtypeStruct((m, d), out_dtype),
        compiler_params=_params("parallel"),
        name="rmsnorm",
    )(x2d, g.reshape(1, d))


def _rmsnorm_pad_kernel(x_ref, g_ref, o_ref, op_ref, *, n_real):
    x = x_ref[0]
    ms = jnp.mean(x * x, axis=-1, keepdims=True)
    y = (x * lax.rsqrt(ms + RMS_EPS) * g_ref[...]).astype(o_ref.dtype)
    o_ref[0] = y
    op_ref[0] = jnp.where(pl.program_id(1) < n_real, y, jnp.zeros_like(y))


def _rmsnorm_pad(h3, g, lp):
    b, l, d = h3.shape
    tr = _largest_divisor(math.gcd(l, lp), 512, BF16_ROWS)
    n_real = l // tr
    real = pl.BlockSpec((1, tr, d), lambda i, j: (i, jnp.minimum(j, n_real - 1), 0))
    return pl.pallas_call(
        functools.partial(_rmsnorm_pad_kernel, n_real=n_real),
        grid=(b, lp // tr),
        in_specs=[real, pl.BlockSpec((1, d), lambda i, j: (0, 0))],
        out_specs=[real, pl.BlockSpec((1, tr, d), lambda i, j: (i, j, 0))],
        out_shape=[jax.ShapeDtypeStruct((b, l, d), BF16), jax.ShapeDtypeStruct((b, lp, d), BF16)],
        compiler_params=_params("parallel", "arbitrary"),
        name="rmsnorm_pad",
    )(h3, g.reshape(1, d))


def _final_norm_prompt_kernel(a_ref, b_ref, g_ref, o_ref):
    x = jnp.concatenate([a_ref[0, N_META:, :], b_ref[0]], axis=0)
    ms = jnp.mean(x * x, axis=-1, keepdims=True)
    o_ref[0] = x * lax.rsqrt(ms + RMS_EPS) * g_ref[...]


def _final_norm_prompt(h3, g):
    b, l, d = h3.shape
    s = l - N_META
    tr = _largest_divisor(s, 256, N_META)
    sub = tr // N_META
    return pl.pallas_call(
        _final_norm_prompt_kernel,
        grid=(b, s // tr),
        in_specs=[pl.BlockSpec((1, tr, d), lambda i, j: (i, j, 0)),
                  pl.BlockSpec((1, N_META, d), lambda i, j: (i, (j + 1) * sub, 0)),
                  pl.BlockSpec((1, d), lambda i, j: (0, 0))],
        out_specs=pl.BlockSpec((1, tr, d), lambda i, j: (i, j, 0)),
        out_shape=jax.ShapeDtypeStruct((b, s, d), F32),
        compiler_params=_params("parallel", "parallel"),
        name="final_norm_prompt",
    )(h3, h3, g.reshape(1, d))


def _mm_kernel(*refs, nk, nx, epilogue, ns):
    refs = list(refs)
    take = lambda cnt: [refs.pop(0) for _ in range(cnt)]
    xp, xs, (w_ref,) = take(nx), take(nx * ns), take(1)
    rp, rs = (take(1 + ns) + [None])[:2] if epilogue == "resid" else (None, None)
    op, osm = (take(1 + ns) + [None])[:2]
    accp, accs = (take(1 + ns) + [None])[:2] if refs else (None, None)
    i = pl.program_id(0)
    k = pl.program_id(2)
    wb = w_ref[...].astype(BF16)

    def run(x_refs, r_ref, o_ref, acc_ref):
        def finish(acc):
            if epilogue == "relu2":
                acc = jnp.square(jnp.maximum(acc, 0.0))
            elif epilogue == "resid":
                acc = r_ref[...] + acc
            o_ref[...] = acc.astype(o_ref.dtype)

        if nk == 1:
            finish(_dot(x_refs[0][...], wb))
        elif nx == 2:
            @pl.when(k == 0)
            def _():
                acc_ref[...] = _dot(x_refs[0][...], wb)

            @pl.when(k == 1)
            def _():
                finish(acc_ref[...] + _dot(x_refs[1][...], wb))
        elif epilogue == "resid":
            part = _dot(x_refs[0][...], wb)

            @pl.when(k == 0)
            def _():
                o_ref[...] = r_ref[...] + part

            @pl.when(k > 0)
            def _():
                o_ref[...] += part
        else:
            part = _dot(x_refs[0][...], wb)

            @pl.when(k == 0)
            def _():
                acc_ref[...] = part

            @pl.when(jnp.logical_and(k > 0, k < nk - 1))
            def _():
                acc_ref[...] += part

            @pl.when(k == nk - 1)
            def _():
                finish(acc_ref[...] + part)

    run(xp, rp, op, accp)
    if ns:
        @pl.when(i == 0)
        def _():
            run(xs, rs, osm, accs)


def _matmul(xp, xs, w, layer, *, n_lo=0, n_cols=None, tn, tk, epilogue="plain", resid=None,
            out_dtype=F32):
    xp = xp if isinstance(xp, tuple) else (xp,)
    ns = 0 if xs is None else 1
    xs = () if xs is None else (xs if isinstance(xs, tuple) else (xs,))
    nx = len(xp)
    m, kdim = xp[0].shape[0], nx * xp[0].shape[1]
    ms = xs[0].shape[0] if ns else 0
    n_cols = w.shape[2] - n_lo if n_cols is None else n_cols
    tn = _largest_divisor(math.gcd(n_cols, n_lo), tn, LANES)
    tk = kdim // 2 if nx == 2 else _largest_divisor(kdim, tk, LANES)
    assert n_lo % tn == 0 and n_cols % tn == 0
    tm = _largest_divisor(m, 2112, BF16_ROWS)
    nk = kdim // tk
    nj = n_cols // tn
    off = n_lo // tn
    kblk = (lambda k: k) if nx == 1 else (lambda k: 0)
    once = pl.Buffered(1) if nk == 1 or nx == 2 else None
    in_specs = ([pl.BlockSpec((tm, tk), lambda i, j, k: (i, kblk(k)), pipeline_mode=once)] * nx
                + [pl.BlockSpec((ms, tk), lambda i, j, k: (0, kblk(k)))] * (nx * ns)
                + [pl.BlockSpec((None, tk, tn), lambda i, j, k: (layer, k, j + off))])
    args = [*xp, *xs, w]
    out_specs = [pl.BlockSpec((tm, tn), lambda i, j, k: (i, j))]
    out_shape = [jax.ShapeDtypeStruct((m, n_cols), out_dtype)]
    use_acc = nk > 1 and (nx == 2 or epilogue != "resid")
    acc = [pltpu.VMEM((tm, tn), F32)] if use_acc else []
    if epilogue == "resid":
        in_specs.append(pl.BlockSpec((tm, tn), lambda i, j, k: (i, j)))
        args.append(resid[0])
    if ns:
        if epilogue == "resid":
            in_specs.append(pl.BlockSpec((ms, tn), lambda i, j, k: (0, jnp.where(i == 0, j, 0))))
            args.append(resid[1])
        out_specs.append(pl.BlockSpec((ms, tn), lambda i, j, k: (0, jnp.where(i == 0, j, nj))))
        out_shape.append(jax.ShapeDtypeStruct((ms, n_cols + tn), out_dtype))
        acc += [pltpu.VMEM((ms, tn), F32)] if use_acc else []
    outs = pl.pallas_call(
        functools.partial(_mm_kernel, nk=nk, nx=nx, epilogue=epilogue, ns=ns),
        grid=(m // tm, nj, nk),
        in_specs=in_specs,
        out_specs=out_specs,
        out_shape=out_shape,
        scratch_shapes=acc,
        compiler_params=_params("arbitrary", "arbitrary", "arbitrary"),
        name="matmul_" + epilogue,
    )(*args)
    return (outs[0], outs[1][:, :n_cols]) if ns else (outs[0], None)


def _rwkv_kernel(pr_ref, pk_ref, pv_ref, px_ref, sr_ref, sk_ref, sv_ref, sx_ref,
                 mur_ref, muk_ref, muv_ref, mux_ref, w0_ref, a0_ref, kk_ref, ka_ref, rk_ref,
                 gg_ref, gb_ref, w2_ref, a2_ref, g2_ref, s0_ref,
                 o_ref, st_ref,
                 s_scr, br_scr, bk_scr, bv_scr, bx_scr, *, c, g, nb, l_valid, n_chunks):
    ci = pl.program_id(2)
    half = RWKV_HEAD
    w = g * LANES
    gs = 1
    nh = 2 * gs
    n = nh * c

    @pl.when(ci == 0)
    def _():
        s_scr[...] = s0_ref[...]
        br_scr[:, 7:8, :] = sr_ref[...]
        bk_scr[:, 7:8, :] = sk_ref[...]
        bv_scr[:, 7:8, :] = sv_ref[...]
        bx_scr[:, 7:8, :] = sx_ref[...]

    ones_w = ((_iota((w, w), 0) // half) == (_iota((w, w), 1) // half)).astype(BF16)
    ones_p = ((_iota((LANES, LANES), 0) // half) == (_iota((LANES, LANES), 1) // half)).astype(F32)
    tri = (_iota((c, c), 1) <= _iota((c, c), 0)).astype(BF16)
    lane_c = _iota((1, LANES), 1)
    m2 = [(lane_c < half).astype(F32), (lane_c >= half).astype(F32)]
    ri = _iota((n, n), 0)
    cj = _iota((n, n), 1)
    in_blk = lambda i, b: jnp.logical_and(i >= b * c, i < (b + 1) * c)
    same = jnp.logical_and(in_blk(ri, 0), in_blk(cj, 0))
    for b in range(1, nh):
        same = jnp.logical_or(same, jnp.logical_and(in_blk(ri, b), in_blk(cj, b)))
    strict = jnp.logical_and(same, cj < ri)
    incl = jnp.logical_and(same, cj <= ri)
    eye = (ri == cj).astype(F32)
    n_doub = max(1, math.ceil(math.log2(c)))
    w2b = w2_ref[...].astype(BF16)
    a2b = a2_ref[...].astype(BF16)
    g2b = g2_ref[...].astype(BF16)
    pair = lambda t, p: t[:, p * LANES:(p + 1) * LANES]

    def seg_sum(t):
        return _dot_exact(_split(t, 3), ones_w)

    def sequence(bi):
        def shifted(p_ref, buf, mu_ref):
            p = p_ref[bi]
            buf[bi, 8:8 + c, :] = p
            prev = buf[bi, 7:7 + c, :]
            buf[bi, 7:8, :] = p[c - 1:c, :]
            return p + (prev - p) * mu_ref[...]

        r = shifted(pr_ref, br_scr, mur_ref)
        k = shifted(pk_ref, bk_scr, muk_ref)
        v = shifted(pv_ref, bv_scr, muv_ref)
        x = shifted(px_ref, bx_scr, mux_ref)
        yield

        wl = w0_ref[...] + _dot(jnp.tanh(x).astype(BF16), w2b)
        wl = -jax.nn.softplus(-wl) - 0.5
        logw = -jnp.exp(wl)
        a = jax.nn.sigmoid(a0_ref[...] + _dot(x.astype(BF16), a2b))
        gate = _dot(jax.nn.sigmoid(x).astype(BF16), g2b)
        kk = k * kk_ref[...]
        yield
        kk = kk / jnp.maximum(jnp.sqrt(seg_sum(kk * kk)), 1e-12)
        k = k * (1.0 + (a - 1.0) * ka_ref[...])
        be = kk * a
        if l_valid < c * n_chunks:
            ok = (_iota((c, w), 0) + ci * c) < l_valid
            logw = jnp.where(ok, logw, 0.0)
            kk = jnp.where(ok, kk, 0.0)
            be = jnp.where(ok, be, 0.0)
            k = jnp.where(ok, k, 0.0)
            v = jnp.where(ok, v, 0.0)

        yield
        cum = _dot_exact(_split(logw, 3), tri, lambda p, t: _dot(t, p))
        yield
        tot = cum[c - 1:c, :]
        ah = -kk * jnp.exp(cum - logw)
        rh = r * jnp.exp(cum)
        ieg = jnp.exp(-cum)
        bc = be * ieg
        kc = k * ieg
        etail = jnp.exp(tot - cum)
        bt = be * etail
        kt = k * etail

        o_parts = {}

        def system(grp):
            stack = lambda t: jnp.concatenate(
                [pair(t, p) * m2[q] for p in grp for q in range(2)], axis=0)
            ars = _split(jnp.concatenate([stack(ah), stack(rh)], axis=0))
            vs = stack(v)
            yield
            gb = _dot3(ars, _split(stack(bc)), _dot_nt)
            yield
            gk = _dot3(ars, _split(stack(kc)), _dot_nt)
            xm = jnp.where(strict, gb[:n], 0.0)
            tm = eye + xm
            xp = _split(xm)
            yield
            if n_doub > 1:
                xp = _split(_dot3(xp, xp))
                yield
                for _ in range(n_doub - 2):
                    ts = _split(tm)
                    both = _dot3([jnp.concatenate([xp[i], ts[i]], axis=0) for i in range(2)], xp)
                    tm = tm + both[n:]
                    xp = _split(both[:n])
                    yield
                tm = tm + _dot3(_split(tm), xp)
                yield
            lm = jnp.concatenate([jnp.where(strict, gk[:n], 0.0), jnp.where(incl, gk[n:], 0.0)], axis=0)
            lmv = _dot3(_split(lm), _split(vs))
            mrb = jnp.where(incl, gb[n:], 0.0)
            yield

            y0 = [_dot3(_split(jnp.concatenate([pair(ah, p), pair(rh, p)], axis=0)),
                        _split(s_scr[bi, p]), _dot_nt) for p in grp]
            yield
            ys = jnp.concatenate([y[:c] * m2[q] for y in y0 for q in range(2)], axis=0) + lmv[:n]
            ps = _dot3(_split(tm), _split(ys))
            yield
            os_ = _dot3(_split(mrb), _split(ps)) + lmv[n:]
            yield
            for i, p in enumerate(grp):
                lo, mid, hi = 2 * i * c, (2 * i + 1) * c, (2 * i + 2) * c
                o_parts[p] = y0[i][c:] + os_[lo:mid] + os_[mid:hi]
                pv = jnp.concatenate([ps[lo:mid] + ps[mid:hi], pair(v, p)], axis=0)
                bk = jnp.concatenate([pair(bt, p), pair(kt, p)], axis=0)
                upd = _dot3(_split(pv), _split(bk), _dot_tn)
                s_scr[bi, p] = (s_scr[bi, p] * jnp.exp(pair(tot, p)) + upd) * ones_p
                yield

        systems = [system(list(range(s, s + gs))) for s in range(0, g, gs)]
        while systems:
            systems = [s for s in systems if next(s, True) is None]
            yield

        o = jnp.concatenate([o_parts[p] for p in range(g)], axis=1) if g > 1 else o_parts[0]
        inv_n = 1.0 / half
        mean = seg_sum(o) * inv_n
        yield
        d = o - mean
        var = seg_sum(d * d) * inv_n
        yield
        on = d * lax.rsqrt(var + GN_EPS) * gg_ref[...] + gb_ref[...]
        bonus = seg_sum(r * k * rk_ref[...]) * v
        o_ref[bi] = ((on + bonus) * gate).astype(o_ref.dtype)

    live = [sequence(bi) for bi in range(nb)]
    while live:
        live = [s for s in live if next(s, True) is None]

    @pl.when(ci == n_chunks - 1)
    def _():
        st_ref[...] = s_scr[...]


def _rwkv(proj3, shift0, s0, mu, w0, w2, a0, a2, g2, k_k, k_a, r_k, gn_g, gn_b, *, c, l_valid,
          l_out, o_width):
    b, lp, _ = proj3.shape
    cw = w0.shape[-1]
    npair = cw // LANES
    heads = cw // RWKV_HEAD
    g = 2 if npair % 2 == 0 else 1
    w = g * LANES
    ng = npair // g
    lw = mu.shape[-1] - 3 * cw
    assert lw % LANES == 0 and (3 * cw) % lw == 0 and lp % c == 0
    n_chunks = lp // c
    wl_, al_ = w2.shape[0], a2.shape[0]
    w2p = jnp.zeros((lw, cw), F32).at[:wl_].set(w2)
    a2p = jnp.zeros((lw, cw), F32).at[wl_:wl_ + al_].set(a2)
    g2p = jnp.zeros((lw, cw), F32).at[wl_ + al_:].set(g2)
    s0p = s0.reshape(b, npair, 2, RWKV_HEAD, RWKV_HEAD)
    z = jnp.zeros_like(s0p[:, :, 0])
    s0bd = jnp.concatenate([jnp.concatenate([s0p[:, :, 0], z], axis=-1),
                            jnp.concatenate([z, s0p[:, :, 1]], axis=-1)], axis=-2)
    sh3 = shift0.reshape(b, 1, -1)
    mu2 = mu.reshape(1, -1)
    row = lambda t: t.reshape(1, cw)
    xoff = (3 * cw) // lw

    nb = _largest_divisor(b, 4, 1)
    seg = lambda s: pl.BlockSpec((nb, c, w), lambda i, j, t: (i, t, s * ng + j))
    sseg = lambda s: pl.BlockSpec((nb, 1, w), lambda i, j, t: (i, 0, s * ng + j))
    mseg = lambda s: pl.BlockSpec((1, w), lambda i, j, t: (0, s * ng + j))
    vec = pl.BlockSpec((1, w), lambda i, j, t: (0, j))
    lora = pl.BlockSpec((lw, w), lambda i, j, t: (0, j))
    state = pl.BlockSpec((nb, g, LANES, LANES), lambda i, j, t: (i, j, 0, 0))
    in_specs = [seg(0), seg(1), seg(2), pl.BlockSpec((nb, c, lw), lambda i, j, t: (i, t, xoff)),
                sseg(0), sseg(1), sseg(2), pl.BlockSpec((nb, 1, lw), lambda i, j, t: (i, 0, xoff)),
                mseg(0), mseg(1), mseg(2), pl.BlockSpec((1, lw), lambda i, j, t: (0, xoff)),
                vec, vec, vec, vec, vec, vec, vec, lora, lora, lora, state]
    o, st = pl.pallas_call(
        functools.partial(_rwkv_kernel, c=c, g=g, nb=nb, l_valid=l_valid, n_chunks=n_chunks),
        grid=(b // nb, ng, n_chunks),
        in_specs=in_specs,
        out_specs=[pl.BlockSpec((nb, c, w), lambda i, j, t: (i, t, j)), state],
        out_shape=[jax.ShapeDtypeStruct((b, l_out, o_width), BF16),
                   jax.ShapeDtypeStruct((b, npair, LANES, LANES), F32)],
        scratch_shapes=[pltpu.VMEM((nb, g, LANES, LANES), F32),
                        pltpu.VMEM((nb, c + 8, w), F32), pltpu.VMEM((nb, c + 8, w), F32),
                        pltpu.VMEM((nb, c + 8, w), F32), pltpu.VMEM((nb, c + 8, lw), F32)],
        compiler_params=_params("parallel", "parallel", "arbitrary"),
        name="rwkv7_chunk",
    )(proj3, proj3, proj3, proj3, sh3, sh3, sh3, sh3, mu2, mu2, mu2, mu2,
      row(w0), row(a0), row(k_k), row(k_a), row(r_k), row(gn_g), row(gn_b), w2p, a2p, g2p, s0bd)
    st = jnp.stack([st[:, :, :RWKV_HEAD, :RWKV_HEAD], st[:, :, RWKV_HEAD:, RWKV_HEAD:]], axis=2)
    return o, st.reshape(b, heads, RWKV_HEAD, RWKV_HEAD)


def _logf_kernel(z_ref, bf_ref, lf_ref, cc_ref, cr_ref, *, rb):
    l, h = z_ref.shape[1], z_ref.shape[2]
    ti = _iota((rb, rb), 0)
    si = _iota((rb, rb), 1)
    tri = (si <= ti).astype(F32)
    carry = jnp.zeros((1, h), F32)
    for i in range(l // rb):
        sl = slice(i * rb, (i + 1) * rb)
        lf = jax.nn.log_sigmoid(z_ref[0, sl, :] + bf_ref[...])
        lf_ref[0, sl, :] = lf
        cblk = carry + _dot(tri, lf, HI)
        cc_ref[0, sl, :] = cblk
        carry = cblk[rb - 1:rb, :]
    eye = (_iota((h, h), 0) == _iota((h, h), 1)).astype(F32)
    cr_ref[0] = _dot_nt(eye, cc_ref[0], HI)


def _logf(z3, b_forget):
    b, l, h = z3.shape
    rb = _largest_divisor(l, 512, 8)
    blk = pl.BlockSpec((1, l, h), lambda i: (i, 0, 0))
    return pl.pallas_call(
        functools.partial(_logf_kernel, rb=rb),
        grid=(b,),
        in_specs=[blk, pl.BlockSpec((1, h), lambda i: (0, 0))],
        out_specs=[blk, blk, pl.BlockSpec((1, h, l), lambda i: (i, 0, 0))],
        out_shape=[jax.ShapeDtypeStruct((b, l, h), F32), jax.ShapeDtypeStruct((b, l, h), F32),
                   jax.ShapeDtypeStruct((b, h, l), F32)],
        compiler_params=_params("parallel"),
        name="fox_logf_cumsum",
    )(z3, b_forget.reshape(1, h))


def _fox_prompt_kernel(q_ref, k_ref, v_ref, cc_ref, cr_ref, o_ref, *, tq):
    l = q_ref.shape[1]
    nh = cc_ref.shape[2]
    h = pl.program_id(1)
    scale = FOX_HEAD ** -0.5
    sel = (_iota((nh, LANES), 0) == h).astype(F32)
    cq_all = _dot(cc_ref[0], sel, HI)
    ck = cr_ref[0, pl.ds(h, 1), :]
    kb = k_ref[0].astype(BF16)
    vb = v_ref[0].astype(BF16)
    for i in range(l // tq):
        nk = (i + 1) * tq
        rows = slice(i * tq, nk)
        s = _dot_nt(q_ref[0, rows, :].astype(BF16), kb[:nk]) * scale
        s = s + cq_all[rows, 0:1] - ck[:, :nk]
        qpos = _iota((tq, nk), 0) + i * tq
        kpos = _iota((tq, nk), 1)
        s = jnp.where(qpos >= kpos, s, NEG_INF)
        m = jnp.max(s, axis=-1, keepdims=True)
        p = jnp.exp(s - m)
        den = jnp.sum(p, axis=-1, keepdims=True)
        o = _dot(p.astype(BF16), vb[:nk]) / den
        o_ref[0, rows, :] = o.astype(o_ref.dtype)


def _fox_prompt(proj3, q_off, k3, v3, c_col, c_row):
    b, l, fw = k3.shape
    nh = c_col.shape[2]
    tq = _largest_divisor(l, 384, 8)
    blk = lambda off: pl.BlockSpec((1, l, LANES), lambda i, j: (i, 0, off + j))
    return pl.pallas_call(
        functools.partial(_fox_prompt_kernel, tq=tq),
        grid=(b, nh),
        in_specs=[blk(q_off), blk(0), blk(0),
                  pl.BlockSpec((1, l, nh), lambda i, j: (i, 0, 0)),
                  pl.BlockSpec((1, nh, l), lambda i, j: (i, 0, 0))],
        out_specs=blk(0),
        out_shape=jax.ShapeDtypeStruct((b, l, fw), BF16),
        compiler_params=_params("parallel", "parallel"),
        name="fox_prompt_attention",
    )(proj3, k3, v3, c_col, c_row)


def _page_tail_kernel(lf_ref, tail_ref, tot_ref):
    pp = lf_ref.shape[0]
    later = (_iota((PAGE_SIZE, PAGE_SIZE), 1) > _iota((PAGE_SIZE, PAGE_SIZE), 0)).astype(BF16)
    for i in range(pp):
        lf = lf_ref[i]
        tail_ref[i] = _dot_exact(_split(lf, 3), later, lambda p, u: _dot(u, p))
        tot_ref[i] = jnp.broadcast_to(jnp.sum(lf, axis=0, keepdims=True), lf.shape)


def _page_tails(lf_pool):
    n_phys, _, nh = lf_pool.shape
    pp = _largest_divisor(n_phys, 32, 1)
    blk = pl.BlockSpec((pp, PAGE_SIZE, nh), lambda i: (i, 0, 0))
    shape = jax.ShapeDtypeStruct(lf_pool.shape, F32)
    return pl.pallas_call(
        _page_tail_kernel,
        grid=(n_phys // pp,),
        in_specs=[blk],
        out_specs=[blk, blk],
        out_shape=[shape, shape],
        compiler_params=_params("parallel"),
        name="fox_page_tails",
    )(lf_pool)


def _fox_sample_kernel(pt_ref, q_ref, kn_ref, vn_ref, cq_ref, ck_ref, *rest, nh, n_steps, pg):
    kp = rest[:pg]
    vp = rest[pg:2 * pg]
    tl = rest[2 * pg:3 * pg]
    tt = rest[3 * pg:4 * pg]
    o_ref, m_scr, l_scr, acc_scr, suf_scr = rest[4 * pg:]
    pi = pl.program_id(1)
    rows = q_ref.shape[1]
    cols = PAGE_SIZE * nh
    scale = FOX_HEAD ** -0.5

    @pl.when(pi == 0)
    def _():
        m_scr[...] = jnp.full(m_scr.shape, NEG_INF, F32)
        l_scr[...] = jnp.zeros(l_scr.shape, F32)
        acc_scr[...] = jnp.zeros(acc_scr.shape, F32)
        suf_scr[...] = jnp.zeros(suf_scr.shape, F32)

    qb = q_ref[0].astype(BF16)
    cnew = cq_ref[0]

    def online(scores, vals):
        m_old = m_scr[...]
        m_new = m_old
        for s in scores:
            m_new = jnp.maximum(m_new, jnp.max(s, axis=-1, keepdims=True))
        corr = jnp.exp(m_old - m_new)
        den = l_scr[...] * corr
        acc = acc_scr[...] * corr
        for s, val in zip(scores, vals):
            pe = jnp.exp(s - m_new)
            den = den + jnp.sum(pe, axis=-1, keepdims=True)
            acc = acc + _dot(pe.astype(BF16), val)
        l_scr[...] = den
        acc_scr[...] = acc
        m_scr[...] = m_new

    own_head = (_iota((rows, cols), 0) % nh) == (_iota((rows, cols), 1) % nh)
    suf = suf_scr[...]
    bias = [None] * pg
    for gi in reversed(range(pg)):
        bias[gi] = suf + tl[gi][0]
        suf = suf + tt[gi][0]
    suf_scr[...] = suf
    scores = []
    for gi in range(pg):
        kflat = kp[gi][0].reshape(cols, FOX_HEAD).astype(BF16)
        s = _dot_nt(qb, kflat) * scale + bias[gi] + cnew
        scores.append(jnp.where(own_head, s, NEG_INF))
    online(scores, [r[0].reshape(cols, FOX_HEAD).astype(BF16) for r in vp])

    @pl.when(pi == n_steps - 1)
    def _():
        sn = _dot_nt(qb, kn_ref[0].astype(BF16)) * scale + cnew - ck_ref[0]
        ri = _iota((rows, rows), 0)
        cj = _iota((rows, rows), 1)
        ok = jnp.logical_and((ri % nh) == (cj % nh), (cj // nh) <= (ri // nh))
        online([jnp.where(ok, sn, NEG_INF)], [vn_ref[0].astype(BF16)])
        o_ref[0] = (acc_scr[...] / l_scr[...]).astype(o_ref.dtype)


def _fox_sample(q, k_new, v_new, c_col, k_pool, v_pool, lf_pool, page_table):
    bd, t_new, width = q.shape
    nh = c_col.shape[2]
    n_pages = page_table.shape[1]
    n_phys = k_pool.shape[0]
    pg = 4 if n_pages % 4 == 0 else 1
    n_steps = n_pages // pg
    rows = t_new * nh
    cols = PAGE_SIZE * nh
    tail, tot = _page_tails(lf_pool)
    tail = tail.reshape(n_phys, 1, cols)
    tot = tot.reshape(n_phys, 1, cols)
    by_head = lambda a: a.reshape(bd, rows, FOX_HEAD)
    full = lambda shape: pl.BlockSpec((1,) + shape, lambda i, p, pt: (i, 0, 0))

    def page(shape, slot):
        zeros = (0,) * len(shape)
        return pl.BlockSpec((1,) + shape,
                            lambda i, p, pt: (pt[i, n_pages - (p + 1) * pg + slot],) + zeros)

    slots = range(pg)
    kv_page = (PAGE_SIZE, nh, FOX_HEAD)
    grid_spec = pltpu.PrefetchScalarGridSpec(
        num_scalar_prefetch=1,
        grid=(bd, n_steps),
        in_specs=([full((rows, FOX_HEAD)), full((rows, FOX_HEAD)), full((rows, FOX_HEAD)),
                   full((rows, 1)), full((1, rows))]
                  + [page(kv_page, s) for s in slots] + [page(kv_page, s) for s in slots]
                  + [page((1, cols), s) for s in slots] + [page((1, cols), s) for s in slots]),
        out_specs=full((rows, FOX_HEAD)),
        scratch_shapes=[pltpu.VMEM((rows, 1), F32), pltpu.VMEM((rows, 1), F32),
                        pltpu.VMEM((rows, FOX_HEAD), F32), pltpu.VMEM((1, cols), F32)],
    )
    out = pl.pallas_call(
        functools.partial(_fox_sample_kernel, nh=nh, n_steps=n_steps, pg=pg),
        grid_spec=grid_spec,
        out_shape=jax.ShapeDtypeStruct((bd, rows, FOX_HEAD), BF16),
        compiler_params=_params("parallel", "arbitrary"),
        name="fox_sample_attention",
    )(page_table, by_head(q), by_head(k_new), by_head(v_new),
      c_col.reshape(bd, rows, 1), c_col.reshape(bd, 1, rows),
      *([k_pool] * pg), *([v_pool] * pg), *([tail] * pg), *([tot] * pg))
    return out.reshape(bd, t_new, width)


def _conv_kernel(bg_ref, cg_ref, h_ref, w_ref, buf_ref, y_ref, nb_ref, u_scr, *, tt, n_t):
    ti = pl.program_id(2)

    @pl.when(ti == 0)
    def _():
        u_scr[6:8, :] = buf_ref[0]

    u = cg_ref[0] * h_ref[0]
    u_scr[8:8 + tt, :] = u
    y = (w_ref[0:1, :] * u_scr[6:6 + tt, :] + w_ref[1:2, :] * u_scr[7:7 + tt, :]
         + w_ref[2:3, :] * u)
    y_ref[0] = (bg_ref[0] * y).astype(y_ref.dtype)
    tail = u_scr[6 + tt:8 + tt, :]
    u_scr[6:8, :] = tail

    @pl.when(ti == n_t - 1)
    def _():
        nb_ref[0] = tail


def _short_conv(proj3, buf0, conv_w):
    b, l, d3 = proj3.shape
    d = d3 // 3
    tc = _largest_divisor(d, 512, LANES)
    tt = _largest_divisor(l, 704, 8)
    n_t = l // tt
    nc = d // tc
    seg = lambda s: pl.BlockSpec((1, tt, tc), lambda i, j, t: (i, t, s * nc + j))
    return pl.pallas_call(
        functools.partial(_conv_kernel, tt=tt, n_t=n_t),
        grid=(b, nc, n_t),
        in_specs=[seg(0), seg(1), seg(2),
                  pl.BlockSpec((conv_w.shape[0], tc), lambda i, j, t: (0, j)),
                  pl.BlockSpec((1, 2, tc), lambda i, j, t: (i, 0, j))],
        out_specs=[pl.BlockSpec((1, tt, tc), lambda i, j, t: (i, t, j)),
                   pl.BlockSpec((1, 2, tc), lambda i, j, t: (i, 0, j))],
        out_shape=[jax.ShapeDtypeStruct((b, l, d), BF16), jax.ShapeDtypeStruct((b, 2, d), F32)],
        scratch_shapes=[pltpu.VMEM((tt + 8, tc), F32)],
        compiler_params=_params("parallel", "parallel", "arbitrary"),
        name="short_conv",
    )(proj3, proj3, proj3, conv_w, buf0)


def _trunk(hp3, hs3, s0, shift0, conv0, k_pool, v_pool, lf_pool, page_table, wts):
    (norm_mix, norm_mlp, w_in_even, b_forget, rwkv_mu, rwkv_w0, rwkv_w2, rwkv_a0, rwkv_a2, rwkv_g2,
     rwkv_k_k, rwkv_k_a, rwkv_r_k, rwkv_gn_g, rwkv_gn_b, w_out_even, w_in_odd, conv_w, w_out_odd,
     w_up, w_down) = wts
    bp, l, d = hp3.shape
    bs, t, _ = hs3.shape
    hp = hp3.reshape(bp * l, d)
    hs = hs3.reshape(bs * t, d)
    cw = rwkv_w0.shape[-1]
    rproj = rwkv_mu.shape[-1]
    fw = d - cw
    nh = fw // FOX_HEAD
    heads = cw // RWKV_HEAD
    rw = (rwkv_mu[0], rwkv_w0[0], rwkv_w2[0], rwkv_a0[0], rwkv_a2[0], rwkv_g2[0], rwkv_k_k[0],
          rwkv_k_a[0], rwkv_r_k[0], rwkv_gn_g[0], rwkv_gn_b[0])
    norm = lambda a, g: _rmsnorm(a, g, BF16)
    wide = functools.partial(_matmul, tn=512, tk=4096)
    narrow = functools.partial(_matmul, tn=256, tk=4096)
    deep = functools.partial(_matmul, tn=256, tk=4096)

    lpad = RWKV_CHUNK * pl.cdiv(l, RWKV_CHUNK)
    if lpad > l:
        hnp, hnp_pad = _rmsnorm_pad(hp3, norm_mix[0], lpad)
        hnp = hnp.reshape(bp * l, d)
        hnp_pad = hnp_pad.reshape(bp * lpad, d)
    else:
        hnp = hnp_pad = norm(hp, norm_mix[0])
    hns = norm(hs, norm_mix[0])
    pq_p, pq_s = narrow(hnp_pad, hns, w_in_even, 0, n_cols=rproj + fw)
    k_p, k_s = narrow(hnp, hns, w_in_even, 0, n_lo=rproj + fw, n_cols=fw)
    v_p, v_s = narrow(hnp, hns, w_in_even, 0, n_lo=rproj + 2 * fw, n_cols=fw)
    w_lf = w_in_even[:, :, rproj + 3 * fw:]
    z_p, _ = _matmul(hnp, None, w_lf, 0, tn=nh, tk=4096)
    z_s, _ = _matmul(hns, None, w_lf, 0, tn=nh, tk=4096)
    pq_p = pq_p.reshape(bp, lpad, rproj + fw)
    pq_s = pq_s.reshape(bs, t, rproj + fw)
    k_p, v_p = k_p.reshape(bp, l, fw), v_p.reshape(bp, l, fw)
    k_s, v_s = k_s.reshape(bs, t, fw), v_s.reshape(bs, t, fw)
    lf_p, cc_p, cr_p = _logf(z_p.reshape(bp, l, nh), b_forget[0])
    lf_s, cc_s, _ = _logf(z_s.reshape(bs, t, nh), b_forget[0])

    zero_s = jnp.zeros((bp, heads, RWKV_HEAD, RWKV_HEAD), F32)
    or_p, st_p = _rwkv(pq_p, jnp.zeros((bp, rproj), F32), zero_s, *rw, c=RWKV_CHUNK, l_valid=l,
                       l_out=l, o_width=cw)
    of_p = _fox_prompt(pq_p, rproj // LANES, k_p, v_p, cc_p, cr_p)
    tpad = 8 * pl.cdiv(t, 8)
    pq_s_pad = jnp.pad(pq_s[:, :, :rproj], ((0, 0), (0, tpad - t), (0, 0)))
    or_s, st_s = _rwkv(pq_s_pad, shift0[0], s0[0], *rw, c=tpad, l_valid=t, l_out=tpad, o_width=cw)
    of_s = _fox_sample(pq_s[:, :, rproj:], k_s, v_s, cc_s, k_pool[0], v_pool[0], lf_pool[0],
                       page_table)
    mix_p = (or_p.reshape(bp * l, cw), of_p.reshape(bp * l, fw))
    mix_s = (or_s[:, :t].reshape(bs * t, cw), of_s.reshape(bs * t, fw))
    hp, hs = narrow(mix_p, mix_s, w_out_even, 0, epilogue="resid", resid=(hp, hs))
    up, us = wide(norm(hp, norm_mlp[0]), norm(hs, norm_mlp[0]), w_up, 0, epilogue="relu2",
                  out_dtype=BF16)
    hp, hs = deep(up, us, w_down, 0, epilogue="resid", resid=(hp, hs))

    p1_p, p1_s = wide(norm(hp, norm_mix[1]), norm(hs, norm_mix[1]), w_in_odd, 0)
    y_p, buf_p = _short_conv(p1_p.reshape(bp, l, 3 * d), jnp.zeros((bp,) + conv0.shape[2:], F32),
                             conv_w[0])
    y_s, buf_s = _short_conv(p1_s.reshape(bs, t, 3 * d), conv0[0], conv_w[0])
    hp, hs = narrow(y_p.reshape(bp * l, d), y_s.reshape(bs * t, d), w_out_odd, 0, epilogue="resid",
                    resid=(hp, hs))
    up, us = wide(norm(hp, norm_mlp[1]), norm(hs, norm_mlp[1]), w_up, 1, epilogue="relu2",
                  out_dtype=BF16)
    hp, hs = deep(up, us, w_down, 1, epilogue="resid", resid=(hp, hs))

    head4 = lambda a, b, n: a.reshape(b, n, nh, FOX_HEAD)[None]
    outs_p = (st_p[None], pq_p[:, l - 1, :rproj][None], head4(k_p, bp, l), head4(v_p, bp, l),
              lf_p[None], buf_p[None])
    outs_s = (st_s[None], pq_s[:, t - 1, :rproj][None], head4(k_s, bs, t), head4(v_s, bs, t),
              lf_s[None], buf_s[None])
    return hp.reshape(bp, l, d), hs, outs_p, outs_s


def kernel(x_prompt, x_sample, state_rwkv, state_rwkv_shift, cache_fox_k, cache_fox_v, cache_fox_logf, state_conv, page_table, meta_tokens, norm_mix, norm_mlp, norm_final, w_in_even, b_forget, rwkv_mu, rwkv_w0, rwkv_w2, rwkv_a0, rwkv_a2, rwkv_g2, rwkv_k_k, rwkv_k_a, rwkv_r_k, rwkv_gn_g, rwkv_gn_b, w_out_even, w_in_odd, conv_w, w_out_odd, w_up, w_down):
    wts = (norm_mix, norm_mlp, w_in_even, b_forget, rwkv_mu, rwkv_w0, rwkv_w2, rwkv_a0, rwkv_a2,
           rwkv_g2, rwkv_k_k, rwkv_k_a, rwkv_r_k, rwkv_gn_g, rwkv_gn_b, w_out_even, w_in_odd, conv_w,
           w_out_odd, w_up, w_down)
    bp, _, d = x_prompt.shape
    bs, ts, _ = x_sample.shape
    h0 = jnp.concatenate([jnp.broadcast_to(meta_tokens[None], (bp, N_META, d)), x_prompt], axis=1)
    hp, hs, outs_p, outs_s = _trunk(h0, x_sample, state_rwkv, state_rwkv_shift, state_conv,
                                    cache_fox_k, cache_fox_v, cache_fox_logf, page_table, wts)
    y_prompt = _final_norm_prompt(hp, norm_final)
    y_sample = _rmsnorm(hs, norm_final, F32).reshape(bs, ts, d)
    return (y_prompt, y_sample, *outs_p, *outs_s)
```

```python
import functools
import math

import jax
import jax.numpy as jnp
from jax import lax
from jax.experimental import pallas as pl
from jax.experimental.pallas import tpu as pltpu

F32 = jnp.float32
BF16 = jnp.bfloat16
HI = lax.Precision.HIGHEST

RMS_EPS = 1e-6
GN_EPS = 64e-5
NEG_INF = -1e30
N_META = 16
RWKV_HEAD = 64
FOX_HEAD = 128
PAGE_SIZE = 128
LANES = 128
VMEM_LIMIT = 62 * 1024 * 1024
BF16_ROWS = 16
RWKV_CHUNK = 48


def _largest_divisor(n, cap, mult):
    best = None
    for d in range(mult, min(n, cap) + 1, mult):
        if n % d == 0:
            best = d
    return n if best is None else best


def _params(*sem, flags=None):
    return pltpu.CompilerParams(dimension_semantics=sem, vmem_limit_bytes=VMEM_LIMIT, flags=flags)


def _dot(a, b, precision=None):
    return jnp.dot(a, b, precision=precision, preferred_element_type=F32)


def _dot_nt(a, b, precision=None):
    return lax.dot_general(a, b, (((1,), (1,)), ((), ())), precision=precision,
                           preferred_element_type=F32)


def _dot_tn(a, b, precision=None):
    return lax.dot_general(a, b, (((0,), (0,)), ((), ())), precision=precision,
                           preferred_element_type=F32)


def _iota(shape, dim):
    return lax.broadcasted_iota(jnp.int32, shape, dim)


def _split(x, pieces=2):
    out = []
    for i in range(pieces):
        p = x.astype(BF16)
        out.append(p)
        if i + 1 < pieces:
            x = x - p.astype(F32)
    return out


def _dot3(a, b, f=_dot):
    axis = 1 if f is _dot_tn else 0
    m = a[0].shape[axis]
    if m % BF16_ROWS:
        return f(a[0], b[0]) + (f(a[0], b[1]) + f(a[1], b[0]))
    t = f(jnp.concatenate([a[0], a[1]], axis=axis), b[0])
    return (t[:m] + t[m:]) + f(a[0], b[1])


def _dot_exact(pieces, other, f=_dot):
    m = pieces[0].shape[0]
    if f is _dot and m % BF16_ROWS == 0:
        t = f(jnp.concatenate(pieces, axis=0), other)
        return sum(t[i * m:(i + 1) * m] for i in range(1, len(pieces))) + t[:m]
    acc = f(pieces[0], other)
    for p in pieces[1:]:
        acc = acc + f(p, other)
    return acc


def _rmsnorm_kernel(x_ref, g_ref, o_ref):
    x = x_ref[...]
    ms = jnp.mean(x * x, axis=-1, keepdims=True)
    o_ref[...] = (x * lax.rsqrt(ms + RMS_EPS) * g_ref[...]).astype(o_ref.dtype)


def _rmsnorm(x2d, g, out_dtype):
    m, d = x2d.shape
    tr = _largest_divisor(m, 512, 16)
    return pl.pallas_call(
        _rmsnorm_kernel,
        grid=(m // tr,),
        in_specs=[pl.BlockSpec((tr, d), lambda i: (i, 0)),
                  pl.BlockSpec((1, d), lambda i: (0, 0))],
        out_specs=pl.BlockSpec((tr, d), lambda i: (i, 0)),
        out_shape=jax.ShapeDtypeStruct((m, d), out_dtype),
        compiler_params=_params("parallel"),
        name="rmsnorm",
    )(x2d, g.reshape(1, d))


def _rmsnorm_pad_kernel(x_ref, g_ref, o_ref, op_ref, *, n_real):
    x = x_ref[0]
    ms = jnp.mean(x * x, axis=-1, keepdims=True)
    y = (x * lax.rsqrt(ms + RMS_EPS) * g_ref[...]).astype(o_ref.dtype)
    o_ref[0] = y
    op_ref[0] = jnp.where(pl.program_id(1) < n_real, y, jnp.zeros_like(y))


def _rmsnorm_pad(h3, g, lp):
    b, l, d = h3.shape
    tr = _largest_divisor(math.gcd(l, lp), 512, BF16_ROWS)
    n_real = l // tr
    real = pl.BlockSpec((1, tr, d), lambda i, j: (i, jnp.minimum(j, n_real - 1), 0))
    return pl.pallas_call(
        functools.partial(_rmsnorm_pad_kernel, n_real=n_real),
        grid=(b, lp // tr),
        in_specs=[real, pl.BlockSpec((1, d), lambda i, j: (0, 0))],
        out_specs=[real, pl.BlockSpec((1, tr, d), lambda i, j: (i, j, 0))],
        out_shape=[jax.ShapeDtypeStruct((b, l, d), BF16), jax.ShapeDtypeStruct((b, lp, d), BF16)],
        compiler_params=_params("parallel", "arbitrary"),
        name="rmsnorm_pad",
    )(h3, g.reshape(1, d))


def _final_norm_prompt_kernel(a_ref, b_ref, g_ref, o_ref):
    x = jnp.concatenate([a_ref[0, N_META:, :], b_ref[0]], axis=0)
    ms = jnp.mean(x * x, axis=-1, keepdims=True)
    o_ref[0] = x * lax.rsqrt(ms + RMS_EPS) * g_ref[...]


def _final_norm_prompt(h3, g):
    b, l, d = h3.shape
    s = l - N_META
    tr = _largest_divisor(s, 256, N_META)
    sub = tr // N_META
    return pl.pallas_call(
        _final_norm_prompt_kernel,
        grid=(b, s // tr),
        in_specs=[pl.BlockSpec((1, tr, d), lambda i, j: (i, j, 0)),
                  pl.BlockSpec((1, N_META, d), lambda i, j: (i, (j + 1) * sub, 0)),
                  pl.BlockSpec((1, d), lambda i, j: (0, 0))],
        out_specs=pl.BlockSpec((1, tr, d), lambda i, j: (i, j, 0)),
        out_shape=jax.ShapeDtypeStruct((b, s, d), F32),
        compiler_params=_params("parallel", "parallel"),
        name="final_norm_prompt",
    )(h3, h3, g.reshape(1, d))


def _mm_kernel(*refs, nk, nx, epilogue, ns):
    refs = list(refs)
    take = lambda cnt: [refs.pop(0) for _ in range(cnt)]
    xp, xs, (w_ref,) = take(nx), take(nx * ns), take(1)
    rp, rs = (take(1 + ns) + [None])[:2] if epilogue == "resid" else (None, None)
    op, osm = (take(1 + ns) + [None])[:2]
    accp, accs = (take(1 + ns) + [None])[:2] if refs else (None, None)
    i = pl.program_id(0)
    k = pl.program_id(2)
    wb = w_ref[...].astype(BF16)

    def run(x_refs, r_ref, o_ref, acc_ref):
        def finish(acc):
            if epilogue == "relu2":
                acc = jnp.square(jnp.maximum(acc, 0.0))
            elif epilogue == "resid":
                acc = r_ref[...] + acc
            o_ref[...] = acc.astype(o_ref.dtype)

        if nk == 1:
            finish(_dot(x_refs[0][...], wb))
        elif epilogue == "resid" and nx == 2:
            @pl.when(k == 0)
            def _():
                o_ref[...] = r_ref[...] + _dot(x_refs[0][...], wb)

            @pl.when(k == 1)
            def _():
                o_ref[...] += _dot(x_refs[1][...], wb)
        elif epilogue == "resid":
            part = _dot(x_refs[0][...], wb)

            @pl.when(k == 0)
            def _():
                o_ref[...] = r_ref[...] + part

            @pl.when(k > 0)
            def _():
                o_ref[...] += part
        else:
            assert nx == 1
            part = _dot(x_refs[0][...], wb)

            @pl.when(k == 0)
            def _():
                acc_ref[...] = part

            @pl.when(jnp.logical_and(k > 0, k < nk - 1))
            def _():
                acc_ref[...] += part

            @pl.when(k == nk - 1)
            def _():
                finish(acc_ref[...] + part)

    run(xp, rp, op, accp)
    if ns:
        @pl.when(i == 0)
        def _():
            run(xs, rs, osm, accs)


def _matmul(xp, xs, w, layer, *, n_lo=0, n_cols=None, tn, tk, epilogue="plain", resid=None,
            out_dtype=F32):
    xp = xp if isinstance(xp, tuple) else (xp,)
    ns = 0 if xs is None else 1
    xs = () if xs is None else (xs if isinstance(xs, tuple) else (xs,))
    nx = len(xp)
    m, kdim = xp[0].shape[0], nx * xp[0].shape[1]
    ms = xs[0].shape[0] if ns else 0
    n_cols = w.shape[2] - n_lo if n_cols is None else n_cols
    tn = _largest_divisor(math.gcd(n_cols, n_lo), tn, LANES)
    tk = kdim // 2 if nx == 2 else _largest_divisor(kdim, tk, LANES)
    assert n_lo % tn == 0 and n_cols % tn == 0
    tm = _largest_divisor(m, 2112, BF16_ROWS)
    nk = kdim // tk
    nj = n_cols // tn
    off = n_lo // tn
    kblk = (lambda k: k) if nx == 1 else (lambda k: 0)
    once = pl.Buffered(1) if nk == 1 or nx == 2 else None
    in_specs = ([pl.BlockSpec((tm, tk), lambda i, j, k: (i, kblk(k)), pipeline_mode=once)] * nx
                + [pl.BlockSpec((ms, tk), lambda i, j, k: (0, kblk(k)))] * (nx * ns)
                + [pl.BlockSpec((None, tk, tn), lambda i, j, k: (layer, k, j + off))])
    args = [*xp, *xs, w]
    out_specs = [pl.BlockSpec((tm, tn), lambda i, j, k: (i, j))]
    out_shape = [jax.ShapeDtypeStruct((m, n_cols), out_dtype)]
    use_acc = nk > 1 and epilogue != "resid"
    acc = [pltpu.VMEM((tm, tn), F32)] if use_acc else []
    if epilogue == "resid":
        in_specs.append(pl.BlockSpec((tm, tn), lambda i, j, k: (i, j)))
        args.append(resid[0])
    if ns:
        if epilogue == "resid":
            in_specs.append(pl.BlockSpec((ms, tn), lambda i, j, k: (0, jnp.where(i == 0, j, 0))))
            args.append(resid[1])
        out_specs.append(pl.BlockSpec((ms, tn), lambda i, j, k: (0, jnp.where(i == 0, j, nj))))
        out_shape.append(jax.ShapeDtypeStruct((ms, n_cols + tn), out_dtype))
        acc += [pltpu.VMEM((ms, tn), F32)] if use_acc else []
    outs = pl.pallas_call(
        functools.partial(_mm_kernel, nk=nk, nx=nx, epilogue=epilogue, ns=ns),
        grid=(m // tm, nj, nk),
        in_specs=in_specs,
        out_specs=out_specs,
        out_shape=out_shape,
        scratch_shapes=acc,
        compiler_params=_params("arbitrary", "arbitrary", "arbitrary"),
        name="matmul_" + epilogue,
    )(*args)
    return (outs[0], outs[1][:, :n_cols]) if ns else (outs[0], None)


def _rwkv_kernel(pr_ref, pk_ref, pv_ref, px_ref, sr_ref, sk_ref, sv_ref, sx_ref,
                 mur_ref, muk_ref, muv_ref, mux_ref, w0_ref, a0_ref, kk_ref, ka_ref, rk_ref,
                 gg_ref, gb_ref, w2_ref, a2_ref, g2_ref, s0_ref,
                 o_ref, st_ref,
                 s_scr, br_scr, bk_scr, bv_scr, bx_scr, *, c, g, nb, l_valid, n_chunks):
    ci = pl.program_id(2)
    half = RWKV_HEAD
    w = g * LANES
    gs = 1
    nh = 2 * gs
    n = nh * c

    @pl.when(ci == 0)
    def _():
        s_scr[...] = s0_ref[...]
        br_scr[:, 7:8, :] = sr_ref[...]
        bk_scr[:, 7:8, :] = sk_ref[...]
        bv_scr[:, 7:8, :] = sv_ref[...]
        bx_scr[:, 7:8, :] = sx_ref[...]

    ones_w = ((_iota((w, w), 0) // half) == (_iota((w, w), 1) // half)).astype(BF16)
    ones_p = ((_iota((LANES, LANES), 0) // half) == (_iota((LANES, LANES), 1) // half)).astype(F32)
    tri = (_iota((c, c), 1) <= _iota((c, c), 0)).astype(BF16)
    lane_c = _iota((1, LANES), 1)
    m2 = [(lane_c < half).astype(F32), (lane_c >= half).astype(F32)]
    ri = _iota((n, n), 0)
    cj = _iota((n, n), 1)
    in_blk = lambda i, b: jnp.logical_and(i >= b * c, i < (b + 1) * c)
    same = jnp.logical_and(in_blk(ri, 0), in_blk(cj, 0))
    for b in range(1, nh):
        same = jnp.logical_or(same, jnp.logical_and(in_blk(ri, b), in_blk(cj, b)))
    strict = jnp.logical_and(same, cj < ri)
    incl = jnp.logical_and(same, cj <= ri)
    eye = (ri == cj).astype(F32)
    n_doub = max(1, math.ceil(math.log2(c)))
    w2b = w2_ref[...].astype(BF16)
    a2b = a2_ref[...].astype(BF16)
    g2b = g2_ref[...].astype(BF16)
    pair = lambda t, p: t[:, p * LANES:(p + 1) * LANES]

    def seg_sum(t):
        return _dot_exact(_split(t, 3), ones_w)

    def sequence(bi):
        def shifted(p_ref, buf, mu_ref):
            p = p_ref[bi]
            buf[bi, 8:8 + c, :] = p
            prev = buf[bi, 7:7 + c, :]
            buf[bi, 7:8, :] = p[c - 1:c, :]
            return p + (prev - p) * mu_ref[...]

        r = shifted(pr_ref, br_scr, mur_ref)
        k = shifted(pk_ref, bk_scr, muk_ref)
        v = shifted(pv_ref, bv_scr, muv_ref)
        x = shifted(px_ref, bx_scr, mux_ref)
        yield

        wl = w0_ref[...] + _dot(jnp.tanh(x).astype(BF16), w2b)
        wl = -jax.nn.softplus(-wl) - 0.5
        logw = -jnp.exp(wl)
        a = jax.nn.sigmoid(a0_ref[...] + _dot(x.astype(BF16), a2b))
        gate = _dot(jax.nn.sigmoid(x).astype(BF16), g2b)
        kk = k * kk_ref[...]
        yield
        kk = kk / jnp.maximum(jnp.sqrt(seg_sum(kk * kk)), 1e-12)
        k = k * (1.0 + (a - 1.0) * ka_ref[...])
        be = kk * a
        if l_valid < c * n_chunks:
            ok = (_iota((c, w), 0) + ci * c) < l_valid
            logw = jnp.where(ok, logw, 0.0)
            kk = jnp.where(ok, kk, 0.0)
            be = jnp.where(ok, be, 0.0)
            k = jnp.where(ok, k, 0.0)
            v = jnp.where(ok, v, 0.0)

        yield
        cum = _dot_exact(_split(logw, 3), tri, lambda p, t: _dot(t, p))
        yield
        tot = cum[c - 1:c, :]
        ah = -kk * jnp.exp(cum - logw)
        rh = r * jnp.exp(cum)
        ieg = jnp.exp(-cum)
        bc = be * ieg
        kc = k * ieg
        etail = jnp.exp(tot - cum)
        bt = be * etail
        kt = k * etail

        o_parts = {}

        def system(grp):
            stack = lambda t: jnp.concatenate(
                [pair(t, p) * m2[q] for p in grp for q in range(2)], axis=0)
            ars = _split(jnp.concatenate([stack(ah), stack(rh)], axis=0))
            vs = stack(v)
            yield
            gb = _dot3(ars, _split(stack(bc)), _dot_nt)
            yield
            gk = _dot3(ars, _split(stack(kc)), _dot_nt)
            xm = jnp.where(strict, gb[:n], 0.0)
            tm = eye + xm
            xp = _split(xm)
            yield
            if n_doub > 1:
                xp = _split(_dot3(xp, xp))
                yield
                for _ in range(n_doub - 2):
                    ts = _split(tm)
                    both = _dot3([jnp.concatenate([xp[i], ts[i]], axis=0) for i in range(2)], xp)
                    tm = tm + both[n:]
                    xp = _split(both[:n])
                    yield
                tm = tm + _dot3(_split(tm), xp)
                yield
            lm = jnp.concatenate([jnp.where(strict, gk[:n], 0.0), jnp.where(incl, gk[n:], 0.0)], axis=0)
            lmv = _dot3(_split(lm), _split(vs))
            mrb = jnp.where(incl, gb[n:], 0.0)
            yield

            y0 = [_dot3(_split(jnp.concatenate([pair(ah, p), pair(rh, p)], axis=0)),
                        _split(s_scr[bi, p]), _dot_nt) for p in grp]
            yield
            ys = jnp.concatenate([y[:c] * m2[q] for y in y0 for q in range(2)], axis=0) + lmv[:n]
            ps = _dot3(_split(tm), _split(ys))
            yield
            os_ = _dot3(_split(mrb), _split(ps)) + lmv[n:]
            yield
            for i, p in enumerate(grp):
                lo, mid, hi = 2 * i * c, (2 * i + 1) * c, (2 * i + 2) * c
                o_parts[p] = y0[i][c:] + os_[lo:mid] + os_[mid:hi]
                pv = jnp.concatenate([ps[lo:mid] + ps[mid:hi], pair(v, p)], axis=0)
                bk = jnp.concatenate([pair(bt, p), pair(kt, p)], axis=0)
                upd = _dot3(_split(pv), _split(bk), _dot_tn)
                s_scr[bi, p] = (s_scr[bi, p] * jnp.exp(pair(tot, p)) + upd) * ones_p
                yield

        systems = [system(list(range(s, s + gs))) for s in range(0, g, gs)]
        while systems:
            systems = [s for s in systems if next(s, True) is None]
            yield

        o = jnp.concatenate([o_parts[p] for p in range(g)], axis=1) if g > 1 else o_parts[0]
        inv_n = 1.0 / half
        mean = seg_sum(o) * inv_n
        yield
        d = o - mean
        var = seg_sum(d * d) * inv_n
        yield
        on = d * lax.rsqrt(var + GN_EPS) * gg_ref[...] + gb_ref[...]
        bonus = seg_sum(r * k * rk_ref[...]) * v
        o_ref[bi] = ((on + bonus) * gate).astype(o_ref.dtype)

    live = [sequence(bi) for bi in range(nb)]
    while live:
        live = [s for s in live if next(s, True) is None]

    @pl.when(ci == n_chunks - 1)
    def _():
        st_ref[...] = s_scr[...]


def _rwkv(proj3, shift0, s0, mu, w0, w2, a0, a2, g2, k_k, k_a, r_k, gn_g, gn_b, *, c, l_valid,
          l_out, o_width):
    b, lp, _ = proj3.shape
    cw = w0.shape[-1]
    npair = cw // LANES
    heads = cw // RWKV_HEAD
    g = 2 if npair % 2 == 0 else 1
    w = g * LANES
    ng = npair // g
    lw = mu.shape[-1] - 3 * cw
    assert lw % LANES == 0 and (3 * cw) % lw == 0 and lp % c == 0
    n_chunks = lp // c
    wl_, al_ = w2.shape[0], a2.shape[0]
    w2p = jnp.zeros((lw, cw), F32).at[:wl_].set(w2)
    a2p = jnp.zeros((lw, cw), F32).at[wl_:wl_ + al_].set(a2)
    g2p = jnp.zeros((lw, cw), F32).at[wl_ + al_:].set(g2)
    s0p = s0.reshape(b, npair, 2, RWKV_HEAD, RWKV_HEAD)
    z = jnp.zeros_like(s0p[:, :, 0])
    s0bd = jnp.concatenate([jnp.concatenate([s0p[:, :, 0], z], axis=-1),
                            jnp.concatenate([z, s0p[:, :, 1]], axis=-1)], axis=-2)
    sh3 = shift0.reshape(b, 1, -1)
    mu2 = mu.reshape(1, -1)
    row = lambda t: t.reshape(1, cw)
    xoff = (3 * cw) // lw

    nb = _largest_divisor(b, 4, 1)
    seg = lambda s: pl.BlockSpec((nb, c, w), lambda i, j, t: (i, t, s * ng + j))
    sseg = lambda s: pl.BlockSpec((nb, 1, w), lambda i, j, t: (i, 0, s * ng + j))
    mseg = lambda s: pl.BlockSpec((1, w), lambda i, j, t: (0, s * ng + j))
    vec = pl.BlockSpec((1, w), lambda i, j, t: (0, j))
    lora = pl.BlockSpec((lw, w), lambda i, j, t: (0, j))
    state = pl.BlockSpec((nb, g, LANES, LANES), lambda i, j, t: (i, j, 0, 0))
    in_specs = [seg(0), seg(1), seg(2), pl.BlockSpec((nb, c, lw), lambda i, j, t: (i, t, xoff)),
                sseg(0), sseg(1), sseg(2), pl.BlockSpec((nb, 1, lw), lambda i, j, t: (i, 0, xoff)),
                mseg(0), mseg(1), mseg(2), pl.BlockSpec((1, lw), lambda i, j, t: (0, xoff)),
                vec, vec, vec, vec, vec, vec, vec, lora, lora, lora, state]
    o, st = pl.pallas_call(
        functools.partial(_rwkv_kernel, c=c, g=g, nb=nb, l_valid=l_valid, n_chunks=n_chunks),
        grid=(b // nb, ng, n_chunks),
        in_specs=in_specs,
        out_specs=[pl.BlockSpec((nb, c, w), lambda i, j, t: (i, t, j)), state],
        out_shape=[jax.ShapeDtypeStruct((b, l_out, o_width), BF16),
                   jax.ShapeDtypeStruct((b, npair, LANES, LANES), F32)],
        scratch_shapes=[pltpu.VMEM((nb, g, LANES, LANES), F32),
                        pltpu.VMEM((nb, c + 8, w), F32), pltpu.VMEM((nb, c + 8, w), F32),
                        pltpu.VMEM((nb, c + 8, w), F32), pltpu.VMEM((nb, c + 8, lw), F32)],
        compiler_params=_params("parallel", "parallel", "arbitrary"),
        name="rwkv7_chunk",
    )(proj3, proj3, proj3, proj3, sh3, sh3, sh3, sh3, mu2, mu2, mu2, mu2,
      row(w0), row(a0), row(k_k), row(k_a), row(r_k), row(gn_g), row(gn_b), w2p, a2p, g2p, s0bd)
    st = jnp.stack([st[:, :, :RWKV_HEAD, :RWKV_HEAD], st[:, :, RWKV_HEAD:, RWKV_HEAD:]], axis=2)
    return o, st.reshape(b, heads, RWKV_HEAD, RWKV_HEAD)


def _logf_kernel(z_ref, bf_ref, lf_ref, cc_ref, cr_ref, *, rb):
    l, h = z_ref.shape[1], z_ref.shape[2]
    ti = _iota((rb, rb), 0)
    si = _iota((rb, rb), 1)
    tri = (si <= ti).astype(F32)
    carry = jnp.zeros((1, h), F32)
    for i in range(l // rb):
        sl = slice(i * rb, (i + 1) * rb)
        lf = jax.nn.log_sigmoid(z_ref[0, sl, :] + bf_ref[...])
        lf_ref[0, sl, :] = lf
        cblk = carry + _dot(tri, lf, HI)
        cc_ref[0, sl, :] = cblk
        carry = cblk[rb - 1:rb, :]
    eye = (_iota((h, h), 0) == _iota((h, h), 1)).astype(F32)
    cr_ref[0] = _dot_nt(eye, cc_ref[0], HI)


def _logf(z3, b_forget):
    b, l, h = z3.shape
    rb = _largest_divisor(l, 512, 8)
    blk = pl.BlockSpec((1, l, h), lambda i: (i, 0, 0))
    return pl.pallas_call(
        functools.partial(_logf_kernel, rb=rb),
        grid=(b,),
        in_specs=[blk, pl.BlockSpec((1, h), lambda i: (0, 0))],
        out_specs=[blk, blk, pl.BlockSpec((1, h, l), lambda i: (i, 0, 0))],
        out_shape=[jax.ShapeDtypeStruct((b, l, h), F32), jax.ShapeDtypeStruct((b, l, h), F32),
                   jax.ShapeDtypeStruct((b, h, l), F32)],
        compiler_params=_params("parallel"),
        name="fox_logf_cumsum",
    )(z3, b_forget.reshape(1, h))


def _fox_prompt_kernel(q_ref, k_ref, v_ref, cc_ref, cr_ref, o_ref, *, tq):
    l = q_ref.shape[1]
    nh = cc_ref.shape[2]
    h = pl.program_id(1)
    scale = FOX_HEAD ** -0.5
    sel = (_iota((nh, LANES), 0) == h).astype(F32)
    cq_all = _dot(cc_ref[0], sel, HI)
    ck = cr_ref[0, pl.ds(h, 1), :]
    kb = k_ref[0].astype(BF16)
    vb = v_ref[0].astype(BF16)
    for i in range(l // tq):
        nk = (i + 1) * tq
        rows = slice(i * tq, nk)
        s = _dot_nt(q_ref[0, rows, :].astype(BF16), kb[:nk]) * scale
        s = s + cq_all[rows, 0:1] - ck[:, :nk]
        qpos = _iota((tq, nk), 0) + i * tq
        kpos = _iota((tq, nk), 1)
        s = jnp.where(qpos >= kpos, s, NEG_INF)
        m = jnp.max(s, axis=-1, keepdims=True)
        p = jnp.exp(s - m)
        den = jnp.sum(p, axis=-1, keepdims=True)
        o = _dot(p.astype(BF16), vb[:nk]) / den
        o_ref[0, rows, :] = o.astype(o_ref.dtype)


def _fox_prompt(proj3, q_off, k3, v3, c_col, c_row):
    b, l, fw = k3.shape
    nh = c_col.shape[2]
    tq = _largest_divisor(l, 384, 8)
    blk = lambda off: pl.BlockSpec((1, l, LANES), lambda i, j: (i, 0, off + j))
    return pl.pallas_call(
        functools.partial(_fox_prompt_kernel, tq=tq),
        grid=(b, nh),
        in_specs=[blk(q_off), blk(0), blk(0),
                  pl.BlockSpec((1, l, nh), lambda i, j: (i, 0, 0)),
                  pl.BlockSpec((1, nh, l), lambda i, j: (i, 0, 0))],
        out_specs=blk(0),
        out_shape=jax.ShapeDtypeStruct((b, l, fw), BF16),
        compiler_params=_params("parallel", "parallel"),
        name="fox_prompt_attention",
    )(proj3, k3, v3, c_col, c_row)


def _page_tail_kernel(lf_ref, tail_ref, tot_ref):
    pp = lf_ref.shape[0]
    later = (_iota((PAGE_SIZE, PAGE_SIZE), 1) > _iota((PAGE_SIZE, PAGE_SIZE), 0)).astype(BF16)
    for i in range(pp):
        lf = lf_ref[i]
        tail_ref[i] = _dot_exact(_split(lf, 3), later, lambda p, u: _dot(u, p))
        tot_ref[i] = jnp.broadcast_to(jnp.sum(lf, axis=0, keepdims=True), lf.shape)


def _page_tails(lf_pool):
    n_phys, _, nh = lf_pool.shape
    pp = _largest_divisor(n_phys, 32, 1)
    blk = pl.BlockSpec((pp, PAGE_SIZE, nh), lambda i: (i, 0, 0))
    shape = jax.ShapeDtypeStruct(lf_pool.shape, F32)
    return pl.pallas_call(
        _page_tail_kernel,
        grid=(n_phys // pp,),
        in_specs=[blk],
        out_specs=[blk, blk],
        out_shape=[shape, shape],
        compiler_params=_params("parallel"),
        name="fox_page_tails",
    )(lf_pool)


def _fox_sample_kernel(pt_ref, q_ref, kn_ref, vn_ref, cq_ref, ck_ref, *rest, nh, n_steps, pg):
    kp = rest[:pg]
    vp = rest[pg:2 * pg]
    tl = rest[2 * pg:3 * pg]
    tt = rest[3 * pg:4 * pg]
    o_ref, m_scr, l_scr, acc_scr, suf_scr = rest[4 * pg:]
    pi = pl.program_id(1)
    rows = q_ref.shape[1]
    cols = PAGE_SIZE * nh
    scale = FOX_HEAD ** -0.5

    @pl.when(pi == 0)
    def _():
        m_scr[...] = jnp.full(m_scr.shape, NEG_INF, F32)
        l_scr[...] = jnp.zeros(l_scr.shape, F32)
        acc_scr[...] = jnp.zeros(acc_scr.shape, F32)
        suf_scr[...] = jnp.zeros(suf_scr.shape, F32)

    qb = q_ref[0].astype(BF16)
    cnew = cq_ref[0]

    def online(scores, vals):
        m_old = m_scr[...]
        m_new = m_old
        for s in scores:
            m_new = jnp.maximum(m_new, jnp.max(s, axis=-1, keepdims=True))
        corr = jnp.exp(m_old - m_new)
        den = l_scr[...] * corr
        acc = acc_scr[...] * corr
        for s, val in zip(scores, vals):
            pe = jnp.exp(s - m_new)
            den = den + jnp.sum(pe, axis=-1, keepdims=True)
            acc = acc + _dot(pe.astype(BF16), val)
        l_scr[...] = den
        acc_scr[...] = acc
        m_scr[...] = m_new

    own_head = (_iota((rows, cols), 0) % nh) == (_iota((rows, cols), 1) % nh)
    suf = suf_scr[...]
    bias = [None] * pg
    for gi in reversed(range(pg)):
        bias[gi] = suf + tl[gi][0]
        suf = suf + tt[gi][0]
    suf_scr[...] = suf
    scores = []
    for gi in range(pg):
        kflat = kp[gi][0].reshape(cols, FOX_HEAD).astype(BF16)
        s = _dot_nt(qb, kflat) * scale + bias[gi] + cnew
        scores.append(jnp.where(own_head, s, NEG_INF))
    online(scores, [r[0].reshape(cols, FOX_HEAD).astype(BF16) for r in vp])

    @pl.when(pi == n_steps - 1)
    def _():
        sn = _dot_nt(qb, kn_ref[0].astype(BF16)) * scale + cnew - ck_ref[0]
        ri = _iota((rows, rows), 0)
        cj = _iota((rows, rows), 1)
        ok = jnp.logical_and((ri % nh) == (cj % nh), (cj // nh) <= (ri // nh))
        online([jnp.where(ok, sn, NEG_INF)], [vn_ref[0].astype(BF16)])
        o_ref[0] = (acc_scr[...] / l_scr[...]).astype(o_ref.dtype)


def _fox_sample(q, k_new, v_new, c_col, k_pool, v_pool, lf_pool, page_table):
    bd, t_new, width = q.shape
    nh = c_col.shape[2]
    n_pages = page_table.shape[1]
    n_phys = k_pool.shape[0]
    pg = 4 if n_pages % 4 == 0 else 1
    n_steps = n_pages // pg
    rows = t_new * nh
    cols = PAGE_SIZE * nh
    tail, tot = _page_tails(lf_pool)
    tail = tail.reshape(n_phys, 1, cols)
    tot = tot.reshape(n_phys, 1, cols)
    by_head = lambda a: a.reshape(bd, rows, FOX_HEAD)
    full = lambda shape: pl.BlockSpec((1,) + shape, lambda i, p, pt: (i, 0, 0))

    def page(shape, slot):
        zeros = (0,) * len(shape)
        return pl.BlockSpec((1,) + shape,
                            lambda i, p, pt: (pt[i, n_pages - (p + 1) * pg + slot],) + zeros)

    slots = range(pg)
    kv_page = (PAGE_SIZE, nh, FOX_HEAD)
    grid_spec = pltpu.PrefetchScalarGridSpec(
        num_scalar_prefetch=1,
        grid=(bd, n_steps),
        in_specs=([full((rows, FOX_HEAD)), full((rows, FOX_HEAD)), full((rows, FOX_HEAD)),
                   full((rows, 1)), full((1, rows))]
                  + [page(kv_page, s) for s in slots] + [page(kv_page, s) for s in slots]
                  + [page((1, cols), s) for s in slots] + [page((1, cols), s) for s in slots]),
        out_specs=full((rows, FOX_HEAD)),
        scratch_shapes=[pltpu.VMEM((rows, 1), F32), pltpu.VMEM((rows, 1), F32),
                        pltpu.VMEM((rows, FOX_HEAD), F32), pltpu.VMEM((1, cols), F32)],
    )
    out = pl.pallas_call(
        functools.partial(_fox_sample_kernel, nh=nh, n_steps=n_steps, pg=pg),
        grid_spec=grid_spec,
        out_shape=jax.ShapeDtypeStruct((bd, rows, FOX_HEAD), BF16),
        compiler_params=_params("parallel", "arbitrary"),
        name="fox_sample_attention",
    )(page_table, by_head(q), by_head(k_new), by_head(v_new),
      c_col.reshape(bd, rows, 1), c_col.reshape(bd, 1, rows),
      *([k_pool] * pg), *([v_pool] * pg), *([tail] * pg), *([tot] * pg))
    return out.reshape(bd, t_new, width)


def _conv_kernel(bg_ref, cg_ref, h_ref, w_ref, buf_ref, y_ref, nb_ref, u_scr, *, tt, n_t):
    ti = pl.program_id(2)

    @pl.when(ti == 0)
    def _():
        u_scr[6:8, :] = buf_ref[0]

    u = cg_ref[0] * h_ref[0]
    u_scr[8:8 + tt, :] = u
    y = (w_ref[0:1, :] * u_scr[6:6 + tt, :] + w_ref[1:2, :] * u_scr[7:7 + tt, :]
         + w_ref[2:3, :] * u)
    y_ref[0] = (bg_ref[0] * y).astype(y_ref.dtype)
    tail = u_scr[6 + tt:8 + tt, :]
    u_scr[6:8, :] = tail

    @pl.when(ti == n_t - 1)
    def _():
        nb_ref[0] = tail


def _short_conv(proj3, buf0, conv_w):
    b, l, d3 = proj3.shape
    d = d3 // 3
    tc = _largest_divisor(d, 512, LANES)
    tt = _largest_divisor(l, 704, 8)
    n_t = l // tt
    nc = d // tc
    seg = lambda s: pl.BlockSpec((1, tt, tc), lambda i, j, t: (i, t, s * nc + j))
    return pl.pallas_call(
        functools.partial(_conv_kernel, tt=tt, n_t=n_t),
        grid=(b, nc, n_t),
        in_specs=[seg(0), seg(1), seg(2),
                  pl.BlockSpec((conv_w.shape[0], tc), lambda i, j, t: (0, j)),
                  pl.BlockSpec((1, 2, tc), lambda i, j, t: (i, 0, j))],
        out_specs=[pl.BlockSpec((1, tt, tc), lambda i, j, t: (i, t, j)),
                   pl.BlockSpec((1, 2, tc), lambda i, j, t: (i, 0, j))],
        out_shape=[jax.ShapeDtypeStruct((b, l, d), BF16), jax.ShapeDtypeStruct((b, 2, d), F32)],
        scratch_shapes=[pltpu.VMEM((tt + 8, tc), F32)],
        compiler_params=_params("parallel", "parallel", "arbitrary"),
        name="short_conv",
    )(proj3, proj3, proj3, conv_w, buf0)


def _trunk(hp3, hs3, s0, shift0, conv0, k_pool, v_pool, lf_pool, page_table, wts):
    (norm_mix, norm_mlp, w_in_even, b_forget, rwkv_mu, rwkv_w0, rwkv_w2, rwkv_a0, rwkv_a2, rwkv_g2,
     rwkv_k_k, rwkv_k_a, rwkv_r_k, rwkv_gn_g, rwkv_gn_b, w_out_even, w_in_odd, conv_w, w_out_odd,
     w_up, w_down) = wts
    bp, l, d = hp3.shape
    bs, t, _ = hs3.shape
    hp = hp3.reshape(bp * l, d)
    hs = hs3.reshape(bs * t, d)
    cw = rwkv_w0.shape[-1]
    rproj = rwkv_mu.shape[-1]
    fw = d - cw
    nh = fw // FOX_HEAD
    heads = cw // RWKV_HEAD
    rw = (rwkv_mu[0], rwkv_w0[0], rwkv_w2[0], rwkv_a0[0], rwkv_a2[0], rwkv_g2[0], rwkv_k_k[0],
          rwkv_k_a[0], rwkv_r_k[0], rwkv_gn_g[0], rwkv_gn_b[0])
    norm = lambda a, g: _rmsnorm(a, g, BF16)
    wide = functools.partial(_matmul, tn=512, tk=4096)
    narrow = functools.partial(_matmul, tn=256, tk=4096)
    deep = functools.partial(_matmul, tn=512, tk=2048)

    lpad = RWKV_CHUNK * pl.cdiv(l, RWKV_CHUNK)
    if lpad > l:
        hnp, hnp_pad = _rmsnorm_pad(hp3, norm_mix[0], lpad)
        hnp = hnp.reshape(bp * l, d)
        hnp_pad = hnp_pad.reshape(bp * lpad, d)
    else:
        hnp = hnp_pad = norm(hp, norm_mix[0])
    hns = norm(hs, norm_mix[0])
    pq_p, pq_s = narrow(hnp_pad, hns, w_in_even, 0, n_cols=rproj + fw)
    k_p, k_s = narrow(hnp, hns, w_in_even, 0, n_lo=rproj + fw, n_cols=fw)
    v_p, v_s = narrow(hnp, hns, w_in_even, 0, n_lo=rproj + 2 * fw, n_cols=fw)
    w_lf = w_in_even[:, :, rproj + 3 * fw:]
    z_p, _ = _matmul(hnp, None, w_lf, 0, tn=nh, tk=4096)
    z_s, _ = _matmul(hns, None, w_lf, 0, tn=nh, tk=4096)
    pq_p = pq_p.reshape(bp, lpad, rproj + fw)
    pq_s = pq_s.reshape(bs, t, rproj + fw)
    k_p, v_p = k_p.reshape(bp, l, fw), v_p.reshape(bp, l, fw)
    k_s, v_s = k_s.reshape(bs, t, fw), v_s.reshape(bs, t, fw)
    lf_p, cc_p, cr_p = _logf(z_p.reshape(bp, l, nh), b_forget[0])
    lf_s, cc_s, _ = _logf(z_s.reshape(bs, t, nh), b_forget[0])

    zero_s = jnp.zeros((bp, heads, RWKV_HEAD, RWKV_HEAD), F32)
    or_p, st_p = _rwkv(pq_p, jnp.zeros((bp, rproj), F32), zero_s, *rw, c=RWKV_CHUNK, l_valid=l,
                       l_out=l, o_width=cw)
    of_p = _fox_prompt(pq_p, rproj // LANES, k_p, v_p, cc_p, cr_p)
    tpad = 8 * pl.cdiv(t, 8)
    pq_s_pad = jnp.pad(pq_s[:, :, :rproj], ((0, 0), (0, tpad - t), (0, 0)))
    or_s, st_s = _rwkv(pq_s_pad, shift0[0], s0[0], *rw, c=tpad, l_valid=t, l_out=tpad, o_width=cw)
    of_s = _fox_sample(pq_s[:, :, rproj:], k_s, v_s, cc_s, k_pool[0], v_pool[0], lf_pool[0],
                       page_table)
    mix_p = (or_p.reshape(bp * l, cw), of_p.reshape(bp * l, fw))
    mix_s = (or_s[:, :t].reshape(bs * t, cw), of_s.reshape(bs * t, fw))
    hp, hs = narrow(mix_p, mix_s, w_out_even, 0, epilogue="resid", resid=(hp, hs))
    up, us = wide(norm(hp, norm_mlp[0]), norm(hs, norm_mlp[0]), w_up, 0, epilogue="relu2",
                  out_dtype=BF16)
    hp, hs = deep(up, us, w_down, 0, epilogue="resid", resid=(hp, hs))

    p1_p, p1_s = wide(norm(hp, norm_mix[1]), norm(hs, norm_mix[1]), w_in_odd, 0)
    y_p, buf_p = _short_conv(p1_p.reshape(bp, l, 3 * d), jnp.zeros((bp,) + conv0.shape[2:], F32),
                             conv_w[0])
    y_s, buf_s = _short_conv(p1_s.reshape(bs, t, 3 * d), conv0[0], conv_w[0])
    hp, hs = narrow(y_p.reshape(bp * l, d), y_s.reshape(bs * t, d), w_out_odd, 0, epilogue="resid",
                    resid=(hp, hs))
    up, us = wide(norm(hp, norm_mlp[1]), norm(hs, norm_mlp[1]), w_up, 1, epilogue="relu2",
                  out_dtype=BF16)
    hp, hs = deep(up, us, w_down, 1, epilogue="resid", resid=(hp, hs))

    head4 = lambda a, b, n: a.reshape(b, n, nh, FOX_HEAD)[None]
    outs_p = (st_p[None], pq_p[:, l - 1, :rproj][None], head4(k_p, bp, l), head4(v_p, bp, l),
              lf_p[None], buf_p[None])
    outs_s = (st_s[None], pq_s[:, t - 1, :rproj][None], head4(k_s, bs, t), head4(v_s, bs, t),
              lf_s[None], buf_s[None])
    return hp.reshape(bp, l, d), hs, outs_p, outs_s


def kernel(x_prompt, x_sample, state_rwkv, state_rwkv_shift, cache_fox_k, cache_fox_v, cache_fox_logf, state_conv, page_table, meta_tokens, norm_mix, norm_mlp, norm_final, w_in_even, b_forget, rwkv_mu, rwkv_w0, rwkv_w2, rwkv_a0, rwkv_a2, rwkv_g2, rwkv_k_k, rwkv_k_a, rwkv_r_k, rwkv_gn_g, rwkv_gn_b, w_out_even, w_in_odd, conv_w, w_out_odd, w_up, w_down):
    wts = (norm_mix, norm_mlp, w_in_even, b_forget, rwkv_mu, rwkv_w0, rwkv_w2, rwkv_a0, rwkv_a2,
           rwkv_g2, rwkv_k_k, rwkv_k_a, rwkv_r_k, rwkv_gn_g, rwkv_gn_b, w_out_even, w_in_odd, conv_w,
           w_out_odd, w_up, w_down)
    bp, _, d = x_prompt.shape
    bs, ts, _ = x_sample.shape
    h0 = jnp.concatenate([jnp.broadcast_to(meta_tokens[None], (bp, N_META, d)), x_prompt], axis=1)
    hp, hs, outs_p, outs_s = _trunk(h0, x_sample, state_rwkv, state_rwkv_shift, state_conv,
                                    cache_fox_k, cache_fox_v, cache_fox_logf, page_table, wts)
    y_prompt = _final_norm_prompt(hp, norm_final)
    y_sample = _rmsnorm(hs, norm_final, F32).reshape(bs, ts, d)
    return (y_prompt, y_sample, *outs_p, *outs_s)
```

```python
import functools
import math

import jax
import jax.numpy as jnp
from jax import lax
from jax.experimental import pallas as pl
from jax.experimental.pallas import tpu as pltpu

F32 = jnp.float32
BF16 = jnp.bfloat16
HI = lax.Precision.HIGHEST

RMS_EPS = 1e-6
GN_EPS = 64e-5
NEG_INF = -1e30
N_META = 16
RWKV_HEAD = 64
FOX_HEAD = 128
PAGE_SIZE = 128
LANES = 128
VMEM_LIMIT = 62 * 1024 * 1024
BF16_ROWS = 16
RWKV_CHUNK = 48


def _largest_divisor(n, cap, mult):
    best = None
    for d in range(mult, min(n, cap) + 1, mult):
        if n % d == 0:
            best = d
    return n if best is None else best


def _params(*sem, flags=None):
    return pltpu.CompilerParams(dimension_semantics=sem, vmem_limit_bytes=VMEM_LIMIT, flags=flags)


def _dot(a, b, precision=None):
    return jnp.dot(a, b, precision=precision, preferred_element_type=F32)


def _dot_nt(a, b, precision=None):
    return lax.dot_general(a, b, (((1,), (1,)), ((), ())), precision=precision,
                           preferred_element_type=F32)


def _dot_tn(a, b, precision=None):
    return lax.dot_general(a, b, (((0,), (0,)), ((), ())), precision=precision,
                           preferred_element_type=F32)


def _iota(shape, dim):
    return lax.broadcasted_iota(jnp.int32, shape, dim)


def _split(x, pieces=2):
    out = []
    for i in range(pieces):
        p = x.astype(BF16)
        out.append(p)
        if i + 1 < pieces:
            x = x - p.astype(F32)
    return out


def _dot3(a, b, f=_dot):
    axis = 1 if f is _dot_tn else 0
    m = a[0].shape[axis]
    if m % BF16_ROWS:
        return f(a[0], b[0]) + (f(a[0], b[1]) + f(a[1], b[0]))
    t = f(jnp.concatenate([a[0], a[1]], axis=axis), b[0])
    return (t[:m] + t[m:]) + f(a[0], b[1])


def _dot_exact(pieces, other, f=_dot):
    m = pieces[0].shape[0]
    if f is _dot and m % BF16_ROWS == 0:
        t = f(jnp.concatenate(pieces, axis=0), other)
        return sum(t[i * m:(i + 1) * m] for i in range(1, len(pieces))) + t[:m]
    acc = f(pieces[0], other)
    for p in pieces[1:]:
        acc = acc + f(p, other)
    return acc


def _rmsnorm_kernel(x_ref, g_ref, o_ref):
    x = x_ref[...]
    ms = jnp.mean(x * x, axis=-1, keepdims=True)
    o_ref[...] = (x * lax.rsqrt(ms + RMS_EPS) * g_ref[...]).astype(o_ref.dtype)


def _rmsnorm(x2d, g, out_dtype):
    m, d = x2d.shape
    tr = _largest_divisor(m, 512, 16)
    return pl.pallas_call(
        _rmsnorm_kernel,
        grid=(m // tr,),
        in_specs=[pl.BlockSpec((tr, d), lambda i: (i, 0)),
                  pl.BlockSpec((1, d), lambda i: (0, 0))],
        out_specs=pl.BlockSpec((tr, d), lambda i: (i, 0)),
        out_shape=jax.ShapeDtypeStruct((m, d), out_dtype),
        compiler_params=_params("parallel"),
        name="rmsnorm",
    )(x2d, g.reshape(1, d))


def _rmsnorm_pad_kernel(x_ref, g_ref, o_ref, op_ref, *, n_real):
    x = x_ref[0]
    ms = jnp.mean(x * x, axis=-1, keepdims=True)
    y = (x * lax.rsqrt(ms + RMS_EPS) * g_ref[...]).astype(o_ref.dtype)
    o_ref[0] = y
    op_ref[0] = jnp.where(pl.program_id(1) < n_real, y, jnp.zeros_like(y))


def _rmsnorm_pad(h3, g, lp):
    b, l, d = h3.shape
    tr = _largest_divisor(math.gcd(l, lp), 512, BF16_ROWS)
    n_real = l // tr
    real = pl.BlockSpec((1, tr, d), lambda i, j: (i, jnp.minimum(j, n_real - 1), 0))
    return pl.pallas_call(
        functools.partial(_rmsnorm_pad_kernel, n_real=n_real),
        grid=(b, lp // tr),
        in_specs=[real, pl.BlockSpec((1, d), lambda i, j: (0, 0))],
        out_specs=[real, pl.BlockSpec((1, tr, d), lambda i, j: (i, j, 0))],
        out_shape=[jax.ShapeDtypeStruct((b, l, d), BF16), jax.ShapeDtypeStruct((b, lp, d), BF16)],
        compiler_params=_params("parallel", "arbitrary"),
        name="rmsnorm_pad",
    )(h3, g.reshape(1, d))


def _final_norm_prompt_kernel(a_ref, b_ref, g_ref, o_ref):
    x = jnp.concatenate([a_ref[0, N_META:, :], b_ref[0]], axis=0)
    ms = jnp.mean(x * x, axis=-1, keepdims=True)
    o_ref[0] = x * lax.rsqrt(ms + RMS_EPS) * g_ref[...]


def _final_norm_prompt(h3, g):
    b, l, d = h3.shape
    s = l - N_META
    tr = _largest_divisor(s, 256, N_META)
    sub = tr // N_META
    return pl.pallas_call(
        _final_norm_prompt_kernel,
        grid=(b, s // tr),
        in_specs=[pl.BlockSpec((1, tr, d), lambda i, j: (i, j, 0)),
                  pl.BlockSpec((1, N_META, d), lambda i, j: (i, (j + 1) * sub, 0)),
                  pl.BlockSpec((1, d), lambda i, j: (0, 0))],
        out_specs=pl.BlockSpec((1, tr, d), lambda i, j: (i, j, 0)),
        out_shape=jax.ShapeDtypeStruct((b, s, d), F32),
        compiler_params=_params("parallel", "parallel"),
        name="final_norm_prompt",
    )(h3, h3, g.reshape(1, d))


def _mm_kernel(*refs, nk, nx, epilogue, ns):
    refs = list(refs)
    take = lambda cnt: [refs.pop(0) for _ in range(cnt)]
    xp, xs, (w_ref,) = take(nx), take(nx * ns), take(1)
    rp, rs = (take(1 + ns) + [None])[:2] if epilogue == "resid" else (None, None)
    op, osm = (take(1 + ns) + [None])[:2]
    accp, accs = (take(1 + ns) + [None])[:2] if refs else (None, None)
    i = pl.program_id(0)
    k = pl.program_id(2)
    wb = w_ref[...].astype(BF16)

    def run(x_refs, r_ref, o_ref, acc_ref):
        def finish(acc):
            if epilogue == "relu2":
                acc = jnp.square(jnp.maximum(acc, 0.0))
            elif epilogue == "resid":
                acc = r_ref[...] + acc
            o_ref[...] = acc.astype(o_ref.dtype)

        if nk == 1:
            finish(_dot(x_refs[0][...], wb))
        elif epilogue == "resid" and nx == 2:
            @pl.when(k == 0)
            def _():
                o_ref[...] = r_ref[...] + _dot(x_refs[0][...], wb)

            @pl.when(k == 1)
            def _():
                o_ref[...] += _dot(x_refs[1][...], wb)
        elif epilogue == "resid":
            part = _dot(x_refs[0][...], wb)

            @pl.when(k == 0)
            def _():
                o_ref[...] = r_ref[...] + part

            @pl.when(k > 0)
            def _():
                o_ref[...] += part
        else:
            assert nx == 1
            part = _dot(x_refs[0][...], wb)

            @pl.when(k == 0)
            def _():
                acc_ref[...] = part

            @pl.when(jnp.logical_and(k > 0, k < nk - 1))
            def _():
                acc_ref[...] += part

            @pl.when(k == nk - 1)
            def _():
                finish(acc_ref[...] + part)

    run(xp, rp, op, accp)
    if ns:
        @pl.when(i == 0)
        def _():
            run(xs, rs, osm, accs)


def _matmul(xp, xs, w, layer, *, n_lo=0, n_cols=None, tn, tk, epilogue="plain", resid=None,
            out_dtype=F32, k_slab=(0, 1)):
    xp = xp if isinstance(xp, tuple) else (xp,)
    ns = 0 if xs is None else 1
    xs = () if xs is None else (xs if isinstance(xs, tuple) else (xs,))
    nx = len(xp)
    m, kdim = xp[0].shape[0], nx * xp[0].shape[1]
    ms = xs[0].shape[0] if ns else 0
    n_cols = w.shape[2] - n_lo if n_cols is None else n_cols
    tn = _largest_divisor(math.gcd(n_cols, n_lo), tn, LANES)
    slab, n_slabs = k_slab
    kdim //= n_slabs
    tk = kdim // 2 if nx == 2 else _largest_divisor(kdim, tk, LANES)
    assert n_lo % tn == 0 and n_cols % tn == 0 and (n_slabs == 1 or (tk == kdim and nx == 1))
    tm = _largest_divisor(m, 2112, BF16_ROWS)
    nk = kdim // tk
    nj = n_cols // tn
    off = n_lo // tn
    kblk = (lambda k: k + slab) if nx == 1 else (lambda k: 0)
    once = pl.Buffered(1) if nk == 1 or nx == 2 else None
    in_specs = ([pl.BlockSpec((tm, tk), lambda i, j, k: (i, kblk(k)), pipeline_mode=once)] * nx
                + [pl.BlockSpec((ms, tk), lambda i, j, k: (0, kblk(k)))] * (nx * ns)
                + [pl.BlockSpec((None, tk, tn), lambda i, j, k: (layer, k + slab, j + off))])
    args = [*xp, *xs, w]
    out_specs = [pl.BlockSpec((tm, tn), lambda i, j, k: (i, j))]
    out_shape = [jax.ShapeDtypeStruct((m, n_cols), out_dtype)]
    use_acc = nk > 1 and epilogue != "resid"
    acc = [pltpu.VMEM((tm, tn), F32)] if use_acc else []
    if epilogue == "resid":
        in_specs.append(pl.BlockSpec((tm, tn), lambda i, j, k: (i, j)))
        args.append(resid[0])
    if ns:
        if epilogue == "resid":
            in_specs.append(pl.BlockSpec((ms, tn), lambda i, j, k: (0, jnp.where(i == 0, j, 0))))
            args.append(resid[1])
        out_specs.append(pl.BlockSpec((ms, tn), lambda i, j, k: (0, jnp.where(i == 0, j, nj))))
        out_shape.append(jax.ShapeDtypeStruct((ms, n_cols + tn), out_dtype))
        acc += [pltpu.VMEM((ms, tn), F32)] if use_acc else []
    outs = pl.pallas_call(
        functools.partial(_mm_kernel, nk=nk, nx=nx, epilogue=epilogue, ns=ns),
        grid=(m // tm, nj, nk),
        in_specs=in_specs,
        out_specs=out_specs,
        out_shape=out_shape,
        scratch_shapes=acc,
        compiler_params=_params("arbitrary", "arbitrary", "arbitrary"),
        name="matmul_" + epilogue,
    )(*args)
    return (outs[0], outs[1][:, :n_cols]) if ns else (outs[0], None)


def _rwkv_kernel(pr_ref, pk_ref, pv_ref, px_ref, sr_ref, sk_ref, sv_ref, sx_ref,
                 mur_ref, muk_ref, muv_ref, mux_ref, w0_ref, a0_ref, kk_ref, ka_ref, rk_ref,
                 gg_ref, gb_ref, w2_ref, a2_ref, g2_ref, s0_ref,
                 o_ref, st_ref,
                 s_scr, br_scr, bk_scr, bv_scr, bx_scr, *, c, g, nb, l_valid, n_chunks):
    ci = pl.program_id(2)
    half = RWKV_HEAD
    w = g * LANES
    gs = 1
    nh = 2 * gs
    n = nh * c

    @pl.when(ci == 0)
    def _():
        s_scr[...] = s0_ref[...]
        br_scr[:, 7:8, :] = sr_ref[...]
        bk_scr[:, 7:8, :] = sk_ref[...]
        bv_scr[:, 7:8, :] = sv_ref[...]
        bx_scr[:, 7:8, :] = sx_ref[...]

    ones_w = ((_iota((w, w), 0) // half) == (_iota((w, w), 1) // half)).astype(BF16)
    ones_p = ((_iota((LANES, LANES), 0) // half) == (_iota((LANES, LANES), 1) // half)).astype(F32)
    tri = (_iota((c, c), 1) <= _iota((c, c), 0)).astype(BF16)
    lane_c = _iota((1, LANES), 1)
    m2 = [(lane_c < half).astype(F32), (lane_c >= half).astype(F32)]
    ri = _iota((n, n), 0)
    cj = _iota((n, n), 1)
    in_blk = lambda i, b: jnp.logical_and(i >= b * c, i < (b + 1) * c)
    same = jnp.logical_and(in_blk(ri, 0), in_blk(cj, 0))
    for b in range(1, nh):
        same = jnp.logical_or(same, jnp.logical_and(in_blk(ri, b), in_blk(cj, b)))
    strict = jnp.logical_and(same, cj < ri)
    incl = jnp.logical_and(same, cj <= ri)
    eye = (ri == cj).astype(F32)
    n_doub = max(1, math.ceil(math.log2(c)))
    w2b = w2_ref[...].astype(BF16)
    a2b = a2_ref[...].astype(BF16)
    g2b = g2_ref[...].astype(BF16)
    pair = lambda t, p: t[:, p * LANES:(p + 1) * LANES]

    def seg_sum(t):
        return _dot_exact(_split(t, 3), ones_w)

    def sequence(bi):
        def shifted(p_ref, buf, mu_ref):
            p = p_ref[bi]
            buf[bi, 8:8 + c, :] = p
            prev = buf[bi, 7:7 + c, :]
            buf[bi, 7:8, :] = p[c - 1:c, :]
            return p + (prev - p) * mu_ref[...]

        r = shifted(pr_ref, br_scr, mur_ref)
        k = shifted(pk_ref, bk_scr, muk_ref)
        v = shifted(pv_ref, bv_scr, muv_ref)
        x = shifted(px_ref, bx_scr, mux_ref)
        yield

        wl = w0_ref[...] + _dot(jnp.tanh(x).astype(BF16), w2b)
        wl = -jax.nn.softplus(-wl) - 0.5
        logw = -jnp.exp(wl)
        a = jax.nn.sigmoid(a0_ref[...] + _dot(x.astype(BF16), a2b))
        gate = _dot(jax.nn.sigmoid(x).astype(BF16), g2b)
        kk = k * kk_ref[...]
        yield
        kk = kk / jnp.maximum(jnp.sqrt(seg_sum(kk * kk)), 1e-12)
        k = k * (1.0 + (a - 1.0) * ka_ref[...])
        be = kk * a
        if l_valid < c * n_chunks:
            ok = (_iota((c, w), 0) + ci * c) < l_valid
            logw = jnp.where(ok, logw, 0.0)
            kk = jnp.where(ok, kk, 0.0)
            be = jnp.where(ok, be, 0.0)
            k = jnp.where(ok, k, 0.0)
            v = jnp.where(ok, v, 0.0)

        yield
        cum = _dot_exact(_split(logw, 3), tri, lambda p, t: _dot(t, p))
        yield
        tot = cum[c - 1:c, :]
        ah = -kk * jnp.exp(cum - logw)
        rh = r * jnp.exp(cum)
        ieg = jnp.exp(-cum)
        bc = be * ieg
        kc = k * ieg
        etail = jnp.exp(tot - cum)
        bt = be * etail
        kt = k * etail

        o_parts = {}

        def system(grp):
            stack = lambda t: jnp.concatenate(
                [pair(t, p) * m2[q] for p in grp for q in range(2)], axis=0)
            ars = _split(jnp.concatenate([stack(ah), stack(rh)], axis=0))
            vs = stack(v)
            yield
            gb = _dot3(ars, _split(stack(bc)), _dot_nt)
            yield
            gk = _dot3(ars, _split(stack(kc)), _dot_nt)
            xm = jnp.where(strict, gb[:n], 0.0)
            tm = eye + xm
            xp = _split(xm)
            yield
            if n_doub > 1:
                xp = _split(_dot3(xp, xp))
                yield
                for _ in range(n_doub - 2):
                    ts = _split(tm)
                    both = _dot3([jnp.concatenate([xp[i], ts[i]], axis=0) for i in range(2)], xp)
                    tm = tm + both[n:]
                    xp = _split(both[:n])
                    yield
                tm = tm + _dot3(_split(tm), xp)
                yield
            lm = jnp.concatenate([jnp.where(strict, gk[:n], 0.0), jnp.where(incl, gk[n:], 0.0)], axis=0)
            lmv = _dot3(_split(lm), _split(vs))
            mrb = jnp.where(incl, gb[n:], 0.0)
            yield

            y0 = [_dot3(_split(jnp.concatenate([pair(ah, p), pair(rh, p)], axis=0)),
                        _split(s_scr[bi, p]), _dot_nt) for p in grp]
            yield
            ys = jnp.concatenate([y[:c] * m2[q] for y in y0 for q in range(2)], axis=0) + lmv[:n]
            ps = _dot3(_split(tm), _split(ys))
            yield
            os_ = _dot3(_split(mrb), _split(ps)) + lmv[n:]
            yield
            for i, p in enumerate(grp):
                lo, mid, hi = 2 * i * c, (2 * i + 1) * c, (2 * i + 2) * c
                o_parts[p] = y0[i][c:] + os_[lo:mid] + os_[mid:hi]
                pv = jnp.concatenate([ps[lo:mid] + ps[mid:hi], pair(v, p)], axis=0)
                bk = jnp.concatenate([pair(bt, p), pair(kt, p)], axis=0)
                upd = _dot3(_split(pv), _split(bk), _dot_tn)
                s_scr[bi, p] = (s_scr[bi, p] * jnp.exp(pair(tot, p)) + upd) * ones_p
                yield

        systems = [system(list(range(s, s + gs))) for s in range(0, g, gs)]
        while systems:
            systems = [s for s in systems if next(s, True) is None]
            yield

        o = jnp.concatenate([o_parts[p] for p in range(g)], axis=1) if g > 1 else o_parts[0]
        inv_n = 1.0 / half
        mean = seg_sum(o) * inv_n
        yield
        d = o - mean
        var = seg_sum(d * d) * inv_n
        yield
        on = d * lax.rsqrt(var + GN_EPS) * gg_ref[...] + gb_ref[...]
        bonus = seg_sum(r * k * rk_ref[...]) * v
        o_ref[bi] = ((on + bonus) * gate).astype(o_ref.dtype)

    live = [sequence(bi) for bi in range(nb)]
    while live:
        live = [s for s in live if next(s, True) is None]

    @pl.when(ci == n_chunks - 1)
    def _():
        st_ref[...] = s_scr[...]


def _rwkv(proj3, shift0, s0, mu, w0, w2, a0, a2, g2, k_k, k_a, r_k, gn_g, gn_b, *, c, l_valid,
          l_out, o_width):
    b, lp, _ = proj3.shape
    cw = w0.shape[-1]
    npair = cw // LANES
    heads = cw // RWKV_HEAD
    g = 2 if npair % 2 == 0 else 1
    w = g * LANES
    ng = npair // g
    lw = mu.shape[-1] - 3 * cw
    assert lw % LANES == 0 and (3 * cw) % lw == 0 and lp % c == 0
    n_chunks = lp // c
    wl_, al_ = w2.shape[0], a2.shape[0]
    w2p = jnp.zeros((lw, cw), F32).at[:wl_].set(w2)
    a2p = jnp.zeros((lw, cw), F32).at[wl_:wl_ + al_].set(a2)
    g2p = jnp.zeros((lw, cw), F32).at[wl_ + al_:].set(g2)
    s0p = s0.reshape(b, npair, 2, RWKV_HEAD, RWKV_HEAD)
    z = jnp.zeros_like(s0p[:, :, 0])
    s0bd = jnp.concatenate([jnp.concatenate([s0p[:, :, 0], z], axis=-1),
                            jnp.concatenate([z, s0p[:, :, 1]], axis=-1)], axis=-2)
    sh3 = shift0.reshape(b, 1, -1)
    mu2 = mu.reshape(1, -1)
    row = lambda t: t.reshape(1, cw)
    xoff = (3 * cw) // lw

    nb = _largest_divisor(b, 4, 1)
    seg = lambda s: pl.BlockSpec((nb, c, w), lambda i, j, t: (i, t, s * ng + j))
    sseg = lambda s: pl.BlockSpec((nb, 1, w), lambda i, j, t: (i, 0, s * ng + j))
    mseg = lambda s: pl.BlockSpec((1, w), lambda i, j, t: (0, s * ng + j))
    vec = pl.BlockSpec((1, w), lambda i, j, t: (0, j))
    lora = pl.BlockSpec((lw, w), lambda i, j, t: (0, j))
    state = pl.BlockSpec((nb, g, LANES, LANES), lambda i, j, t: (i, j, 0, 0))
    in_specs = [seg(0), seg(1), seg(2), pl.BlockSpec((nb, c, lw), lambda i, j, t: (i, t, xoff)),
                sseg(0), sseg(1), sseg(2), pl.BlockSpec((nb, 1, lw), lambda i, j, t: (i, 0, xoff)),
                mseg(0), mseg(1), mseg(2), pl.BlockSpec((1, lw), lambda i, j, t: (0, xoff)),
                vec, vec, vec, vec, vec, vec, vec, lora, lora, lora, state]
    o, st = pl.pallas_call(
        functools.partial(_rwkv_kernel, c=c, g=g, nb=nb, l_valid=l_valid, n_chunks=n_chunks),
        grid=(b // nb, ng, n_chunks),
        in_specs=in_specs,
        out_specs=[pl.BlockSpec((nb, c, w), lambda i, j, t: (i, t, j)), state],
        out_shape=[jax.ShapeDtypeStruct((b, l_out, o_width), BF16),
                   jax.ShapeDtypeStruct((b, npair, LANES, LANES), F32)],
        scratch_shapes=[pltpu.VMEM((nb, g, LANES, LANES), F32),
                        pltpu.VMEM((nb, c + 8, w), F32), pltpu.VMEM((nb, c + 8, w), F32),
                        pltpu.VMEM((nb, c + 8, w), F32), pltpu.VMEM((nb, c + 8, lw), F32)],
        compiler_params=_params("parallel", "parallel", "arbitrary"),
        name="rwkv7_chunk",
    )(proj3, proj3, proj3, proj3, sh3, sh3, sh3, sh3, mu2, mu2, mu2, mu2,
      row(w0), row(a0), row(k_k), row(k_a), row(r_k), row(gn_g), row(gn_b), w2p, a2p, g2p, s0bd)
    st = jnp.stack([st[:, :, :RWKV_HEAD, :RWKV_HEAD], st[:, :, RWKV_HEAD:, RWKV_HEAD:]], axis=2)
    return o, st.reshape(b, heads, RWKV_HEAD, RWKV_HEAD)


def _logf_kernel(z_ref, bf_ref, lf_ref, cc_ref, cr_ref, *, rb):
    l, h = z_ref.shape[1], z_ref.shape[2]
    ti = _iota((rb, rb), 0)
    si = _iota((rb, rb), 1)
    tri = (si <= ti).astype(F32)
    carry = jnp.zeros((1, h), F32)
    for i in range(l // rb):
        sl = slice(i * rb, (i + 1) * rb)
        lf = jax.nn.log_sigmoid(z_ref[0, sl, :] + bf_ref[...])
        lf_ref[0, sl, :] = lf
        cblk = carry + _dot(tri, lf, HI)
        cc_ref[0, sl, :] = cblk
        carry = cblk[rb - 1:rb, :]
    eye = (_iota((h, h), 0) == _iota((h, h), 1)).astype(F32)
    cr_ref[0] = _dot_nt(eye, cc_ref[0], HI)


def _logf(z3, b_forget):
    b, l, h = z3.shape
    rb = _largest_divisor(l, 512, 8)
    blk = pl.BlockSpec((1, l, h), lambda i: (i, 0, 0))
    return pl.pallas_call(
        functools.partial(_logf_kernel, rb=rb),
        grid=(b,),
        in_specs=[blk, pl.BlockSpec((1, h), lambda i: (0, 0))],
        out_specs=[blk, blk, pl.BlockSpec((1, h, l), lambda i: (i, 0, 0))],
        out_shape=[jax.ShapeDtypeStruct((b, l, h), F32), jax.ShapeDtypeStruct((b, l, h), F32),
                   jax.ShapeDtypeStruct((b, h, l), F32)],
        compiler_params=_params("parallel"),
        name="fox_logf_cumsum",
    )(z3, b_forget.reshape(1, h))


def _fox_prompt_kernel(q_ref, k_ref, v_ref, cc_ref, cr_ref, o_ref, *, tq):
    l = q_ref.shape[1]
    nh = cc_ref.shape[2]
    h = pl.program_id(1)
    scale = FOX_HEAD ** -0.5
    sel = (_iota((nh, LANES), 0) == h).astype(F32)
    cq_all = _dot(cc_ref[0], sel, HI)
    ck = cr_ref[0, pl.ds(h, 1), :]
    kb = k_ref[0].astype(BF16)
    vb = v_ref[0].astype(BF16)
    for i in range(l // tq):
        nk = (i + 1) * tq
        rows = slice(i * tq, nk)
        s = _dot_nt(q_ref[0, rows, :].astype(BF16), kb[:nk]) * scale
        s = s + cq_all[rows, 0:1] - ck[:, :nk]
        qpos = _iota((tq, nk), 0) + i * tq
        kpos = _iota((tq, nk), 1)
        s = jnp.where(qpos >= kpos, s, NEG_INF)
        m = jnp.max(s, axis=-1, keepdims=True)
        p = jnp.exp(s - m)
        den = jnp.sum(p, axis=-1, keepdims=True)
        o = _dot(p.astype(BF16), vb[:nk]) / den
        o_ref[0, rows, :] = o.astype(o_ref.dtype)


def _fox_prompt(proj3, q_off, k3, v3, c_col, c_row):
    b, l, fw = k3.shape
    nh = c_col.shape[2]
    tq = _largest_divisor(l, 384, 8)
    blk = lambda off: pl.BlockSpec((1, l, LANES), lambda i, j: (i, 0, off + j))
    return pl.pallas_call(
        functools.partial(_fox_prompt_kernel, tq=tq),
        grid=(b, nh),
        in_specs=[blk(q_off), blk(0), blk(0),
                  pl.BlockSpec((1, l, nh), lambda i, j: (i, 0, 0)),
                  pl.BlockSpec((1, nh, l), lambda i, j: (i, 0, 0))],
        out_specs=blk(0),
        out_shape=jax.ShapeDtypeStruct((b, l, fw), BF16),
        compiler_params=_params("parallel", "parallel"),
        name="fox_prompt_attention",
    )(proj3, k3, v3, c_col, c_row)


def _page_tail_kernel(lf_ref, tail_ref, tot_ref):
    pp = lf_ref.shape[0]
    later = (_iota((PAGE_SIZE, PAGE_SIZE), 1) > _iota((PAGE_SIZE, PAGE_SIZE), 0)).astype(BF16)
    for i in range(pp):
        lf = lf_ref[i]
        tail_ref[i] = _dot_exact(_split(lf, 3), later, lambda p, u: _dot(u, p))
        tot_ref[i] = jnp.broadcast_to(jnp.sum(lf, axis=0, keepdims=True), lf.shape)


def _page_tails(lf_pool):
    n_phys, _, nh = lf_pool.shape
    pp = _largest_divisor(n_phys, 32, 1)
    blk = pl.BlockSpec((pp, PAGE_SIZE, nh), lambda i: (i, 0, 0))
    shape = jax.ShapeDtypeStruct(lf_pool.shape, F32)
    return pl.pallas_call(
        _page_tail_kernel,
        grid=(n_phys // pp,),
        in_specs=[blk],
        out_specs=[blk, blk],
        out_shape=[shape, shape],
        compiler_params=_params("parallel"),
        name="fox_page_tails",
    )(lf_pool)


def _fox_sample_kernel(pt_ref, q_ref, kn_ref, vn_ref, cq_ref, ck_ref, *rest, nh, n_steps, pg):
    kp = rest[:pg]
    vp = rest[pg:2 * pg]
    tl = rest[2 * pg:3 * pg]
    tt = rest[3 * pg:4 * pg]
    o_ref, m_scr, l_scr, acc_scr, suf_scr = rest[4 * pg:]
    pi = pl.program_id(1)
    rows = q_ref.shape[1]
    cols = PAGE_SIZE * nh
    scale = FOX_HEAD ** -0.5

    @pl.when(pi == 0)
    def _():
        m_scr[...] = jnp.full(m_scr.shape, NEG_INF, F32)
        l_scr[...] = jnp.zeros(l_scr.shape, F32)
        acc_scr[...] = jnp.zeros(acc_scr.shape, F32)
        suf_scr[...] = jnp.zeros(suf_scr.shape, F32)

    qb = q_ref[0].astype(BF16)
    cnew = cq_ref[0]

    def online(scores, vals):
        m_old = m_scr[...]
        m_new = m_old
        for s in scores:
            m_new = jnp.maximum(m_new, jnp.max(s, axis=-1, keepdims=True))
        corr = jnp.exp(m_old - m_new)
        den = l_scr[...] * corr
        acc = acc_scr[...] * corr
        for s, val in zip(scores, vals):
            pe = jnp.exp(s - m_new)
            den = den + jnp.sum(pe, axis=-1, keepdims=True)
            acc = acc + _dot(pe.astype(BF16), val)
        l_scr[...] = den
        acc_scr[...] = acc
        m_scr[...] = m_new

    own_head = (_iota((rows, cols), 0) % nh) == (_iota((rows, cols), 1) % nh)
    suf = suf_scr[...]
    bias = [None] * pg
    for gi in reversed(range(pg)):
        bias[gi] = suf + tl[gi][0]
        suf = suf + tt[gi][0]
    suf_scr[...] = suf
    scores = []
    for gi in range(pg):
        kflat = kp[gi][0].reshape(cols, FOX_HEAD).astype(BF16)
        s = _dot_nt(qb, kflat) * scale + bias[gi] + cnew
        scores.append(jnp.where(own_head, s, NEG_INF))
    online(scores, [r[0].reshape(cols, FOX_HEAD).astype(BF16) for r in vp])

    @pl.when(pi == n_steps - 1)
    def _():
        sn = _dot_nt(qb, kn_ref[0].astype(BF16)) * scale + cnew - ck_ref[0]
        ri = _iota((rows, rows), 0)
        cj = _iota((rows, rows), 1)
        ok = jnp.logical_and((ri % nh) == (cj % nh), (cj // nh) <= (ri // nh))
        online([jnp.where(ok, sn, NEG_INF)], [vn_ref[0].astype(BF16)])
        o_ref[0] = (acc_scr[...] / l_scr[...]).astype(o_ref.dtype)


def _fox_sample(q, k_new, v_new, c_col, k_pool, v_pool, lf_pool, page_table):
    bd, t_new, width = q.shape
    nh = c_col.shape[2]
    n_pages = page_table.shape[1]
    n_phys = k_pool.shape[0]
    pg = 4 if n_pages % 4 == 0 else 1
    n_steps = n_pages // pg
    rows = t_new * nh
    cols = PAGE_SIZE * nh
    tail, tot = _page_tails(lf_pool)
    tail = tail.reshape(n_phys, 1, cols)
    tot = tot.reshape(n_phys, 1, cols)
    by_head = lambda a: a.reshape(bd, rows, FOX_HEAD)
    full = lambda shape: pl.BlockSpec((1,) + shape, lambda i, p, pt: (i, 0, 0))

    def page(shape, slot):
        zeros = (0,) * len(shape)
        return pl.BlockSpec((1,) + shape,
                            lambda i, p, pt: (pt[i, n_pages - (p + 1) * pg + slot],) + zeros)

    slots = range(pg)
    kv_page = (PAGE_SIZE, nh, FOX_HEAD)
    grid_spec = pltpu.PrefetchScalarGridSpec(
        num_scalar_prefetch=1,
        grid=(bd, n_steps),
        in_specs=([full((rows, FOX_HEAD)), full((rows, FOX_HEAD)), full((rows, FOX_HEAD)),
                   full((rows, 1)), full((1, rows))]
                  + [page(kv_page, s) for s in slots] + [page(kv_page, s) for s in slots]
                  + [page((1, cols), s) for s in slots] + [page((1, cols), s) for s in slots]),
        out_specs=full((rows, FOX_HEAD)),
        scratch_shapes=[pltpu.VMEM((rows, 1), F32), pltpu.VMEM((rows, 1), F32),
                        pltpu.VMEM((rows, FOX_HEAD), F32), pltpu.VMEM((1, cols), F32)],
    )
    out = pl.pallas_call(
        functools.partial(_fox_sample_kernel, nh=nh, n_steps=n_steps, pg=pg),
        grid_spec=grid_spec,
        out_shape=jax.ShapeDtypeStruct((bd, rows, FOX_HEAD), BF16),
        compiler_params=_params("parallel", "arbitrary"),
        name="fox_sample_attention",
    )(page_table, by_head(q), by_head(k_new), by_head(v_new),
      c_col.reshape(bd, rows, 1), c_col.reshape(bd, 1, rows),
      *([k_pool] * pg), *([v_pool] * pg), *([tail] * pg), *([tot] * pg))
    return out.reshape(bd, t_new, width)


def _conv_kernel(bg_ref, cg_ref, h_ref, w_ref, buf_ref, y_ref, nb_ref, u_scr, *, tt, n_t):
    ti = pl.program_id(2)

    @pl.when(ti == 0)
    def _():
        u_scr[6:8, :] = buf_ref[0]

    u = cg_ref[0] * h_ref[0]
    u_scr[8:8 + tt, :] = u
    y = (w_ref[0:1, :] * u_scr[6:6 + tt, :] + w_ref[1:2, :] * u_scr[7:7 + tt, :]
         + w_ref[2:3, :] * u)
    y_ref[0] = (bg_ref[0] * y).astype(y_ref.dtype)
    tail = u_scr[6 + tt:8 + tt, :]
    u_scr[6:8, :] = tail

    @pl.when(ti == n_t - 1)
    def _():
        nb_ref[0] = tail


def _short_conv(proj3, buf0, conv_w):
    b, l, d3 = proj3.shape
    d = d3 // 3
    tc = _largest_divisor(d, 512, LANES)
    tt = _largest_divisor(l, 704, 8)
    n_t = l // tt
    nc = d // tc
    seg = lambda s: pl.BlockSpec((1, tt, tc), lambda i, j, t: (i, t, s * nc + j))
    return pl.pallas_call(
        functools.partial(_conv_kernel, tt=tt, n_t=n_t),
        grid=(b, nc, n_t),
        in_specs=[seg(0), seg(1), seg(2),
                  pl.BlockSpec((conv_w.shape[0], tc), lambda i, j, t: (0, j)),
                  pl.BlockSpec((1, 2, tc), lambda i, j, t: (i, 0, j))],
        out_specs=[pl.BlockSpec((1, tt, tc), lambda i, j, t: (i, t, j)),
                   pl.BlockSpec((1, 2, tc), lambda i, j, t: (i, 0, j))],
        out_shape=[jax.ShapeDtypeStruct((b, l, d), BF16), jax.ShapeDtypeStruct((b, 2, d), F32)],
        scratch_shapes=[pltpu.VMEM((tt + 8, tc), F32)],
        compiler_params=_params("parallel", "parallel", "arbitrary"),
        name="short_conv",
    )(proj3, proj3, proj3, conv_w, buf0)


def _trunk(hp3, hs3, s0, shift0, conv0, k_pool, v_pool, lf_pool, page_table, wts):
    (norm_mix, norm_mlp, w_in_even, b_forget, rwkv_mu, rwkv_w0, rwkv_w2, rwkv_a0, rwkv_a2, rwkv_g2,
     rwkv_k_k, rwkv_k_a, rwkv_r_k, rwkv_gn_g, rwkv_gn_b, w_out_even, w_in_odd, conv_w, w_out_odd,
     w_up, w_down) = wts
    bp, l, d = hp3.shape
    bs, t, _ = hs3.shape
    hp = hp3.reshape(bp * l, d)
    hs = hs3.reshape(bs * t, d)
    cw = rwkv_w0.shape[-1]
    rproj = rwkv_mu.shape[-1]
    fw = d - cw
    nh = fw // FOX_HEAD
    heads = cw // RWKV_HEAD
    rw = (rwkv_mu[0], rwkv_w0[0], rwkv_w2[0], rwkv_a0[0], rwkv_a2[0], rwkv_g2[0], rwkv_k_k[0],
          rwkv_k_a[0], rwkv_r_k[0], rwkv_gn_g[0], rwkv_gn_b[0])
    norm = lambda a, g: _rmsnorm(a, g, BF16)
    wide = functools.partial(_matmul, tn=512, tk=4096)
    narrow = functools.partial(_matmul, tn=256, tk=4096)

    def down(up, us, layer, hp, hs):
        n_slabs = max(1, w_down.shape[1] // d)
        for slab in range(n_slabs):
            hp, hs = wide(up, us, w_down, layer, epilogue="resid", resid=(hp, hs),
                          k_slab=(slab, n_slabs))
        return hp, hs

    lpad = RWKV_CHUNK * pl.cdiv(l, RWKV_CHUNK)
    if lpad > l:
        hnp, hnp_pad = _rmsnorm_pad(hp3, norm_mix[0], lpad)
        hnp = hnp.reshape(bp * l, d)
        hnp_pad = hnp_pad.reshape(bp * lpad, d)
    else:
        hnp = hnp_pad = norm(hp, norm_mix[0])
    hns = norm(hs, norm_mix[0])
    pq_p, pq_s = narrow(hnp_pad, hns, w_in_even, 0, n_cols=rproj + fw)
    k_p, k_s = narrow(hnp, hns, w_in_even, 0, n_lo=rproj + fw, n_cols=fw)
    v_p, v_s = narrow(hnp, hns, w_in_even, 0, n_lo=rproj + 2 * fw, n_cols=fw)
    w_lf = w_in_even[:, :, rproj + 3 * fw:]
    z_p, _ = _matmul(hnp, None, w_lf, 0, tn=nh, tk=4096)
    z_s, _ = _matmul(hns, None, w_lf, 0, tn=nh, tk=4096)
    pq_p = pq_p.reshape(bp, lpad, rproj + fw)
    pq_s = pq_s.reshape(bs, t, rproj + fw)
    k_p, v_p = k_p.reshape(bp, l, fw), v_p.reshape(bp, l, fw)
    k_s, v_s = k_s.reshape(bs, t, fw), v_s.reshape(bs, t, fw)
    lf_p, cc_p, cr_p = _logf(z_p.reshape(bp, l, nh), b_forget[0])
    lf_s, cc_s, _ = _logf(z_s.reshape(bs, t, nh), b_forget[0])

    zero_s = jnp.zeros((bp, heads, RWKV_HEAD, RWKV_HEAD), F32)
    or_p, st_p = _rwkv(pq_p, jnp.zeros((bp, rproj), F32), zero_s, *rw, c=RWKV_CHUNK, l_valid=l,
                       l_out=l, o_width=cw)
    of_p = _fox_prompt(pq_p, rproj // LANES, k_p, v_p, cc_p, cr_p)
    tpad = 8 * pl.cdiv(t, 8)
    pq_s_pad = jnp.pad(pq_s[:, :, :rproj], ((0, 0), (0, tpad - t), (0, 0)))
    or_s, st_s = _rwkv(pq_s_pad, shift0[0], s0[0], *rw, c=tpad, l_valid=t, l_out=tpad, o_width=cw)
    of_s = _fox_sample(pq_s[:, :, rproj:], k_s, v_s, cc_s, k_pool[0], v_pool[0], lf_pool[0],
                       page_table)
    mix_p = (or_p.reshape(bp * l, cw), of_p.reshape(bp * l, fw))
    mix_s = (or_s[:, :t].reshape(bs * t, cw), of_s.reshape(bs * t, fw))
    hp, hs = wide(mix_p, mix_s, w_out_even, 0, epilogue="resid", resid=(hp, hs))
    up, us = wide(norm(hp, norm_mlp[0]), norm(hs, norm_mlp[0]), w_up, 0, epilogue="relu2",
                  out_dtype=BF16)
    hp, hs = down(up, us, 0, hp, hs)

    p1_p, p1_s = wide(norm(hp, norm_mix[1]), norm(hs, norm_mix[1]), w_in_odd, 0)
    y_p, buf_p = _short_conv(p1_p.reshape(bp, l, 3 * d), jnp.zeros((bp,) + conv0.shape[2:], F32),
                             conv_w[0])
    y_s, buf_s = _short_conv(p1_s.reshape(bs, t, 3 * d), conv0[0], conv_w[0])
    hp, hs = wide(y_p.reshape(bp * l, d), y_s.reshape(bs * t, d), w_out_odd, 0, epilogue="resid",
                  resid=(hp, hs))
    up, us = wide(norm(hp, norm_mlp[1]), norm(hs, norm_mlp[1]), w_up, 1, epilogue="relu2",
                  out_dtype=BF16)
    hp, hs = down(up, us, 1, hp, hs)

    head4 = lambda a, b, n: a.reshape(b, n, nh, FOX_HEAD)[None]
    outs_p = (st_p[None], pq_p[:, l - 1, :rproj][None], head4(k_p, bp, l), head4(v_p, bp, l),
              lf_p[None], buf_p[None])
    outs_s = (st_s[None], pq_s[:, t - 1, :rproj][None], head4(k_s, bs, t), head4(v_s, bs, t),
              lf_s[None], buf_s[None])
    return hp.reshape(bp, l, d), hs, outs_p, outs_s


def kernel(x_prompt, x_sample, state_rwkv, state_rwkv_shift, cache_fox_k, cache_fox_v, cache_fox_logf, state_conv, page_table, meta_tokens, norm_mix, norm_mlp, norm_final, w_in_even, b_forget, rwkv_mu, rwkv_w0, rwkv_w2, rwkv_a0, rwkv_a2, rwkv_g2, rwkv_k_k, rwkv_k_a, rwkv_r_k, rwkv_gn_g, rwkv_gn_b, w_out_even, w_in_odd, conv_w, w_out_odd, w_up, w_down):
    wts = (norm_mix, norm_mlp, w_in_even, b_forget, rwkv_mu, rwkv_w0, rwkv_w2, rwkv_a0, rwkv_a2,
           rwkv_g2, rwkv_k_k, rwkv_k_a, rwkv_r_k, rwkv_gn_g, rwkv_gn_b, w_out_even, w_in_odd, conv_w,
           w_out_odd, w_up, w_down)
    bp, _, d = x_prompt.shape
    bs, ts, _ = x_sample.shape
    h0 = jnp.concatenate([jnp.broadcast_to(meta_tokens[None], (bp, N_META, d)), x_prompt], axis=1)
    hp, hs, outs_p, outs_s = _trunk(h0, x_sample, state_rwkv, state_rwkv_shift, state_conv,
                                    cache_fox_k, cache_fox_v, cache_fox_logf, page_table, wts)
    y_prompt = _final_norm_prompt(hp, norm_final)
    y_sample = _rmsnorm(hs, norm_final, F32).reshape(bs, ts, d)
    return (y_prompt, y_sample, *outs_p, *outs_s)
```

```python
import functools
import math

import jax
import jax.numpy as jnp
from jax import lax
from jax.experimental import pallas as pl
from jax.experimental.pallas import tpu as pltpu

F32 = jnp.float32
BF16 = jnp.bfloat16
HI = lax.Precision.HIGHEST

RMS_EPS = 1e-6
GN_EPS = 64e-5
NEG_INF = -1e30
N_META = 16
RWKV_HEAD = 64
FOX_HEAD = 128
PAGE_SIZE = 128
LANES = 128
VMEM_LIMIT = 62 * 1024 * 1024
BF16_ROWS = 16
RWKV_CHUNK = 48


def _largest_divisor(n, cap, mult):
    best = None
    for d in range(mult, min(n, cap) + 1, mult):
        if n % d == 0:
            best = d
    return n if best is None else best


def _params(*sem, flags=None):
    return pltpu.CompilerParams(dimension_semantics=sem, vmem_limit_bytes=VMEM_LIMIT, flags=flags)


def _dot(a, b, precision=None):
    return jnp.dot(a, b, precision=precision, preferred_element_type=F32)


def _dot_nt(a, b, precision=None):
    return lax.dot_general(a, b, (((1,), (1,)), ((), ())), precision=precision,
                           preferred_element_type=F32)


def _dot_tn(a, b, precision=None):
    return lax.dot_general(a, b, (((0,), (0,)), ((), ())), precision=precision,
                           preferred_element_type=F32)


def _iota(shape, dim):
    return lax.broadcasted_iota(jnp.int32, shape, dim)


def _split(x, pieces=2):
    out = []
    for i in range(pieces):
        p = x.astype(BF16)
        out.append(p)
        if i + 1 < pieces:
            x = x - p.astype(F32)
    return out


def _dot3(a, b, f=_dot):
    axis = 1 if f is _dot_tn else 0
    m = a[0].shape[axis]
    if m % BF16_ROWS:
        return f(a[0], b[0]) + (f(a[0], b[1]) + f(a[1], b[0]))
    t = f(jnp.concatenate([a[0], a[1]], axis=axis), b[0])
    return (t[:m] + t[m:]) + f(a[0], b[1])


def _dot_exact(pieces, other, f=_dot):
    m = pieces[0].shape[0]
    if f is _dot and m % BF16_ROWS == 0:
        t = f(jnp.concatenate(pieces, axis=0), other)
        return sum(t[i * m:(i + 1) * m] for i in range(1, len(pieces))) + t[:m]
    acc = f(pieces[0], other)
    for p in pieces[1:]:
        acc = acc + f(p, other)
    return acc


def _rmsnorm_kernel(x_ref, g_ref, o_ref):
    x = x_ref[...]
    ms = jnp.mean(x * x, axis=-1, keepdims=True)
    o_ref[...] = (x * lax.rsqrt(ms + RMS_EPS) * g_ref[...]).astype(o_ref.dtype)


def _rmsnorm(x2d, g, out_dtype):
    m, d = x2d.shape
    tr = _largest_divisor(m, 512, 16)
    return pl.pallas_call(
        _rmsnorm_kernel,
        grid=(m // tr,),
        in_specs=[pl.BlockSpec((tr, d), lambda i: (i, 0)),
                  pl.BlockSpec((1, d), lambda i: (0, 0))],
        out_specs=pl.BlockSpec((tr, d), lambda i: (i, 0)),
        out_shape=jax.ShapeDtypeStruct((m, d), out_dtype),
        compiler_params=_params("parallel"),
        name="rmsnorm",
    )(x2d, g.reshape(1, d))


def _rmsnorm_pad_kernel(x_ref, g_ref, o_ref, op_ref, *, n_real):
    x = x_ref[0]
    ms = jnp.mean(x * x, axis=-1, keepdims=True)
    y = (x * lax.rsqrt(ms + RMS_EPS) * g_ref[...]).astype(o_ref.dtype)
    o_ref[0] = y
    op_ref[0] = jnp.where(pl.program_id(1) < n_real, y, jnp.zeros_like(y))


def _rmsnorm_pad(h3, g, lp):
    b, l, d = h3.shape
    tr = _largest_divisor(math.gcd(l, lp), 512, BF16_ROWS)
    n_real = l // tr
    real = pl.BlockSpec((1, tr, d), lambda i, j: (i, jnp.minimum(j, n_real - 1), 0))
    return pl.pallas_call(
        functools.partial(_rmsnorm_pad_kernel, n_real=n_real),
        grid=(b, lp // tr),
        in_specs=[real, pl.BlockSpec((1, d), lambda i, j: (0, 0))],
        out_specs=[real, pl.BlockSpec((1, tr, d), lambda i, j: (i, j, 0))],
        out_shape=[jax.ShapeDtypeStruct((b, l, d), BF16), jax.ShapeDtypeStruct((b, lp, d), BF16)],
        compiler_params=_params("parallel", "arbitrary"),
        name="rmsnorm_pad",
    )(h3, g.reshape(1, d))


def _final_norm_prompt_kernel(a_ref, b_ref, g_ref, o_ref):
    x = jnp.concatenate([a_ref[0, N_META:, :], b_ref[0]], axis=0)
    ms = jnp.mean(x * x, axis=-1, keepdims=True)
    o_ref[0] = x * lax.rsqrt(ms + RMS_EPS) * g_ref[...]


def _final_norm_prompt(h3, g):
    b, l, d = h3.shape
    s = l - N_META
    tr = _largest_divisor(s, 256, N_META)
    sub = tr // N_META
    return pl.pallas_call(
        _final_norm_prompt_kernel,
        grid=(b, s // tr),
        in_specs=[pl.BlockSpec((1, tr, d), lambda i, j: (i, j, 0)),
                  pl.BlockSpec((1, N_META, d), lambda i, j: (i, (j + 1) * sub, 0)),
                  pl.BlockSpec((1, d), lambda i, j: (0, 0))],
        out_specs=pl.BlockSpec((1, tr, d), lambda i, j: (i, j, 0)),
        out_shape=jax.ShapeDtypeStruct((b, s, d), F32),
        compiler_params=_params("parallel", "parallel"),
        name="final_norm_prompt",
    )(h3, h3, g.reshape(1, d))


def _mm_kernel(*refs, nk, nx, epilogue, ns, spare):
    refs = list(refs)
    take = lambda cnt: [refs.pop(0) for _ in range(cnt)]
    xp, xs, (w_ref,) = take(nx), take(nx * ns), take(1)
    rp, rs = (take(1 + ns) + [None])[:2] if epilogue == "resid" else (None, None)
    op, osm = (take(1 + ns) + [None])[:2]
    accp, accs = (take(1 + ns) + [None])[:2] if refs else (None, None)
    i = pl.program_id(0)
    k = pl.program_id(2)
    wb = w_ref[...].astype(BF16)

    def run(x_refs, r_ref, o_ref, acc_ref):
        def finish(acc):
            if epilogue == "relu2":
                acc = jnp.square(jnp.maximum(acc, 0.0))
            elif epilogue == "resid":
                acc = r_ref[...] + acc
            o_ref[...] = acc.astype(o_ref.dtype)

        if nk == 1:
            finish(_dot(x_refs[0][...], wb))
        elif epilogue == "resid" and nx == 2:
            @pl.when(k == 0)
            def _():
                o_ref[...] = r_ref[...] + _dot(x_refs[0][...], wb)

            @pl.when(k == 1)
            def _():
                o_ref[...] += _dot(x_refs[1][...], wb)
        elif epilogue == "resid":
            part = _dot(x_refs[0][...], wb)

            @pl.when(k == 0)
            def _():
                o_ref[...] = r_ref[...] + part

            @pl.when(k > 0)
            def _():
                o_ref[...] += part
        else:
            assert nx == 1
            part = _dot(x_refs[0][...], wb)

            @pl.when(k == 0)
            def _():
                acc_ref[...] = part

            @pl.when(jnp.logical_and(k > 0, k < nk - 1))
            def _():
                acc_ref[...] += part

            @pl.when(k == nk - 1)
            def _():
                finish(acc_ref[...] + part)

    run(xp, rp, op, accp)
    if ns:
        @pl.when(i == 0)
        def _():
            run(xs, rs, osm, accs)

        if spare:
            @pl.when(i > 0)
            def _():
                osm[...] = jnp.zeros(osm.shape, osm.dtype)


def _matmul(xp, xs, w, layer, *, n_lo=0, n_cols=None, tn, tk, epilogue="plain", resid=None,
            out_dtype=F32, k_slab=(0, 1)):
    xp = xp if isinstance(xp, tuple) else (xp,)
    ns = 0 if xs is None else 1
    xs = () if xs is None else (xs if isinstance(xs, tuple) else (xs,))
    nx = len(xp)
    m, kdim = xp[0].shape[0], nx * xp[0].shape[1]
    ms = xs[0].shape[0] if ns else 0
    n_cols = w.shape[2] - n_lo if n_cols is None else n_cols
    tn = _largest_divisor(math.gcd(n_cols, n_lo), tn, LANES)
    slab, n_slabs = k_slab
    kdim //= n_slabs
    tk = kdim // 2 if nx == 2 else _largest_divisor(kdim, tk, LANES)
    assert n_lo % tn == 0 and n_cols % tn == 0 and (n_slabs == 1 or (tk == kdim and nx == 1))
    tm = _largest_divisor(m, 2112, BF16_ROWS)
    nk = kdim // tk
    nj = n_cols // tn
    off = n_lo // tn
    kblk = (lambda k: k + slab) if nx == 1 else (lambda k: 0)
    once = pl.Buffered(1) if nk == 1 or nx == 2 else None
    in_specs = ([pl.BlockSpec((tm, tk), lambda i, j, k: (i, kblk(k)), pipeline_mode=once)] * nx
                + [pl.BlockSpec((ms, tk), lambda i, j, k: (0, kblk(k)))] * (nx * ns)
                + [pl.BlockSpec((None, tk, tn), lambda i, j, k: (layer, k + slab, j + off))])
    args = [*xp, *xs, w]
    out_specs = [pl.BlockSpec((tm, tn), lambda i, j, k: (i, j))]
    out_shape = [jax.ShapeDtypeStruct((m, n_cols), out_dtype)]
    use_acc = nk > 1 and epilogue != "resid"
    acc = [pltpu.VMEM((tm, tn), F32)] if use_acc else []
    if epilogue == "resid":
        in_specs.append(pl.BlockSpec((tm, tn), lambda i, j, k: (i, j)))
        args.append(resid[0])
    spare = m // tm > 1
    if ns:
        if epilogue == "resid":
            in_specs.append(pl.BlockSpec((ms, tn), lambda i, j, k: (0, jnp.where(i == 0, j, 0))))
            args.append(resid[1])
        out_specs.append(pl.BlockSpec((ms, tn), lambda i, j, k: (0, jnp.where(i == 0, j, nj))))
        out_shape.append(jax.ShapeDtypeStruct((ms, n_cols + tn * spare), out_dtype))
        acc += [pltpu.VMEM((ms, tn), F32)] if use_acc else []
    outs = pl.pallas_call(
        functools.partial(_mm_kernel, nk=nk, nx=nx, epilogue=epilogue, ns=ns, spare=spare),
        grid=(m // tm, nj, nk),
        in_specs=in_specs,
        out_specs=out_specs,
        out_shape=out_shape,
        scratch_shapes=acc,
        compiler_params=_params("arbitrary", "arbitrary", "arbitrary"),
        name="matmul_" + epilogue,
    )(*args)
    return (outs[0], outs[1][:, :n_cols]) if ns else (outs[0], None)


def _rwkv_kernel(pr_ref, pk_ref, pv_ref, px_ref, sr_ref, sk_ref, sv_ref, sx_ref,
                 mur_ref, muk_ref, muv_ref, mux_ref, w0_ref, a0_ref, kk_ref, ka_ref, rk_ref,
                 gg_ref, gb_ref, w2_ref, a2_ref, g2_ref, s0_ref,
                 o_ref, st_ref,
                 s_scr, br_scr, bk_scr, bv_scr, bx_scr, *, c, g, nb, l_valid, n_chunks):
    ci = pl.program_id(2)
    half = RWKV_HEAD
    w = g * LANES
    gs = 1
    nh = 2 * gs
    n = nh * c

    @pl.when(ci == 0)
    def _():
        s_scr[...] = s0_ref[...]
        br_scr[:, 7:8, :] = sr_ref[...]
        bk_scr[:, 7:8, :] = sk_ref[...]
        bv_scr[:, 7:8, :] = sv_ref[...]
        bx_scr[:, 7:8, :] = sx_ref[...]

    ones_w = ((_iota((w, w), 0) // half) == (_iota((w, w), 1) // half)).astype(BF16)
    ones_p = ((_iota((LANES, LANES), 0) // half) == (_iota((LANES, LANES), 1) // half)).astype(F32)
    tri = (_iota((c, c), 1) <= _iota((c, c), 0)).astype(BF16)
    lane_c = _iota((1, LANES), 1)
    m2 = [(lane_c < half).astype(F32), (lane_c >= half).astype(F32)]
    ri = _iota((n, n), 0)
    cj = _iota((n, n), 1)
    in_blk = lambda i, b: jnp.logical_and(i >= b * c, i < (b + 1) * c)
    same = jnp.logical_and(in_blk(ri, 0), in_blk(cj, 0))
    for b in range(1, nh):
        same = jnp.logical_or(same, jnp.logical_and(in_blk(ri, b), in_blk(cj, b)))
    strict = jnp.logical_and(same, cj < ri)
    incl = jnp.logical_and(same, cj <= ri)
    eye = (ri == cj).astype(F32)
    n_doub = max(1, math.ceil(math.log2(c)))
    w2b = w2_ref[...].astype(BF16)
    a2b = a2_ref[...].astype(BF16)
    g2b = g2_ref[...].astype(BF16)
    pair = lambda t, p: t[:, p * LANES:(p + 1) * LANES]

    def seg_sum(t):
        return _dot_exact(_split(t, 3), ones_w)

    def sequence(bi):
        def shifted(p_ref, buf, mu_ref):
            p = p_ref[bi]
            buf[bi, 8:8 + c, :] = p
            prev = buf[bi, 7:7 + c, :]
            buf[bi, 7:8, :] = p[c - 1:c, :]
            return p + (prev - p) * mu_ref[...]

        r = shifted(pr_ref, br_scr, mur_ref)
        k = shifted(pk_ref, bk_scr, muk_ref)
        v = shifted(pv_ref, bv_scr, muv_ref)
        x = shifted(px_ref, bx_scr, mux_ref)
        yield

        wl = w0_ref[...] + _dot(jnp.tanh(x).astype(BF16), w2b)
        wl = -jax.nn.softplus(-wl) - 0.5
        logw = -jnp.exp(wl)
        a = jax.nn.sigmoid(a0_ref[...] + _dot(x.astype(BF16), a2b))
        gate = _dot(jax.nn.sigmoid(x).astype(BF16), g2b)
        kk = k * kk_ref[...]
        yield
        kk = kk / jnp.maximum(jnp.sqrt(seg_sum(kk * kk)), 1e-12)
        k = k * (1.0 + (a - 1.0) * ka_ref[...])
        be = kk * a
        if l_valid < c * n_chunks:
            ok = (_iota((c, w), 0) + ci * c) < l_valid
            logw = jnp.where(ok, logw, 0.0)
            kk = jnp.where(ok, kk, 0.0)
            be = jnp.where(ok, be, 0.0)
            k = jnp.where(ok, k, 0.0)
            v = jnp.where(ok, v, 0.0)

        yield
        cum = _dot_exact(_split(logw, 3), tri, lambda p, t: _dot(t, p))
        yield
        tot = cum[c - 1:c, :]
        ah = -kk * jnp.exp(cum - logw)
        rh = r * jnp.exp(cum)
        ieg = jnp.exp(-cum)
        bc = be * ieg
        kc = k * ieg
        etail = jnp.exp(tot - cum)
        bt = be * etail
        kt = k * etail

        o_parts = {}

        def system(grp):
            stack = lambda t: jnp.concatenate(
                [pair(t, p) * m2[q] for p in grp for q in range(2)], axis=0)
            ars = _split(jnp.concatenate([stack(ah), stack(rh)], axis=0))
            vs = stack(v)
            yield
            gb = _dot3(ars, _split(stack(bc)), _dot_nt)
            yield
            gk = _dot3(ars, _split(stack(kc)), _dot_nt)
            xm = jnp.where(strict, gb[:n], 0.0)
            tm = eye + xm
            xp = _split(xm)
            yield
            if n_doub > 1:
                xp = _split(_dot3(xp, xp))
                yield
                for _ in range(n_doub - 2):
                    ts = _split(tm)
                    both = _dot3([jnp.concatenate([xp[i], ts[i]], axis=0) for i in range(2)], xp)
                    tm = tm + both[n:]
                    xp = _split(both[:n])
                    yield
                tm = tm + _dot3(_split(tm), xp)
                yield
            lm = jnp.concatenate([jnp.where(strict, gk[:n], 0.0), jnp.where(incl, gk[n:], 0.0)], axis=0)
            lmv = _dot3(_split(lm), _split(vs))
            mrb = jnp.where(incl, gb[n:], 0.0)
            yield

            y0 = [_dot3(_split(jnp.concatenate([pair(ah, p), pair(rh, p)], axis=0)),
                        _split(s_scr[bi, p]), _dot_nt) for p in grp]
            yield
            ys = jnp.concatenate([y[:c] * m2[q] for y in y0 for q in range(2)], axis=0) + lmv[:n]
            ps = _dot3(_split(tm), _split(ys))
            yield
            os_ = _dot3(_split(mrb), _split(ps)) + lmv[n:]
            yield
            for i, p in enumerate(grp):
                lo, mid, hi = 2 * i * c, (2 * i + 1) * c, (2 * i + 2) * c
                o_parts[p] = y0[i][c:] + os_[lo:mid] + os_[mid:hi]
                pv = jnp.concatenate([ps[lo:mid] + ps[mid:hi], pair(v, p)], axis=0)
                bk = jnp.concatenate([pair(bt, p), pair(kt, p)], axis=0)
                upd = _dot3(_split(pv), _split(bk), _dot_tn)
                s_scr[bi, p] = (s_scr[bi, p] * jnp.exp(pair(tot, p)) + upd) * ones_p
                yield

        systems = [system(list(range(s, s + gs))) for s in range(0, g, gs)]
        while systems:
            systems = [s for s in systems if next(s, True) is None]
            yield

        o = jnp.concatenate([o_parts[p] for p in range(g)], axis=1) if g > 1 else o_parts[0]
        inv_n = 1.0 / half
        mean = seg_sum(o) * inv_n
        yield
        d = o - mean
        var = seg_sum(d * d) * inv_n
        yield
        on = d * lax.rsqrt(var + GN_EPS) * gg_ref[...] + gb_ref[...]
        bonus = seg_sum(r * k * rk_ref[...]) * v
        o_ref[bi] = ((on + bonus) * gate).astype(o_ref.dtype)

    live = [sequence(bi) for bi in range(nb)]
    while live:
        live = [s for s in live if next(s, True) is None]

    @pl.when(ci == n_chunks - 1)
    def _():
        st_ref[...] = s_scr[...]


def _rwkv(proj3, shift0, s0, mu, w0, w2, a0, a2, g2, k_k, k_a, r_k, gn_g, gn_b, *, c, l_valid,
          l_out, o_width):
    b, lp, _ = proj3.shape
    cw = w0.shape[-1]
    npair = cw // LANES
    heads = cw // RWKV_HEAD
    g = 2 if npair % 2 == 0 else 1
    w = g * LANES
    ng = npair // g
    lw = mu.shape[-1] - 3 * cw
    assert lw % LANES == 0 and (3 * cw) % lw == 0 and lp % c == 0
    n_chunks = lp // c
    wl_, al_ = w2.shape[0], a2.shape[0]
    w2p = jnp.zeros((lw, cw), F32).at[:wl_].set(w2)
    a2p = jnp.zeros((lw, cw), F32).at[wl_:wl_ + al_].set(a2)
    g2p = jnp.zeros((lw, cw), F32).at[wl_ + al_:].set(g2)
    s0p = s0.reshape(b, npair, 2, RWKV_HEAD, RWKV_HEAD)
    z = jnp.zeros_like(s0p[:, :, 0])
    s0bd = jnp.concatenate([jnp.concatenate([s0p[:, :, 0], z], axis=-1),
                            jnp.concatenate([z, s0p[:, :, 1]], axis=-1)], axis=-2)
    sh3 = shift0.reshape(b, 1, -1)
    mu2 = mu.reshape(1, -1)
    row = lambda t: t.reshape(1, cw)
    xoff = (3 * cw) // lw

    nb = _largest_divisor(b, 4, 1)
    seg = lambda s: pl.BlockSpec((nb, c, w), lambda i, j, t: (i, t, s * ng + j))
    sseg = lambda s: pl.BlockSpec((nb, 1, w), lambda i, j, t: (i, 0, s * ng + j))
    mseg = lambda s: pl.BlockSpec((1, w), lambda i, j, t: (0, s * ng + j))
    vec = pl.BlockSpec((1, w), lambda i, j, t: (0, j))
    lora = pl.BlockSpec((lw, w), lambda i, j, t: (0, j))
    state = pl.BlockSpec((nb, g, LANES, LANES), lambda i, j, t: (i, j, 0, 0))
    in_specs = [seg(0), seg(1), seg(2), pl.BlockSpec((nb, c, lw), lambda i, j, t: (i, t, xoff)),
                sseg(0), sseg(1), sseg(2), pl.BlockSpec((nb, 1, lw), lambda i, j, t: (i, 0, xoff)),
                mseg(0), mseg(1), mseg(2), pl.BlockSpec((1, lw), lambda i, j, t: (0, xoff)),
                vec, vec, vec, vec, vec, vec, vec, lora, lora, lora, state]
    o, st = pl.pallas_call(
        functools.partial(_rwkv_kernel, c=c, g=g, nb=nb, l_valid=l_valid, n_chunks=n_chunks),
        grid=(b // nb, ng, n_chunks),
        in_specs=in_specs,
        out_specs=[pl.BlockSpec((nb, c, w), lambda i, j, t: (i, t, j)), state],
        out_shape=[jax.ShapeDtypeStruct((b, l_out, o_width), BF16),
                   jax.ShapeDtypeStruct((b, npair, LANES, LANES), F32)],
        scratch_shapes=[pltpu.VMEM((nb, g, LANES, LANES), F32),
                        pltpu.VMEM((nb, c + 8, w), F32), pltpu.VMEM((nb, c + 8, w), F32),
                        pltpu.VMEM((nb, c + 8, w), F32), pltpu.VMEM((nb, c + 8, lw), F32)],
        compiler_params=_params("parallel", "parallel", "arbitrary"),
        name="rwkv7_chunk",
    )(proj3, proj3, proj3, proj3, sh3, sh3, sh3, sh3, mu2, mu2, mu2, mu2,
      row(w0), row(a0), row(k_k), row(k_a), row(r_k), row(gn_g), row(gn_b), w2p, a2p, g2p, s0bd)
    st = jnp.stack([st[:, :, :RWKV_HEAD, :RWKV_HEAD], st[:, :, RWKV_HEAD:, RWKV_HEAD:]], axis=2)
    return o, st.reshape(b, heads, RWKV_HEAD, RWKV_HEAD)


def _logf_kernel(z_ref, bf_ref, lf_ref, cc_ref, cr_ref, *, rb):
    l, h = z_ref.shape[1], z_ref.shape[2]
    ti = _iota((rb, rb), 0)
    si = _iota((rb, rb), 1)
    tri = (si <= ti).astype(F32)
    carry = jnp.zeros((1, h), F32)
    for i in range(l // rb):
        sl = slice(i * rb, (i + 1) * rb)
        lf = jax.nn.log_sigmoid(z_ref[0, sl, :] + bf_ref[...])
        lf_ref[0, sl, :] = lf
        cblk = carry + _dot(tri, lf, HI)
        cc_ref[0, sl, :] = cblk
        carry = cblk[rb - 1:rb, :]
    eye = (_iota((h, h), 0) == _iota((h, h), 1)).astype(F32)
    cr_ref[0] = _dot_nt(eye, cc_ref[0], HI)


def _logf(z3, b_forget):
    b, l, h = z3.shape
    rb = _largest_divisor(l, 512, 8)
    blk = pl.BlockSpec((1, l, h), lambda i: (i, 0, 0))
    return pl.pallas_call(
        functools.partial(_logf_kernel, rb=rb),
        grid=(b,),
        in_specs=[blk, pl.BlockSpec((1, h), lambda i: (0, 0))],
        out_specs=[blk, blk, pl.BlockSpec((1, h, l), lambda i: (i, 0, 0))],
        out_shape=[jax.ShapeDtypeStruct((b, l, h), F32), jax.ShapeDtypeStruct((b, l, h), F32),
                   jax.ShapeDtypeStruct((b, h, l), F32)],
        compiler_params=_params("parallel"),
        name="fox_logf_cumsum",
    )(z3, b_forget.reshape(1, h))


def _fox_prompt_kernel(q_ref, k_ref, v_ref, cc_ref, cr_ref, o_ref, *, tq):
    l = q_ref.shape[1]
    nh = cc_ref.shape[2]
    h = pl.program_id(1)
    scale = FOX_HEAD ** -0.5
    sel = (_iota((nh, LANES), 0) == h).astype(F32)
    cq_all = _dot(cc_ref[0], sel, HI)
    ck = cr_ref[0, pl.ds(h, 1), :]
    kb = k_ref[0].astype(BF16)
    vb = v_ref[0].astype(BF16)
    for i in range(l // tq):
        nk = (i + 1) * tq
        rows = slice(i * tq, nk)
        s = _dot_nt(q_ref[0, rows, :].astype(BF16), kb[:nk]) * scale
        s = s + cq_all[rows, 0:1] - ck[:, :nk]
        qpos = _iota((tq, nk), 0) + i * tq
        kpos = _iota((tq, nk), 1)
        s = jnp.where(qpos >= kpos, s, NEG_INF)
        m = jnp.max(s, axis=-1, keepdims=True)
        p = jnp.exp(s - m)
        den = jnp.sum(p, axis=-1, keepdims=True)
        o = _dot(p.astype(BF16), vb[:nk]) / den
        o_ref[0, rows, :] = o.astype(o_ref.dtype)


def _fox_prompt(proj3, q_off, k3, v3, c_col, c_row):
    b, l, fw = k3.shape
    nh = c_col.shape[2]
    tq = _largest_divisor(l, 384, 8)
    blk = lambda off: pl.BlockSpec((1, l, LANES), lambda i, j: (i, 0, off + j))
    return pl.pallas_call(
        functools.partial(_fox_prompt_kernel, tq=tq),
        grid=(b, nh),
        in_specs=[blk(q_off), blk(0), blk(0),
                  pl.BlockSpec((1, l, nh), lambda i, j: (i, 0, 0)),
                  pl.BlockSpec((1, nh, l), lambda i, j: (i, 0, 0))],
        out_specs=blk(0),
        out_shape=jax.ShapeDtypeStruct((b, l, fw), BF16),
        compiler_params=_params("parallel", "parallel"),
        name="fox_prompt_attention",
    )(proj3, k3, v3, c_col, c_row)


def _page_tail_kernel(lf_ref, tail_ref, tot_ref):
    pp = lf_ref.shape[0]
    later = (_iota((PAGE_SIZE, PAGE_SIZE), 1) > _iota((PAGE_SIZE, PAGE_SIZE), 0)).astype(BF16)
    for i in range(pp):
        lf = lf_ref[i]
        tail_ref[i] = _dot_exact(_split(lf, 3), later, lambda p, u: _dot(u, p))
        tot_ref[i] = jnp.broadcast_to(jnp.sum(lf, axis=0, keepdims=True), lf.shape)


def _page_tails(lf_pool):
    n_phys, _, nh = lf_pool.shape
    pp = _largest_divisor(n_phys, 32, 1)
    blk = pl.BlockSpec((pp, PAGE_SIZE, nh), lambda i: (i, 0, 0))
    shape = jax.ShapeDtypeStruct(lf_pool.shape, F32)
    return pl.pallas_call(
        _page_tail_kernel,
        grid=(n_phys // pp,),
        in_specs=[blk],
        out_specs=[blk, blk],
        out_shape=[shape, shape],
        compiler_params=_params("parallel"),
        name="fox_page_tails",
    )(lf_pool)


def _fox_sample_kernel(pt_ref, q_ref, kn_ref, vn_ref, cq_ref, ck_ref, *rest, nh, n_steps, pg):
    kp = rest[:pg]
    vp = rest[pg:2 * pg]
    tl = rest[2 * pg:3 * pg]
    tt = rest[3 * pg:4 * pg]
    o_ref, m_scr, l_scr, acc_scr, suf_scr = rest[4 * pg:]
    pi = pl.program_id(1)
    rows = q_ref.shape[1]
    cols = PAGE_SIZE * nh
    scale = FOX_HEAD ** -0.5

    @pl.when(pi == 0)
    def _():
        m_scr[...] = jnp.full(m_scr.shape, NEG_INF, F32)
        l_scr[...] = jnp.zeros(l_scr.shape, F32)
        acc_scr[...] = jnp.zeros(acc_scr.shape, F32)
        suf_scr[...] = jnp.zeros(suf_scr.shape, F32)

    qb = q_ref[0].astype(BF16)
    cnew = cq_ref[0]

    def online(scores, vals):
        m_old = m_scr[...]
        m_new = m_old
        for s in scores:
            m_new = jnp.maximum(m_new, jnp.max(s, axis=-1, keepdims=True))
        corr = jnp.exp(m_old - m_new)
        den = l_scr[...] * corr
        acc = acc_scr[...] * corr
        for s, val in zip(scores, vals):
            pe = jnp.exp(s - m_new)
            den = den + jnp.sum(pe, axis=-1, keepdims=True)
            acc = acc + _dot(pe.astype(BF16), val)
        l_scr[...] = den
        acc_scr[...] = acc
        m_scr[...] = m_new

    own_head = (_iota((rows, cols), 0) % nh) == (_iota((rows, cols), 1) % nh)
    suf = suf_scr[...]
    bias = [None] * pg
    for gi in reversed(range(pg)):
        bias[gi] = suf + tl[gi][0]
        suf = suf + tt[gi][0]
    suf_scr[...] = suf
    scores = []
    for gi in range(pg):
        kflat = kp[gi][0].reshape(cols, FOX_HEAD).astype(BF16)
        s = _dot_nt(qb, kflat) * scale + bias[gi] + cnew
        scores.append(jnp.where(own_head, s, NEG_INF))
    online(scores, [r[0].reshape(cols, FOX_HEAD).astype(BF16) for r in vp])

    @pl.when(pi == n_steps - 1)
    def _():
        sn = _dot_nt(qb, kn_ref[0].astype(BF16)) * scale + cnew - ck_ref[0]
        ri = _iota((rows, rows), 0)
        cj = _iota((rows, rows), 1)
        ok = jnp.logical_and((ri % nh) == (cj % nh), (cj // nh) <= (ri // nh))
        online([jnp.where(ok, sn, NEG_INF)], [vn_ref[0].astype(BF16)])
        o_ref[0] = (acc_scr[...] / l_scr[...]).astype(o_ref.dtype)


def _fox_sample(q, k_new, v_new, c_col, k_pool, v_pool, lf_pool, page_table):
    bd, t_new, width = q.shape
    nh = c_col.shape[2]
    n_pages = page_table.shape[1]
    n_phys = k_pool.shape[0]
    pg = _largest_divisor(n_pages, 8, 1)
    n_steps = n_pages // pg
    rows = t_new * nh
    cols = PAGE_SIZE * nh
    tail, tot = _page_tails(lf_pool)
    tail = tail.reshape(n_phys, 1, cols)
    tot = tot.reshape(n_phys, 1, cols)
    by_head = lambda a: a.reshape(bd, rows, FOX_HEAD)
    full = lambda shape: pl.BlockSpec((1,) + shape, lambda i, p, pt: (i, 0, 0))

    def page(shape, slot):
        zeros = (0,) * len(shape)
        return pl.BlockSpec((1,) + shape,
                            lambda i, p, pt: (pt[i, n_pages - (p + 1) * pg + slot],) + zeros)

    slots = range(pg)
    kv_page = (PAGE_SIZE, nh, FOX_HEAD)
    grid_spec = pltpu.PrefetchScalarGridSpec(
        num_scalar_prefetch=1,
        grid=(bd, n_steps),
        in_specs=([full((rows, FOX_HEAD)), full((rows, FOX_HEAD)), full((rows, FOX_HEAD)),
                   full((rows, 1)), full((1, rows))]
                  + [page(kv_page, s) for s in slots] + [page(kv_page, s) for s in slots]
                  + [page((1, cols), s) for s in slots] + [page((1, cols), s) for s in slots]),
        out_specs=full((rows, FOX_HEAD)),
        scratch_shapes=[pltpu.VMEM((rows, 1), F32), pltpu.VMEM((rows, 1), F32),
                        pltpu.VMEM((rows, FOX_HEAD), F32), pltpu.VMEM((1, cols), F32)],
    )
    out = pl.pallas_call(
        functools.partial(_fox_sample_kernel, nh=nh, n_steps=n_steps, pg=pg),
        grid_spec=grid_spec,
        out_shape=jax.ShapeDtypeStruct((bd, rows, FOX_HEAD), BF16),
        compiler_params=_params("parallel", "arbitrary"),
        name="fox_sample_attention",
    )(page_table, by_head(q), by_head(k_new), by_head(v_new),
      c_col.reshape(bd, rows, 1), c_col.reshape(bd, 1, rows),
      *([k_pool] * pg), *([v_pool] * pg), *([tail] * pg), *([tot] * pg))
    return out.reshape(bd, t_new, width)


def _conv_kernel(bg_ref, cg_ref, h_ref, w_ref, buf_ref, y_ref, nb_ref, u_scr, *, tt, n_t):
    ti = pl.program_id(2)

    @pl.when(ti == 0)
    def _():
        u_scr[6:8, :] = buf_ref[0]

    u = cg_ref[0] * h_ref[0]
    u_scr[8:8 + tt, :] = u
    y = (w_ref[0:1, :] * u_scr[6:6 + tt, :] + w_ref[1:2, :] * u_scr[7:7 + tt, :]
         + w_ref[2:3, :] * u)
    y_ref[0] = (bg_ref[0] * y).astype(y_ref.dtype)
    tail = u_scr[6 + tt:8 + tt, :]
    u_scr[6:8, :] = tail

    @pl.when(ti == n_t - 1)
    def _():
        nb_ref[0] = tail


def _short_conv(proj3, buf0, conv_w):
    b, l, d3 = proj3.shape
    d = d3 // 3
    tc = _largest_divisor(d, 512, LANES)
    tt = _largest_divisor(l, 704, 8)
    n_t = l // tt
    nc = d // tc
    seg = lambda s: pl.BlockSpec((1, tt, tc), lambda i, j, t: (i, t, s * nc + j))
    return pl.pallas_call(
        functools.partial(_conv_kernel, tt=tt, n_t=n_t),
        grid=(b, nc, n_t),
        in_specs=[seg(0), seg(1), seg(2),
                  pl.BlockSpec((conv_w.shape[0], tc), lambda i, j, t: (0, j)),
                  pl.BlockSpec((1, 2, tc), lambda i, j, t: (i, 0, j))],
        out_specs=[pl.BlockSpec((1, tt, tc), lambda i, j, t: (i, t, j)),
                   pl.BlockSpec((1, 2, tc), lambda i, j, t: (i, 0, j))],
        out_shape=[jax.ShapeDtypeStruct((b, l, d), BF16), jax.ShapeDtypeStruct((b, 2, d), F32)],
        scratch_shapes=[pltpu.VMEM((tt + 8, tc), F32)],
        compiler_params=_params("parallel", "parallel", "arbitrary"),
        name="short_conv",
    )(proj3, proj3, proj3, conv_w, buf0)


def _trunk(hp3, hs3, s0, shift0, conv0, k_pool, v_pool, lf_pool, page_table, wts):
    (norm_mix, norm_mlp, w_in_even, b_forget, rwkv_mu, rwkv_w0, rwkv_w2, rwkv_a0, rwkv_a2, rwkv_g2,
     rwkv_k_k, rwkv_k_a, rwkv_r_k, rwkv_gn_g, rwkv_gn_b, w_out_even, w_in_odd, conv_w, w_out_odd,
     w_up, w_down) = wts
    bp, l, d = hp3.shape
    bs, t, _ = hs3.shape
    hp = hp3.reshape(bp * l, d)
    hs = hs3.reshape(bs * t, d)
    cw = rwkv_w0.shape[-1]
    rproj = rwkv_mu.shape[-1]
    fw = d - cw
    nh = fw // FOX_HEAD
    heads = cw // RWKV_HEAD
    rw = (rwkv_mu[0], rwkv_w0[0], rwkv_w2[0], rwkv_a0[0], rwkv_a2[0], rwkv_g2[0], rwkv_k_k[0],
          rwkv_k_a[0], rwkv_r_k[0], rwkv_gn_g[0], rwkv_gn_b[0])
    norm = lambda a, g: _rmsnorm(a, g, BF16)
    wide = functools.partial(_matmul, tn=512, tk=4096)
    narrow = functools.partial(_matmul, tn=256, tk=4096)

    def down(up, us, layer, hp, hs):
        n_slabs = max(1, w_down.shape[1] // d)
        for slab in range(n_slabs):
            hp, hs = wide(up, us, w_down, layer, epilogue="resid", resid=(hp, hs),
                          k_slab=(slab, n_slabs))
        return hp, hs

    lpad = RWKV_CHUNK * pl.cdiv(l, RWKV_CHUNK)
    if lpad > l:
        hnp, hnp_pad = _rmsnorm_pad(hp3, norm_mix[0], lpad)
        hnp = hnp.reshape(bp * l, d)
        hnp_pad = hnp_pad.reshape(bp * lpad, d)
    else:
        hnp = hnp_pad = norm(hp, norm_mix[0])
    hns = norm(hs, norm_mix[0])
    pq_p, pq_s = narrow(hnp_pad, hns, w_in_even, 0, n_cols=rproj + fw)
    k_p, k_s = narrow(hnp, hns, w_in_even, 0, n_lo=rproj + fw, n_cols=fw)
    v_p, v_s = narrow(hnp, hns, w_in_even, 0, n_lo=rproj + 2 * fw, n_cols=fw)
    w_lf = w_in_even[:, :, rproj + 3 * fw:]
    z_p, _ = _matmul(hnp, None, w_lf, 0, tn=nh, tk=4096)
    z_s, _ = _matmul(hns, None, w_lf, 0, tn=nh, tk=4096)
    pq_p = pq_p.reshape(bp, lpad, rproj + fw)
    pq_s = pq_s.reshape(bs, t, rproj + fw)
    k_p, v_p = k_p.reshape(bp, l, fw), v_p.reshape(bp, l, fw)
    k_s, v_s = k_s.reshape(bs, t, fw), v_s.reshape(bs, t, fw)
    lf_p, cc_p, cr_p = _logf(z_p.reshape(bp, l, nh), b_forget[0])
    lf_s, cc_s, _ = _logf(z_s.reshape(bs, t, nh), b_forget[0])

    zero_s = jnp.zeros((bp, heads, RWKV_HEAD, RWKV_HEAD), F32)
    or_p, st_p = _rwkv(pq_p, jnp.zeros((bp, rproj), F32), zero_s, *rw, c=RWKV_CHUNK, l_valid=l,
                       l_out=l, o_width=cw)
    of_p = _fox_prompt(pq_p, rproj // LANES, k_p, v_p, cc_p, cr_p)
    tpad = 8 * pl.cdiv(t, 8)
    pq_s_pad = jnp.pad(pq_s[:, :, :rproj], ((0, 0), (0, tpad - t), (0, 0)))
    or_s, st_s = _rwkv(pq_s_pad, shift0[0], s0[0], *rw, c=tpad, l_valid=t, l_out=tpad, o_width=cw)
    of_s = _fox_sample(pq_s[:, :, rproj:], k_s, v_s, cc_s, k_pool[0], v_pool[0], lf_pool[0],
                       page_table)
    mix_p = (or_p.reshape(bp * l, cw), of_p.reshape(bp * l, fw))
    mix_s = (or_s[:, :t].reshape(bs * t, cw), of_s.reshape(bs * t, fw))
    hp, hs = wide(mix_p, mix_s, w_out_even, 0, epilogue="resid", resid=(hp, hs))
    up, us = wide(norm(hp, norm_mlp[0]), norm(hs, norm_mlp[0]), w_up, 0, epilogue="relu2",
                  out_dtype=BF16)
    hp, hs = down(up, us, 0, hp, hs)

    p1_p, p1_s = wide(norm(hp, norm_mix[1]), norm(hs, norm_mix[1]), w_in_odd, 0)
    y_p, buf_p = _short_conv(p1_p.reshape(bp, l, 3 * d), jnp.zeros((bp,) + conv0.shape[2:], F32),
                             conv_w[0])
    y_s, buf_s = _short_conv(p1_s.reshape(bs, t, 3 * d), conv0[0], conv_w[0])
    hp, hs = wide(y_p.reshape(bp * l, d), y_s.reshape(bs * t, d), w_out_odd, 0, epilogue="resid",
                  resid=(hp, hs))
    up, us = wide(norm(hp, norm_mlp[1]), norm(hs, norm_mlp[1]), w_up, 1, epilogue="relu2",
                  out_dtype=BF16)
    hp, hs = down(up, us, 1, hp, hs)

    head4 = lambda a, b, n: a.reshape(b, n, nh, FOX_HEAD)[None]
    outs_p = (st_p[None], pq_p[:, l - 1, :rproj][None], head4(k_p, bp, l), head4(v_p, bp, l),
              lf_p[None], buf_p[None])
    outs_s = (st_s[None], pq_s[:, t - 1, :rproj][None], head4(k_s, bs, t), head4(v_s, bs, t),
              lf_s[None], buf_s[None])
    return hp.reshape(bp, l, d), hs, outs_p, outs_s


def kernel(x_prompt, x_sample, state_rwkv, state_rwkv_shift, cache_fox_k, cache_fox_v, cache_fox_logf, state_conv, page_table, meta_tokens, norm_mix, norm_mlp, norm_final, w_in_even, b_forget, rwkv_mu, rwkv_w0, rwkv_w2, rwkv_a0, rwkv_a2, rwkv_g2, rwkv_k_k, rwkv_k_a, rwkv_r_k, rwkv_gn_g, rwkv_gn_b, w_out_even, w_in_odd, conv_w, w_out_odd, w_up, w_down):
    wts = (norm_mix, norm_mlp, w_in_even, b_forget, rwkv_mu, rwkv_w0, rwkv_w2, rwkv_a0, rwkv_a2,
           rwkv_g2, rwkv_k_k, rwkv_k_a, rwkv_r_k, rwkv_gn_g, rwkv_gn_b, w_out_even, w_in_odd, conv_w,
           w_out_odd, w_up, w_down)
    bp, _, d = x_prompt.shape
    bs, ts, _ = x_sample.shape
    h0 = jnp.concatenate([jnp.broadcast_to(meta_tokens[None], (bp, N_META, d)), x_prompt], axis=1)
    hp, hs, outs_p, outs_s = _trunk(h0, x_sample, state_rwkv, state_rwkv_shift, state_conv,
                                    cache_fox_k, cache_fox_v, cache_fox_logf, page_table, wts)
    y_prompt = _final_norm_prompt(hp, norm_final)
    y_sample = _rmsnorm(hs, norm_final, F32).reshape(bs, ts, d)
    return (y_prompt, y_sample, *outs_p, *outs_s)
```

```python
import functools
import math

import jax
import jax.numpy as jnp
from jax import lax
from jax.experimental import pallas as pl
from jax.experimental.pallas import tpu as pltpu

F32 = jnp.float32
BF16 = jnp.bfloat16
HI = lax.Precision.HIGHEST

RMS_EPS = 1e-6
GN_EPS = 64e-5
NEG_INF = -1e30
N_META = 16
RWKV_HEAD = 64
FOX_HEAD = 128
PAGE_SIZE = 128
LANES = 128
VMEM_LIMIT = 62 * 1024 * 1024
BF16_ROWS = 16
RWKV_CHUNK = 48


def _largest_divisor(n, cap, mult):
    best = None
    for d in range(mult, min(n, cap) + 1, mult):
        if n % d == 0:
            best = d
    return n if best is None else best


def _params(*sem, flags=None):
    return pltpu.CompilerParams(dimension_semantics=sem, vmem_limit_bytes=VMEM_LIMIT, flags=flags)


def _dot(a, b, precision=None):
    return jnp.dot(a, b, precision=precision, preferred_element_type=F32)


def _dot_nt(a, b, precision=None):
    return lax.dot_general(a, b, (((1,), (1,)), ((), ())), precision=precision,
                           preferred_element_type=F32)


def _dot_tn(a, b, precision=None):
    return lax.dot_general(a, b, (((0,), (0,)), ((), ())), precision=precision,
                           preferred_element_type=F32)


def _iota(shape, dim):
    return lax.broadcasted_iota(jnp.int32, shape, dim)


def _split(x, pieces=2):
    out = []
    for i in range(pieces):
        p = x.astype(BF16)
        out.append(p)
        if i + 1 < pieces:
            x = x - p.astype(F32)
    return out


def _dot3(a, b, f=_dot):
    axis = 1 if f is _dot_tn else 0
    m = a[0].shape[axis]
    if m % BF16_ROWS:
        return f(a[0], b[0]) + (f(a[0], b[1]) + f(a[1], b[0]))
    t = f(jnp.concatenate([a[0], a[1]], axis=axis), b[0])
    return (t[:m] + t[m:]) + f(a[0], b[1])


def _dot_exact(pieces, other, f=_dot):
    m = pieces[0].shape[0]
    if f is _dot and m % BF16_ROWS == 0:
        t = f(jnp.concatenate(pieces, axis=0), other)
        return sum(t[i * m:(i + 1) * m] for i in range(1, len(pieces))) + t[:m]
    acc = f(pieces[0], other)
    for p in pieces[1:]:
        acc = acc + f(p, other)
    return acc


def _rmsnorm_kernel(x_ref, g_ref, o_ref):
    x = x_ref[...]
    ms = jnp.mean(x * x, axis=-1, keepdims=True)
    o_ref[...] = (x * lax.rsqrt(ms + RMS_EPS) * g_ref[...]).astype(o_ref.dtype)


def _rmsnorm(x2d, g, out_dtype):
    m, d = x2d.shape
    tr = _largest_divisor(m, 512, 16)
    return pl.pallas_call(
        _rmsnorm_kernel,
        grid=(m // tr,),
        in_specs=[pl.BlockSpec((tr, d), lambda i: (i, 0)),
                  pl.BlockSpec((1, d), lambda i: (0, 0))],
        out_specs=pl.BlockSpec((tr, d), lambda i: (i, 0)),
        out_shape=jax.ShapeDtypeStruct((m, d), out_dtype),
        compiler_params=_params("parallel"),
        name="rmsnorm",
    )(x2d, g.reshape(1, d))


def _rmsnorm_pad_kernel(x_ref, g_ref, o_ref, op_ref, *, n_real):
    x = x_ref[0]
    ms = jnp.mean(x * x, axis=-1, keepdims=True)
    y = (x * lax.rsqrt(ms + RMS_EPS) * g_ref[...]).astype(o_ref.dtype)
    o_ref[0] = y
    op_ref[0] = jnp.where(pl.program_id(1) < n_real, y, jnp.zeros_like(y))


def _rmsnorm_pad(h3, g, lp):
    b, l, d = h3.shape
    tr = _largest_divisor(math.gcd(l, lp), 512, BF16_ROWS)
    n_real = l // tr
    real = pl.BlockSpec((1, tr, d), lambda i, j: (i, jnp.minimum(j, n_real - 1), 0))
    return pl.pallas_call(
        functools.partial(_rmsnorm_pad_kernel, n_real=n_real),
        grid=(b, lp // tr),
        in_specs=[real, pl.BlockSpec((1, d), lambda i, j: (0, 0))],
        out_specs=[real, pl.BlockSpec((1, tr, d), lambda i, j: (i, j, 0))],
        out_shape=[jax.ShapeDtypeStruct((b, l, d), BF16), jax.ShapeDtypeStruct((b, lp, d), BF16)],
        compiler_params=_params("parallel", "arbitrary"),
        name="rmsnorm_pad",
    )(h3, g.reshape(1, d))


def _final_norm_prompt_kernel(a_ref, b_ref, g_ref, o_ref):
    x = jnp.concatenate([a_ref[0, N_META:, :], b_ref[0]], axis=0)
    ms = jnp.mean(x * x, axis=-1, keepdims=True)
    o_ref[0] = x * lax.rsqrt(ms + RMS_EPS) * g_ref[...]


def _final_norm_prompt(h3, g):
    b, l, d = h3.shape
    s = l - N_META
    tr = _largest_divisor(s, 256, N_META)
    sub = tr // N_META
    return pl.pallas_call(
        _final_norm_prompt_kernel,
        grid=(b, s // tr),
        in_specs=[pl.BlockSpec((1, tr, d), lambda i, j: (i, j, 0)),
                  pl.BlockSpec((1, N_META, d), lambda i, j: (i, (j + 1) * sub, 0)),
                  pl.BlockSpec((1, d), lambda i, j: (0, 0))],
        out_specs=pl.BlockSpec((1, tr, d), lambda i, j: (i, j, 0)),
        out_shape=jax.ShapeDtypeStruct((b, s, d), F32),
        compiler_params=_params("parallel", "parallel"),
        name="final_norm_prompt",
    )(h3, h3, g.reshape(1, d))


def _mm_kernel(*refs, nk, nx, epilogue, ns, spare, w_nk):
    refs = list(refs)
    take = lambda cnt: [refs.pop(0) for _ in range(cnt)]
    xp, xs, (w_ref,) = take(nx), take(nx * ns), take(1)
    rp, rs = (take(1 + ns) + [None])[:2] if epilogue == "resid" else (None, None)
    op, osm = (take(1 + ns) + [None])[:2]
    accp, accs = (take(1 + ns) + [None])[:2] if refs else (None, None)
    i = pl.program_id(0)
    k = pl.program_id(2)
    wb = w_ref[...].astype(BF16)
    mm = _dot_nt if w_nk else _dot

    def run(x_refs, r_ref, o_ref, acc_ref):
        def finish(acc):
            if epilogue == "relu2":
                acc = jnp.square(jnp.maximum(acc, 0.0))
            elif epilogue == "resid":
                acc = r_ref[...] + acc
            o_ref[...] = acc.astype(o_ref.dtype)

        if nk == 1:
            finish(mm(x_refs[0][...], wb))
        elif epilogue == "resid" and nx == 2:
            @pl.when(k == 0)
            def _():
                o_ref[...] = r_ref[...] + mm(x_refs[0][...], wb)

            @pl.when(k == 1)
            def _():
                o_ref[...] += mm(x_refs[1][...], wb)
        elif epilogue == "resid":
            part = mm(x_refs[0][...], wb)

            @pl.when(k == 0)
            def _():
                o_ref[...] = r_ref[...] + part

            @pl.when(k > 0)
            def _():
                o_ref[...] += part
        else:
            assert nx == 1
            part = mm(x_refs[0][...], wb)

            @pl.when(k == 0)
            def _():
                acc_ref[...] = part

            @pl.when(jnp.logical_and(k > 0, k < nk - 1))
            def _():
                acc_ref[...] += part

            @pl.when(k == nk - 1)
            def _():
                finish(acc_ref[...] + part)

    run(xp, rp, op, accp)
    if ns:
        @pl.when(i == 0)
        def _():
            run(xs, rs, osm, accs)

        if spare:
            @pl.when(i > 0)
            def _():
                osm[...] = jnp.zeros(osm.shape, osm.dtype)


def _matmul(xp, xs, w, layer, *, n_lo=0, n_cols=None, tn, tk, epilogue="plain", resid=None,
            out_dtype=F32, k_slab=(0, 1), w_nk=False):
    xp = xp if isinstance(xp, tuple) else (xp,)
    ns = 0 if xs is None else 1
    xs = () if xs is None else (xs if isinstance(xs, tuple) else (xs,))
    nx = len(xp)
    m, kdim = xp[0].shape[0], nx * xp[0].shape[1]
    ms = xs[0].shape[0] if ns else 0
    n_cols = w.shape[1 if w_nk else 2] - n_lo if n_cols is None else n_cols
    tn = _largest_divisor(math.gcd(n_cols, n_lo), tn, LANES)
    slab, n_slabs = k_slab
    kdim //= n_slabs
    tk = kdim // 2 if nx == 2 else _largest_divisor(kdim, tk, LANES)
    assert n_lo % tn == 0 and n_cols % tn == 0 and (n_slabs == 1 or (tk == kdim and nx == 1))
    tm = _largest_divisor(m, 2112, BF16_ROWS)
    nk = kdim // tk
    nj = n_cols // tn
    off = n_lo // tn
    kblk = (lambda k: k + slab) if nx == 1 else (lambda k: 0)
    once = pl.Buffered(1) if nk == 1 or nx == 2 else None
    in_specs = ([pl.BlockSpec((tm, tk), lambda i, j, k: (i, kblk(k)), pipeline_mode=once)] * nx
                + [pl.BlockSpec((ms, tk), lambda i, j, k: (0, kblk(k)))] * (nx * ns)
                + [pl.BlockSpec((None, tn, tk), lambda i, j, k: (layer, j + off, k + slab)) if w_nk
                   else pl.BlockSpec((None, tk, tn), lambda i, j, k: (layer, k + slab, j + off))])
    args = [*xp, *xs, w]
    out_specs = [pl.BlockSpec((tm, tn), lambda i, j, k: (i, j))]
    out_shape = [jax.ShapeDtypeStruct((m, n_cols), out_dtype)]
    use_acc = nk > 1 and epilogue != "resid"
    acc = [pltpu.VMEM((tm, tn), F32)] if use_acc else []
    if epilogue == "resid":
        in_specs.append(pl.BlockSpec((tm, tn), lambda i, j, k: (i, j)))
        args.append(resid[0])
    spare = m // tm > 1
    if ns:
        if epilogue == "resid":
            in_specs.append(pl.BlockSpec((ms, tn), lambda i, j, k: (0, jnp.where(i == 0, j, 0))))
            args.append(resid[1])
        out_specs.append(pl.BlockSpec((ms, tn), lambda i, j, k: (0, jnp.where(i == 0, j, nj))))
        out_shape.append(jax.ShapeDtypeStruct((ms, n_cols + tn * spare), out_dtype))
        acc += [pltpu.VMEM((ms, tn), F32)] if use_acc else []
    outs = pl.pallas_call(
        functools.partial(_mm_kernel, nk=nk, nx=nx, epilogue=epilogue, ns=ns, spare=spare, w_nk=w_nk),
        grid=(m // tm, nj, nk),
        in_specs=in_specs,
        out_specs=out_specs,
        out_shape=out_shape,
        scratch_shapes=acc,
        compiler_params=_params("arbitrary", "arbitrary", "arbitrary"),
        name="matmul_" + epilogue,
    )(*args)
    return (outs[0], outs[1][:, :n_cols]) if ns else (outs[0], None)


def _rwkv_kernel(pr_ref, pk_ref, pv_ref, px_ref, sr_ref, sk_ref, sv_ref, sx_ref,
                 mur_ref, muk_ref, muv_ref, mux_ref, w0_ref, a0_ref, kk_ref, ka_ref, rk_ref,
                 gg_ref, gb_ref, w2_ref, a2_ref, g2_ref, s0_ref,
                 o_ref, st_ref,
                 s_scr, br_scr, bk_scr, bv_scr, bx_scr, *, c, g, nb, l_valid, n_chunks):
    ci = pl.program_id(2)
    half = RWKV_HEAD
    w = g * LANES
    gs = 1
    nh = 2 * gs
    n = nh * c

    @pl.when(ci == 0)
    def _():
        s_scr[...] = s0_ref[...]
        br_scr[:, 7:8, :] = sr_ref[...]
        bk_scr[:, 7:8, :] = sk_ref[...]
        bv_scr[:, 7:8, :] = sv_ref[...]
        bx_scr[:, 7:8, :] = sx_ref[...]

    ones_w = ((_iota((w, w), 0) // half) == (_iota((w, w), 1) // half)).astype(BF16)
    ones_p = ((_iota((LANES, LANES), 0) // half) == (_iota((LANES, LANES), 1) // half)).astype(F32)
    tri = (_iota((c, c), 1) <= _iota((c, c), 0)).astype(BF16)
    lane_c = _iota((1, LANES), 1)
    m2 = [(lane_c < half).astype(F32), (lane_c >= half).astype(F32)]
    ri = _iota((n, n), 0)
    cj = _iota((n, n), 1)
    in_blk = lambda i, b: jnp.logical_and(i >= b * c, i < (b + 1) * c)
    same = jnp.logical_and(in_blk(ri, 0), in_blk(cj, 0))
    for b in range(1, nh):
        same = jnp.logical_or(same, jnp.logical_and(in_blk(ri, b), in_blk(cj, b)))
    strict = jnp.logical_and(same, cj < ri)
    incl = jnp.logical_and(same, cj <= ri)
    eye = (ri == cj).astype(F32)
    n_doub = max(1, math.ceil(math.log2(c)))
    w2b = w2_ref[...].astype(BF16)
    a2b = a2_ref[...].astype(BF16)
    g2b = g2_ref[...].astype(BF16)
    pair = lambda t, p: t[:, p * LANES:(p + 1) * LANES]

    def seg_sum(t):
        return _dot_exact(_split(t, 3), ones_w)

    def sequence(bi):
        def shifted(p_ref, buf, mu_ref):
            p = p_ref[bi]
            buf[bi, 8:8 + c, :] = p
            prev = buf[bi, 7:7 + c, :]
            buf[bi, 7:8, :] = p[c - 1:c, :]
            return p + (prev - p) * mu_ref[...]

        r = shifted(pr_ref, br_scr, mur_ref)
        k = shifted(pk_ref, bk_scr, muk_ref)
        v = shifted(pv_ref, bv_scr, muv_ref)
        x = shifted(px_ref, bx_scr, mux_ref)
        yield

        wl = w0_ref[...] + _dot(jnp.tanh(x).astype(BF16), w2b)
        wl = -jax.nn.softplus(-wl) - 0.5
        logw = -jnp.exp(wl)
        a = jax.nn.sigmoid(a0_ref[...] + _dot(x.astype(BF16), a2b))
        gate = _dot(jax.nn.sigmoid(x).astype(BF16), g2b)
        kk = k * kk_ref[...]
        yield
        kk = kk / jnp.maximum(jnp.sqrt(seg_sum(kk * kk)), 1e-12)
        k = k * (1.0 + (a - 1.0) * ka_ref[...])
        be = kk * a
        if l_valid < c * n_chunks:
            ok = (_iota((c, w), 0) + ci * c) < l_valid
            logw = jnp.where(ok, logw, 0.0)
            kk = jnp.where(ok, kk, 0.0)
            be = jnp.where(ok, be, 0.0)
            k = jnp.where(ok, k, 0.0)
            v = jnp.where(ok, v, 0.0)

        yield
        cum = _dot_exact(_split(logw, 3), tri, lambda p, t: _dot(t, p))
        yield
        tot = cum[c - 1:c, :]
        ah = -kk * jnp.exp(cum - logw)
        rh = r * jnp.exp(cum)
        ieg = jnp.exp(-cum)
        bc = be * ieg
        kc = k * ieg
        etail = jnp.exp(tot - cum)
        bt = be * etail
        kt = k * etail

        o_parts = {}

        def system(grp):
            stack = lambda t: jnp.concatenate(
                [pair(t, p) * m2[q] for p in grp for q in range(2)], axis=0)
            ars = _split(jnp.concatenate([stack(ah), stack(rh)], axis=0))
            vs = stack(v)
            yield
            gb = _dot3(ars, _split(stack(bc)), _dot_nt)
            yield
            gk = _dot3(ars, _split(stack(kc)), _dot_nt)
            xm = jnp.where(strict, gb[:n], 0.0)
            tm = eye + xm
            xp = _split(xm)
            yield
            if n_doub > 1:
                xp = _split(_dot3(xp, xp))
                yield
                for _ in range(n_doub - 2):
                    ts = _split(tm)
                    both = _dot3([jnp.concatenate([xp[i], ts[i]], axis=0) for i in range(2)], xp)
                    tm = tm + both[n:]
                    xp = _split(both[:n])
                    yield
                tm = tm + _dot3(_split(tm), xp)
                yield
            lm = jnp.concatenate([jnp.where(strict, gk[:n], 0.0), jnp.where(incl, gk[n:], 0.0)], axis=0)
            lmv = _dot3(_split(lm), _split(vs))
            mrb = jnp.where(incl, gb[n:], 0.0)
            yield

            y0 = [_dot3(_split(jnp.concatenate([pair(ah, p), pair(rh, p)], axis=0)),
                        _split(s_scr[bi, p]), _dot_nt) for p in grp]
            yield
            ys = jnp.concatenate([y[:c] * m2[q] for y in y0 for q in range(2)], axis=0) + lmv[:n]
            ps = _dot3(_split(tm), _split(ys))
            yield
            os_ = _dot3(_split(mrb), _split(ps)) + lmv[n:]
            yield
            for i, p in enumerate(grp):
                lo, mid, hi = 2 * i * c, (2 * i + 1) * c, (2 * i + 2) * c
                o_parts[p] = y0[i][c:] + os_[lo:mid] + os_[mid:hi]
                pv = jnp.concatenate([ps[lo:mid] + ps[mid:hi], pair(v, p)], axis=0)
                bk = jnp.concatenate([pair(bt, p), pair(kt, p)], axis=0)
                upd = _dot3(_split(pv), _split(bk), _dot_tn)
                s_scr[bi, p] = (s_scr[bi, p] * jnp.exp(pair(tot, p)) + upd) * ones_p
                yield

        systems = [system(list(range(s, s + gs))) for s in range(0, g, gs)]
        while systems:
            systems = [s for s in systems if next(s, True) is None]
            yield

        o = jnp.concatenate([o_parts[p] for p in range(g)], axis=1) if g > 1 else o_parts[0]
        inv_n = 1.0 / half
        mean = seg_sum(o) * inv_n
        yield
        d = o - mean
        var = seg_sum(d * d) * inv_n
        yield
        on = d * lax.rsqrt(var + GN_EPS) * gg_ref[...] + gb_ref[...]
        bonus = seg_sum(r * k * rk_ref[...]) * v
        o_ref[bi] = ((on + bonus) * gate).astype(o_ref.dtype)

    live = [sequence(bi) for bi in range(nb)]
    while live:
        live = [s for s in live if next(s, True) is None]

    @pl.when(ci == n_chunks - 1)
    def _():
        st_ref[...] = s_scr[...]


def _rwkv(proj3, shift0, s0, mu, w0, w2, a0, a2, g2, k_k, k_a, r_k, gn_g, gn_b, *, c, l_valid,
          l_out, o_width):
    b, lp, _ = proj3.shape
    cw = w0.shape[-1]
    npair = cw // LANES
    heads = cw // RWKV_HEAD
    g = 2 if npair % 2 == 0 else 1
    w = g * LANES
    ng = npair // g
    lw = mu.shape[-1] - 3 * cw
    assert lw % LANES == 0 and (3 * cw) % lw == 0 and lp % c == 0
    n_chunks = lp // c
    wl_, al_ = w2.shape[0], a2.shape[0]
    w2p = jnp.zeros((lw, cw), F32).at[:wl_].set(w2)
    a2p = jnp.zeros((lw, cw), F32).at[wl_:wl_ + al_].set(a2)
    g2p = jnp.zeros((lw, cw), F32).at[wl_ + al_:].set(g2)
    s0p = s0.reshape(b, npair, 2, RWKV_HEAD, RWKV_HEAD)
    z = jnp.zeros_like(s0p[:, :, 0])
    s0bd = jnp.concatenate([jnp.concatenate([s0p[:, :, 0], z], axis=-1),
                            jnp.concatenate([z, s0p[:, :, 1]], axis=-1)], axis=-2)
    sh3 = shift0.reshape(b, 1, -1)
    mu2 = mu.reshape(1, -1)
    row = lambda t: t.reshape(1, cw)
    xoff = (3 * cw) // lw

    nb = _largest_divisor(b, 4, 1)
    seg = lambda s: pl.BlockSpec((nb, c, w), lambda i, j, t: (i, t, s * ng + j))
    sseg = lambda s: pl.BlockSpec((nb, 1, w), lambda i, j, t: (i, 0, s * ng + j))
    mseg = lambda s: pl.BlockSpec((1, w), lambda i, j, t: (0, s * ng + j))
    vec = pl.BlockSpec((1, w), lambda i, j, t: (0, j))
    lora = pl.BlockSpec((lw, w), lambda i, j, t: (0, j))
    state = pl.BlockSpec((nb, g, LANES, LANES), lambda i, j, t: (i, j, 0, 0))
    in_specs = [seg(0), seg(1), seg(2), pl.BlockSpec((nb, c, lw), lambda i, j, t: (i, t, xoff)),
                sseg(0), sseg(1), sseg(2), pl.BlockSpec((nb, 1, lw), lambda i, j, t: (i, 0, xoff)),
                mseg(0), mseg(1), mseg(2), pl.BlockSpec((1, lw), lambda i, j, t: (0, xoff)),
                vec, vec, vec, vec, vec, vec, vec, lora, lora, lora, state]
    o, st = pl.pallas_call(
        functools.partial(_rwkv_kernel, c=c, g=g, nb=nb, l_valid=l_valid, n_chunks=n_chunks),
        grid=(b // nb, ng, n_chunks),
        in_specs=in_specs,
        out_specs=[pl.BlockSpec((nb, c, w), lambda i, j, t: (i, t, j)), state],
        out_shape=[jax.ShapeDtypeStruct((b, l_out, o_width), BF16),
                   jax.ShapeDtypeStruct((b, npair, LANES, LANES), F32)],
        scratch_shapes=[pltpu.VMEM((nb, g, LANES, LANES), F32),
                        pltpu.VMEM((nb, c + 8, w), F32), pltpu.VMEM((nb, c + 8, w), F32),
                        pltpu.VMEM((nb, c + 8, w), F32), pltpu.VMEM((nb, c + 8, lw), F32)],
        compiler_params=_params("parallel", "parallel", "arbitrary"),
        name="rwkv7_chunk",
    )(proj3, proj3, proj3, proj3, sh3, sh3, sh3, sh3, mu2, mu2, mu2, mu2,
      row(w0), row(a0), row(k_k), row(k_a), row(r_k), row(gn_g), row(gn_b), w2p, a2p, g2p, s0bd)
    st = jnp.stack([st[:, :, :RWKV_HEAD, :RWKV_HEAD], st[:, :, RWKV_HEAD:, RWKV_HEAD:]], axis=2)
    return o, st.reshape(b, heads, RWKV_HEAD, RWKV_HEAD)


def _logf_kernel(z_ref, bf_ref, lf_ref, cc_ref, cr_ref, *, rb):
    l, h = z_ref.shape[1], z_ref.shape[2]
    ti = _iota((rb, rb), 0)
    si = _iota((rb, rb), 1)
    tri = (si <= ti).astype(F32)
    carry = jnp.zeros((1, h), F32)
    for i in range(l // rb):
        sl = slice(i * rb, (i + 1) * rb)
        lf = jax.nn.log_sigmoid(z_ref[0, sl, :] + bf_ref[...])
        lf_ref[0, sl, :] = lf
        cblk = carry + _dot(tri, lf, HI)
        cc_ref[0, sl, :] = cblk
        carry = cblk[rb - 1:rb, :]
    eye = (_iota((h, h), 0) == _iota((h, h), 1)).astype(F32)
    cr_ref[0] = _dot_nt(eye, cc_ref[0], HI)


def _logf(z3, b_forget):
    b, l, h = z3.shape
    rb = _largest_divisor(l, 512, 8)
    blk = pl.BlockSpec((1, l, h), lambda i: (i, 0, 0))
    return pl.pallas_call(
        functools.partial(_logf_kernel, rb=rb),
        grid=(b,),
        in_specs=[blk, pl.BlockSpec((1, h), lambda i: (0, 0))],
        out_specs=[blk, blk, pl.BlockSpec((1, h, l), lambda i: (i, 0, 0))],
        out_shape=[jax.ShapeDtypeStruct((b, l, h), F32), jax.ShapeDtypeStruct((b, l, h), F32),
                   jax.ShapeDtypeStruct((b, h, l), F32)],
        compiler_params=_params("parallel"),
        name="fox_logf_cumsum",
    )(z3, b_forget.reshape(1, h))


def _fox_prompt_kernel(q_ref, k_ref, v_ref, cc_ref, cr_ref, o_ref, *, tq):
    l = q_ref.shape[1]
    nh = cc_ref.shape[2]
    h = pl.program_id(1)
    scale = FOX_HEAD ** -0.5
    sel = (_iota((nh, LANES), 0) == h).astype(F32)
    cq_all = _dot(cc_ref[0], sel, HI)
    ck = cr_ref[0, pl.ds(h, 1), :]
    kb = k_ref[0].astype(BF16)
    vb = v_ref[0].astype(BF16)
    for i in range(l // tq):
        nk = (i + 1) * tq
        rows = slice(i * tq, nk)
        s = _dot_nt(q_ref[0, rows, :].astype(BF16), kb[:nk]) * scale
        s = s + cq_all[rows, 0:1] - ck[:, :nk]
        qpos = _iota((tq, nk), 0) + i * tq
        kpos = _iota((tq, nk), 1)
        s = jnp.where(qpos >= kpos, s, NEG_INF)
        m = jnp.max(s, axis=-1, keepdims=True)
        p = jnp.exp(s - m)
        den = jnp.sum(p, axis=-1, keepdims=True)
        o = _dot(p.astype(BF16), vb[:nk]) / den
        o_ref[0, rows, :] = o.astype(o_ref.dtype)


def _fox_prompt(proj3, q_off, k3, v3, c_col, c_row):
    b, l, fw = k3.shape
    nh = c_col.shape[2]
    tq = _largest_divisor(l, 384, 8)
    blk = lambda off: pl.BlockSpec((1, l, LANES), lambda i, j: (i, 0, off + j))
    return pl.pallas_call(
        functools.partial(_fox_prompt_kernel, tq=tq),
        grid=(b, nh),
        in_specs=[blk(q_off), blk(0), blk(0),
                  pl.BlockSpec((1, l, nh), lambda i, j: (i, 0, 0)),
                  pl.BlockSpec((1, nh, l), lambda i, j: (i, 0, 0))],
        out_specs=blk(0),
        out_shape=jax.ShapeDtypeStruct((b, l, fw), BF16),
        compiler_params=_params("parallel", "parallel"),
        name="fox_prompt_attention",
    )(proj3, k3, v3, c_col, c_row)


def _page_tail_kernel(lf_ref, tail_ref, tot_ref):
    pp = lf_ref.shape[0]
    later = (_iota((PAGE_SIZE, PAGE_SIZE), 0) > _iota((PAGE_SIZE, PAGE_SIZE), 1)).astype(BF16)
    for i in range(pp):
        lf = lf_ref[i]
        tail_ref[i] = _dot_exact(_split(lf, 3), later)
        tot_ref[i] = jnp.broadcast_to(jnp.sum(lf, axis=1, keepdims=True), lf.shape)


def _page_tails(lf_pool_t):
    n_phys, nh, _ = lf_pool_t.shape
    pp = _largest_divisor(n_phys, 32, 1)
    blk = pl.BlockSpec((pp, nh, PAGE_SIZE), lambda i: (i, 0, 0))
    shape = jax.ShapeDtypeStruct(lf_pool_t.shape, F32)
    return pl.pallas_call(
        _page_tail_kernel,
        grid=(n_phys // pp,),
        in_specs=[blk],
        out_specs=[blk, blk],
        out_shape=[shape, shape],
        compiler_params=_params("parallel"),
        name="fox_page_tails",
    )(lf_pool_t)


def _fox_sample_kernel(pt_ref, q_ref, kn_ref, vn_ref, cq_ref, ck_ref, *rest, nh, n_steps, pg):
    kp = rest[:pg]
    vp = rest[pg:2 * pg]
    tl = rest[2 * pg:3 * pg]
    tt = rest[3 * pg:4 * pg]
    o_ref, m_scr, l_scr, acc_scr, suf_scr = rest[4 * pg:]
    pi = pl.program_id(1)
    rows = q_ref.shape[1]
    cols = PAGE_SIZE * nh
    scale = FOX_HEAD ** -0.5

    @pl.when(pi == 0)
    def _():
        m_scr[...] = jnp.full(m_scr.shape, NEG_INF, F32)
        l_scr[...] = jnp.zeros(l_scr.shape, F32)
        acc_scr[...] = jnp.zeros(acc_scr.shape, F32)
        suf_scr[...] = jnp.zeros(suf_scr.shape, F32)

    qb = q_ref[0].astype(BF16)
    cnew = cq_ref[0]

    def online(scores, vals):
        m_old = m_scr[...]
        m_new = m_old
        for s in scores:
            m_new = jnp.maximum(m_new, jnp.max(s, axis=-1, keepdims=True))
        corr = jnp.exp(m_old - m_new)
        den = l_scr[...] * corr
        acc = acc_scr[...] * corr
        for s, val in zip(scores, vals):
            pe = jnp.exp(s - m_new)
            den = den + jnp.sum(pe, axis=-1, keepdims=True)
            acc = acc + _dot(pe.astype(BF16), val)
        l_scr[...] = den
        acc_scr[...] = acc
        m_scr[...] = m_new

    own_head = (_iota((rows, cols), 0) % nh) == (_iota((rows, cols), 1) % nh)
    suf = suf_scr[...]
    bias = [None] * pg
    for gi in reversed(range(pg)):
        bias[gi] = suf + tl[gi][0]
        suf = suf + tt[gi][0]
    suf_scr[...] = suf
    scores = []
    for gi in range(pg):
        kflat = kp[gi][0].reshape(cols, FOX_HEAD).astype(BF16)
        s = _dot_nt(qb, kflat) * scale + bias[gi] + cnew
        scores.append(jnp.where(own_head, s, NEG_INF))
    online(scores, [r[0].reshape(cols, FOX_HEAD).astype(BF16) for r in vp])

    @pl.when(pi == n_steps - 1)
    def _():
        sn = _dot_nt(qb, kn_ref[0].astype(BF16)) * scale + cnew - ck_ref[0]
        ri = _iota((rows, rows), 0)
        cj = _iota((rows, rows), 1)
        ok = jnp.logical_and((ri % nh) == (cj % nh), (cj // nh) <= (ri // nh))
        online([jnp.where(ok, sn, NEG_INF)], [vn_ref[0].astype(BF16)])
        o_ref[0] = (acc_scr[...] / l_scr[...]).astype(o_ref.dtype)


def _fox_sample(q, k_new, v_new, c_col, k_pool, v_pool, lf_pool, page_table):
    bd, t_new, width = q.shape
    nh = c_col.shape[2]
    n_pages = page_table.shape[1]
    n_phys = k_pool.shape[0]
    pg = _largest_divisor(n_pages, 8, 1)
    n_steps = n_pages // pg
    rows = t_new * nh
    cols = PAGE_SIZE * nh
    tail, tot = _page_tails(jnp.swapaxes(lf_pool, 1, 2))
    tail = jnp.swapaxes(tail, 1, 2).reshape(n_phys, 1, cols)
    tot = jnp.swapaxes(tot, 1, 2).reshape(n_phys, 1, cols)
    by_head = lambda a: a.reshape(bd, rows, FOX_HEAD)
    full = lambda shape: pl.BlockSpec((1,) + shape, lambda i, p, pt: (i, 0, 0))

    def page(shape, slot):
        zeros = (0,) * len(shape)
        return pl.BlockSpec((1,) + shape,
                            lambda i, p, pt: (pt[i, n_pages - (p + 1) * pg + slot],) + zeros)

    slots = range(pg)
    kv_page = (PAGE_SIZE, nh, FOX_HEAD)
    grid_spec = pltpu.PrefetchScalarGridSpec(
        num_scalar_prefetch=1,
        grid=(bd, n_steps),
        in_specs=([full((rows, FOX_HEAD)), full((rows, FOX_HEAD)), full((rows, FOX_HEAD)),
                   full((rows, 1)), full((1, rows))]
                  + [page(kv_page, s) for s in slots] + [page(kv_page, s) for s in slots]
                  + [page((1, cols), s) for s in slots] + [page((1, cols), s) for s in slots]),
        out_specs=full((rows, FOX_HEAD)),
        scratch_shapes=[pltpu.VMEM((rows, 1), F32), pltpu.VMEM((rows, 1), F32),
                        pltpu.VMEM((rows, FOX_HEAD), F32), pltpu.VMEM((1, cols), F32)],
    )
    out = pl.pallas_call(
        functools.partial(_fox_sample_kernel, nh=nh, n_steps=n_steps, pg=pg),
        grid_spec=grid_spec,
        out_shape=jax.ShapeDtypeStruct((bd, rows, FOX_HEAD), BF16),
        compiler_params=_params("parallel", "arbitrary"),
        name="fox_sample_attention",
    )(page_table, by_head(q), by_head(k_new), by_head(v_new),
      c_col.reshape(bd, rows, 1), c_col.reshape(bd, 1, rows),
      *([k_pool] * pg), *([v_pool] * pg), *([tail] * pg), *([tot] * pg))
    return out.reshape(bd, t_new, width)


def _conv_kernel(bg_ref, cg_ref, h_ref, w_ref, buf_ref, y_ref, nb_ref, u_scr, *, tt, n_t):
    ti = pl.program_id(2)

    @pl.when(ti == 0)
    def _():
        u_scr[6:8, :] = buf_ref[0]

    u = cg_ref[0] * h_ref[0]
    u_scr[8:8 + tt, :] = u
    y = (w_ref[0:1, :] * u_scr[6:6 + tt, :] + w_ref[1:2, :] * u_scr[7:7 + tt, :]
         + w_ref[2:3, :] * u)
    y_ref[0] = (bg_ref[0] * y).astype(y_ref.dtype)
    tail = u_scr[6 + tt:8 + tt, :]
    u_scr[6:8, :] = tail

    @pl.when(ti == n_t - 1)
    def _():
        nb_ref[0] = tail


def _short_conv(proj3, buf0, conv_w):
    b, l, d3 = proj3.shape
    d = d3 // 3
    tc = _largest_divisor(d, 512, LANES)
    tt = _largest_divisor(l, 704, 8)
    n_t = l // tt
    nc = d // tc
    seg = lambda s: pl.BlockSpec((1, tt, tc), lambda i, j, t: (i, t, s * nc + j))
    return pl.pallas_call(
        functools.partial(_conv_kernel, tt=tt, n_t=n_t),
        grid=(b, nc, n_t),
        in_specs=[seg(0), seg(1), seg(2),
                  pl.BlockSpec((conv_w.shape[0], tc), lambda i, j, t: (0, j)),
                  pl.BlockSpec((1, 2, tc), lambda i, j, t: (i, 0, j))],
        out_specs=[pl.BlockSpec((1, tt, tc), lambda i, j, t: (i, t, j)),
                   pl.BlockSpec((1, 2, tc), lambda i, j, t: (i, 0, j))],
        out_shape=[jax.ShapeDtypeStruct((b, l, d), BF16), jax.ShapeDtypeStruct((b, 2, d), F32)],
        scratch_shapes=[pltpu.VMEM((tt + 8, tc), F32)],
        compiler_params=_params("parallel", "parallel", "arbitrary"),
        name="short_conv",
    )(proj3, proj3, proj3, conv_w, buf0)


def _trunk(hp3, hs3, s0, shift0, conv0, k_pool, v_pool, lf_pool, page_table, wts):
    (norm_mix, norm_mlp, w_in_even, b_forget, rwkv_mu, rwkv_w0, rwkv_w2, rwkv_a0, rwkv_a2, rwkv_g2,
     rwkv_k_k, rwkv_k_a, rwkv_r_k, rwkv_gn_g, rwkv_gn_b, w_out_even, w_in_odd, conv_w, w_out_odd,
     w_up, w_down) = wts
    bp, l, d = hp3.shape
    bs, t, _ = hs3.shape
    hp = hp3.reshape(bp * l, d)
    hs = hs3.reshape(bs * t, d)
    cw = rwkv_w0.shape[-1]
    rproj = rwkv_mu.shape[-1]
    fw = d - cw
    nh = fw // FOX_HEAD
    heads = cw // RWKV_HEAD
    rw = (rwkv_mu[0], rwkv_w0[0], rwkv_w2[0], rwkv_a0[0], rwkv_a2[0], rwkv_g2[0], rwkv_k_k[0],
          rwkv_k_a[0], rwkv_r_k[0], rwkv_gn_g[0], rwkv_gn_b[0])
    norm = lambda a, g: _rmsnorm(a, g, BF16)
    wide = functools.partial(_matmul, tn=512, tk=4096)
    narrow = functools.partial(_matmul, tn=256, tk=4096)

    def down(up, us, layer, hp, hs):
        n_slabs = max(1, w_down.shape[1] // d)
        for slab in range(n_slabs):
            hp, hs = wide(up, us, w_down, layer, epilogue="resid", resid=(hp, hs),
                          k_slab=(slab, n_slabs))
        return hp, hs

    lpad = RWKV_CHUNK * pl.cdiv(l, RWKV_CHUNK)
    if lpad > l:
        hnp, hnp_pad = _rmsnorm_pad(hp3, norm_mix[0], lpad)
        hnp = hnp.reshape(bp * l, d)
        hnp_pad = hnp_pad.reshape(bp * lpad, d)
    else:
        hnp = hnp_pad = norm(hp, norm_mix[0])
    hns = norm(hs, norm_mix[0])
    w_in_t = jnp.swapaxes(w_in_even, 1, 2)
    in_proj = functools.partial(narrow, w_nk=True)
    pq_p, pq_s = in_proj(hnp_pad, hns, w_in_t, 0, n_cols=rproj + fw)
    k_p, k_s = in_proj(hnp, hns, w_in_t, 0, n_lo=rproj + fw, n_cols=fw)
    v_p, v_s = in_proj(hnp, hns, w_in_t, 0, n_lo=rproj + 2 * fw, n_cols=fw)
    w_lf = w_in_t[:, rproj + 3 * fw:, :]
    z_p, _ = _matmul(hnp, None, w_lf, 0, tn=nh, tk=4096, w_nk=True)
    z_s, _ = _matmul(hns, None, w_lf, 0, tn=nh, tk=4096, w_nk=True)
    pq_p = pq_p.reshape(bp, lpad, rproj + fw)
    pq_s = pq_s.reshape(bs, t, rproj + fw)
    k_p, v_p = k_p.reshape(bp, l, fw), v_p.reshape(bp, l, fw)
    k_s, v_s = k_s.reshape(bs, t, fw), v_s.reshape(bs, t, fw)
    lf_p, cc_p, cr_p = _logf(z_p.reshape(bp, l, nh), b_forget[0])
    lf_s, cc_s, _ = _logf(z_s.reshape(bs, t, nh), b_forget[0])

    zero_s = jnp.zeros((bp, heads, RWKV_HEAD, RWKV_HEAD), F32)
    or_p, st_p = _rwkv(pq_p, jnp.zeros((bp, rproj), F32), zero_s, *rw, c=RWKV_CHUNK, l_valid=l,
                       l_out=l, o_width=cw)
    of_p = _fox_prompt(pq_p, rproj // LANES, k_p, v_p, cc_p, cr_p)
    tpad = 8 * pl.cdiv(t, 8)
    pq_s_pad = jnp.pad(pq_s[:, :, :rproj], ((0, 0), (0, tpad - t), (0, 0)))
    or_s, st_s = _rwkv(pq_s_pad, shift0[0], s0[0], *rw, c=tpad, l_valid=t, l_out=tpad, o_width=cw)
    of_s = _fox_sample(pq_s[:, :, rproj:], k_s, v_s, cc_s, k_pool[0], v_pool[0], lf_pool[0],
                       page_table)
    mix_p = (or_p.reshape(bp * l, cw), of_p.reshape(bp * l, fw))
    mix_s = (or_s[:, :t].reshape(bs * t, cw), of_s.reshape(bs * t, fw))
    hp, hs = wide(mix_p, mix_s, w_out_even, 0, epilogue="resid", resid=(hp, hs))
    up, us = wide(norm(hp, norm_mlp[0]), norm(hs, norm_mlp[0]), w_up, 0, epilogue="relu2",
                  out_dtype=BF16)
    hp, hs = down(up, us, 0, hp, hs)

    p1_p, p1_s = wide(norm(hp, norm_mix[1]), norm(hs, norm_mix[1]), w_in_odd, 0)
    y_p, buf_p = _short_conv(p1_p.reshape(bp, l, 3 * d), jnp.zeros((bp,) + conv0.shape[2:], F32),
                             conv_w[0])
    y_s, buf_s = _short_conv(p1_s.reshape(bs, t, 3 * d), conv0[0], conv_w[0])
    hp, hs = wide(y_p.reshape(bp * l, d), y_s.reshape(bs * t, d), w_out_odd, 0, epilogue="resid",
                  resid=(hp, hs))
    up, us = wide(norm(hp, norm_mlp[1]), norm(hs, norm_mlp[1]), w_up, 1, epilogue="relu2",
                  out_dtype=BF16)
    hp, hs = down(up, us, 1, hp, hs)

    head4 = lambda a, b, n: a.reshape(b, n, nh, FOX_HEAD)[None]
    outs_p = (st_p[None], pq_p[:, l - 1, :rproj][None], head4(k_p, bp, l), head4(v_p, bp, l),
              lf_p[None], buf_p[None])
    outs_s = (st_s[None], pq_s[:, t - 1, :rproj][None], head4(k_s, bs, t), head4(v_s, bs, t),
              lf_s[None], buf_s[None])
    return hp.reshape(bp, l, d), hs, outs_p, outs_s


def kernel(x_prompt, x_sample, state_rwkv, state_rwkv_shift, cache_fox_k, cache_fox_v, cache_fox_logf, state_conv, page_table, meta_tokens, norm_mix, norm_mlp, norm_final, w_in_even, b_forget, rwkv_mu, rwkv_w0, rwkv_w2, rwkv_a0, rwkv_a2, rwkv_g2, rwkv_k_k, rwkv_k_a, rwkv_r_k, rwkv_gn_g, rwkv_gn_b, w_out_even, w_in_odd, conv_w, w_out_odd, w_up, w_down):
    wts = (norm_mix, norm_mlp, w_in_even, b_forget, rwkv_mu, rwkv_w0, rwkv_w2, rwkv_a0, rwkv_a2,
           rwkv_g2, rwkv_k_k, rwkv_k_a, rwkv_r_k, rwkv_gn_g, rwkv_gn_b, w_out_even, w_in_odd, conv_w,
           w_out_odd, w_up, w_down)
    bp, _, d = x_prompt.shape
    bs, ts, _ = x_sample.shape
    h0 = jnp.concatenate([jnp.broadcast_to(meta_tokens[None], (bp, N_META, d)), x_prompt], axis=1)
    hp, hs, outs_p, outs_s = _trunk(h0, x_sample, state_rwkv, state_rwkv_shift, state_conv,
                                    cache_fox_k, cache_fox_v, cache_fox_logf, page_table, wts)
    y_prompt = _final_norm_prompt(hp, norm_final)
    y_sample = _rmsnorm(hs, norm_final, F32).reshape(bs, ts, d)
    return (y_prompt, y_sample, *outs_p, *outs_s)
```

```python
import functools
import math

import jax
import jax.numpy as jnp
from jax import lax
from jax.experimental import pallas as pl
from jax.experimental.pallas import tpu as pltpu

F32 = jnp.float32
BF16 = jnp.bfloat16
HI = lax.Precision.HIGHEST

RMS_EPS = 1e-6
GN_EPS = 64e-5
NEG_INF = -1e30
N_META = 16
RWKV_HEAD = 64
FOX_HEAD = 128
PAGE_SIZE = 128
LANES = 128
VMEM_LIMIT = 62 * 1024 * 1024
BF16_ROWS = 16
RWKV_CHUNK = 48


def _largest_divisor(n, cap, mult):
    best = None
    for d in range(mult, min(n, cap) + 1, mult):
        if n % d == 0:
            best = d
    return n if best is None else best


def _params(*sem, flags=None):
    return pltpu.CompilerParams(dimension_semantics=sem, vmem_limit_bytes=VMEM_LIMIT, flags=flags)


def _dot(a, b, precision=None):
    return jnp.dot(a, b, precision=precision, preferred_element_type=F32)


def _dot_nt(a, b, precision=None):
    return lax.dot_general(a, b, (((1,), (1,)), ((), ())), precision=precision,
                           preferred_element_type=F32)


def _dot_tn(a, b, precision=None):
    return lax.dot_general(a, b, (((0,), (0,)), ((), ())), precision=precision,
                           preferred_element_type=F32)


def _iota(shape, dim):
    return lax.broadcasted_iota(jnp.int32, shape, dim)


def _split(x, pieces=2):
    out = []
    for i in range(pieces):
        p = x.astype(BF16)
        out.append(p)
        if i + 1 < pieces:
            x = x - p.astype(F32)
    return out


def _dot3(a, b, f=_dot):
    axis = 1 if f is _dot_tn else 0
    m = a[0].shape[axis]
    if m % BF16_ROWS:
        return f(a[0], b[0]) + (f(a[0], b[1]) + f(a[1], b[0]))
    t = f(jnp.concatenate([a[0], a[1]], axis=axis), b[0])
    return (t[:m] + t[m:]) + f(a[0], b[1])


def _dot_exact(pieces, other, f=_dot):
    m = pieces[0].shape[0]
    if f is _dot and m % BF16_ROWS == 0:
        t = f(jnp.concatenate(pieces, axis=0), other)
        return sum(t[i * m:(i + 1) * m] for i in range(1, len(pieces))) + t[:m]
    acc = f(pieces[0], other)
    for p in pieces[1:]:
        acc = acc + f(p, other)
    return acc


def _rmsnorm_kernel(x_ref, g_ref, o_ref):
    x = x_ref[...]
    ms = jnp.mean(x * x, axis=-1, keepdims=True)
    o_ref[...] = (x * lax.rsqrt(ms + RMS_EPS) * g_ref[...]).astype(o_ref.dtype)


def _rmsnorm(x2d, g, out_dtype):
    m, d = x2d.shape
    tr = _largest_divisor(m, 704, BF16_ROWS)
    return pl.pallas_call(
        _rmsnorm_kernel,
        grid=(m // tr,),
        in_specs=[pl.BlockSpec((tr, d), lambda i: (i, 0)),
                  pl.BlockSpec((1, d), lambda i: (0, 0))],
        out_specs=pl.BlockSpec((tr, d), lambda i: (i, 0)),
        out_shape=jax.ShapeDtypeStruct((m, d), out_dtype),
        compiler_params=_params("parallel"),
        name="rmsnorm",
    )(x2d, g.reshape(1, d))


def _rmsnorm_pad_kernel(x_ref, g_ref, o_ref, op_ref, *, n_real):
    x = x_ref[0]
    ms = jnp.mean(x * x, axis=-1, keepdims=True)
    y = (x * lax.rsqrt(ms + RMS_EPS) * g_ref[...]).astype(o_ref.dtype)
    o_ref[0] = y
    op_ref[0] = jnp.where(pl.program_id(1) < n_real, y, jnp.zeros_like(y))


def _rmsnorm_pad(h3, g, lp):
    b, l, d = h3.shape
    tr = _largest_divisor(math.gcd(l, lp), 512, BF16_ROWS)
    n_real = l // tr
    real = pl.BlockSpec((1, tr, d), lambda i, j: (i, jnp.minimum(j, n_real - 1), 0))
    return pl.pallas_call(
        functools.partial(_rmsnorm_pad_kernel, n_real=n_real),
        grid=(b, lp // tr),
        in_specs=[real, pl.BlockSpec((1, d), lambda i, j: (0, 0))],
        out_specs=[real, pl.BlockSpec((1, tr, d), lambda i, j: (i, j, 0))],
        out_shape=[jax.ShapeDtypeStruct((b, l, d), BF16), jax.ShapeDtypeStruct((b, lp, d), BF16)],
        compiler_params=_params("parallel", "arbitrary"),
        name="rmsnorm_pad",
    )(h3, g.reshape(1, d))


def _final_norm_prompt_kernel(a_ref, b_ref, g_ref, o_ref):
    x = jnp.concatenate([a_ref[0, N_META:, :], b_ref[0]], axis=0)
    ms = jnp.mean(x * x, axis=-1, keepdims=True)
    o_ref[0] = x * lax.rsqrt(ms + RMS_EPS) * g_ref[...]


def _final_norm_prompt(h3, g):
    b, l, d = h3.shape
    s = l - N_META
    tr = _largest_divisor(s, 256, N_META)
    sub = tr // N_META
    return pl.pallas_call(
        _final_norm_prompt_kernel,
        grid=(b, s // tr),
        in_specs=[pl.BlockSpec((1, tr, d), lambda i, j: (i, j, 0)),
                  pl.BlockSpec((1, N_META, d), lambda i, j: (i, (j + 1) * sub, 0)),
                  pl.BlockSpec((1, d), lambda i, j: (0, 0))],
        out_specs=pl.BlockSpec((1, tr, d), lambda i, j: (i, j, 0)),
        out_shape=jax.ShapeDtypeStruct((b, s, d), F32),
        compiler_params=_params("parallel", "parallel"),
        name="final_norm_prompt",
    )(h3, h3, g.reshape(1, d))


def _mm_kernel(*refs, nk, nx, epilogue, ns, spare, w_nk):
    refs = list(refs)
    take = lambda cnt: [refs.pop(0) for _ in range(cnt)]
    xp, xs, (w_ref,) = take(nx), take(nx * ns), take(1)
    rp, rs = (take(1 + ns) + [None])[:2] if epilogue == "resid" else (None, None)
    op, osm = (take(1 + ns) + [None])[:2]
    accp, accs = (take(1 + ns) + [None])[:2] if refs else (None, None)
    i = pl.program_id(0)
    k = pl.program_id(2)
    wb = w_ref[...].astype(BF16)
    mm = _dot_nt if w_nk else _dot

    def run(x_refs, r_ref, o_ref, acc_ref):
        def finish(acc):
            if epilogue == "relu2":
                acc = jnp.square(jnp.maximum(acc, 0.0))
            elif epilogue == "resid":
                acc = r_ref[...] + acc
            o_ref[...] = acc.astype(o_ref.dtype)

        if nk == 1:
            finish(mm(x_refs[0][...], wb))
        elif epilogue == "resid" and nx == 2:
            @pl.when(k == 0)
            def _():
                o_ref[...] = r_ref[...] + mm(x_refs[0][...], wb)

            @pl.when(k == 1)
            def _():
                o_ref[...] += mm(x_refs[1][...], wb)
        elif epilogue == "resid":
            part = mm(x_refs[0][...], wb)

            @pl.when(k == 0)
            def _():
                o_ref[...] = r_ref[...] + part

            @pl.when(k > 0)
            def _():
                o_ref[...] += part
        else:
            assert nx == 1
            part = mm(x_refs[0][...], wb)

            @pl.when(k == 0)
            def _():
                acc_ref[...] = part

            @pl.when(jnp.logical_and(k > 0, k < nk - 1))
            def _():
                acc_ref[...] += part

            @pl.when(k == nk - 1)
            def _():
                finish(acc_ref[...] + part)

    run(xp, rp, op, accp)
    if ns:
        @pl.when(i == 0)
        def _():
            run(xs, rs, osm, accs)

        if spare:
            @pl.when(i > 0)
            def _():
                osm[...] = jnp.zeros(osm.shape, osm.dtype)


def _matmul(xp, xs, w, layer, *, n_lo=0, n_cols=None, tn, tk, epilogue="plain", resid=None,
            out_dtype=F32, k_slab=(0, 1), w_nk=False):
    xp = xp if isinstance(xp, tuple) else (xp,)
    ns = 0 if xs is None else 1
    xs = () if xs is None else (xs if isinstance(xs, tuple) else (xs,))
    nx = len(xp)
    m, kdim = xp[0].shape[0], nx * xp[0].shape[1]
    ms = xs[0].shape[0] if ns else 0
    n_cols = w.shape[1 if w_nk else 2] - n_lo if n_cols is None else n_cols
    tn = _largest_divisor(math.gcd(n_cols, n_lo), tn, LANES)
    slab, n_slabs = k_slab
    kdim //= n_slabs
    tk = kdim // 2 if nx == 2 else _largest_divisor(kdim, tk, LANES)
    assert n_lo % tn == 0 and n_cols % tn == 0 and (n_slabs == 1 or (tk == kdim and nx == 1))
    tm = _largest_divisor(m, 2112, BF16_ROWS)
    nk = kdim // tk
    nj = n_cols // tn
    off = n_lo // tn
    kblk = (lambda k: k + slab) if nx == 1 else (lambda k: 0)
    once = pl.Buffered(1) if (nk == 1 or nx == 2) and tn > 2 * LANES else None
    in_specs = ([pl.BlockSpec((tm, tk), lambda i, j, k: (i, kblk(k)), pipeline_mode=once)] * nx
                + [pl.BlockSpec((ms, tk), lambda i, j, k: (0, kblk(k)))] * (nx * ns)
                + [pl.BlockSpec((None, tn, tk), lambda i, j, k: (layer, j + off, k + slab)) if w_nk
                   else pl.BlockSpec((None, tk, tn), lambda i, j, k: (layer, k + slab, j + off))])
    args = [*xp, *xs, w]
    out_specs = [pl.BlockSpec((tm, tn), lambda i, j, k: (i, j))]
    out_shape = [jax.ShapeDtypeStruct((m, n_cols), out_dtype)]
    use_acc = nk > 1 and epilogue != "resid"
    acc = [pltpu.VMEM((tm, tn), F32)] if use_acc else []
    if epilogue == "resid":
        in_specs.append(pl.BlockSpec((tm, tn), lambda i, j, k: (i, j)))
        args.append(resid[0])
    spare = m // tm > 1
    if ns:
        if epilogue == "resid":
            in_specs.append(pl.BlockSpec((ms, tn), lambda i, j, k: (0, jnp.where(i == 0, j, 0))))
            args.append(resid[1])
        out_specs.append(pl.BlockSpec((ms, tn), lambda i, j, k: (0, jnp.where(i == 0, j, nj))))
        out_shape.append(jax.ShapeDtypeStruct((ms, n_cols + tn * spare), out_dtype))
        acc += [pltpu.VMEM((ms, tn), F32)] if use_acc else []
    outs = pl.pallas_call(
        functools.partial(_mm_kernel, nk=nk, nx=nx, epilogue=epilogue, ns=ns, spare=spare, w_nk=w_nk),
        grid=(m // tm, nj, nk),
        in_specs=in_specs,
        out_specs=out_specs,
        out_shape=out_shape,
        scratch_shapes=acc,
        compiler_params=_params("arbitrary", "arbitrary", "arbitrary"),
        name="matmul_" + epilogue,
    )(*args)
    return (outs[0], outs[1][:, :n_cols]) if ns else (outs[0], None)


def _rwkv_kernel(pr_ref, pk_ref, pv_ref, px_ref, sr_ref, sk_ref, sv_ref, sx_ref,
                 mur_ref, muk_ref, muv_ref, mux_ref, w0_ref, a0_ref, kk_ref, ka_ref, rk_ref,
                 gg_ref, gb_ref, w2_ref, a2_ref, g2_ref, s0_ref,
                 o_ref, st_ref,
                 s_scr, br_scr, bk_scr, bv_scr, bx_scr, *, c, g, nb, l_valid, n_chunks):
    ci = pl.program_id(2)
    half = RWKV_HEAD
    w = g * LANES
    gs = 1
    nh = 2 * gs
    n = nh * c

    @pl.when(ci == 0)
    def _():
        s_scr[...] = s0_ref[...]
        br_scr[:, 7:8, :] = sr_ref[...]
        bk_scr[:, 7:8, :] = sk_ref[...]
        bv_scr[:, 7:8, :] = sv_ref[...]
        bx_scr[:, 7:8, :] = sx_ref[...]

    ones_w = ((_iota((w, w), 0) // half) == (_iota((w, w), 1) // half)).astype(BF16)
    ones_p = ((_iota((LANES, LANES), 0) // half) == (_iota((LANES, LANES), 1) // half)).astype(F32)
    tri = (_iota((c, c), 1) <= _iota((c, c), 0)).astype(BF16)
    lane_c = _iota((1, LANES), 1)
    m2 = [(lane_c < half).astype(F32), (lane_c >= half).astype(F32)]
    ri = _iota((n, n), 0)
    cj = _iota((n, n), 1)
    in_blk = lambda i, b: jnp.logical_and(i >= b * c, i < (b + 1) * c)
    same = jnp.logical_and(in_blk(ri, 0), in_blk(cj, 0))
    for b in range(1, nh):
        same = jnp.logical_or(same, jnp.logical_and(in_blk(ri, b), in_blk(cj, b)))
    strict = jnp.logical_and(same, cj < ri)
    incl = jnp.logical_and(same, cj <= ri)
    eye = (ri == cj).astype(F32)
    n_doub = max(1, math.ceil(math.log2(c)))
    w2b = w2_ref[...].astype(BF16)
    a2b = a2_ref[...].astype(BF16)
    g2b = g2_ref[...].astype(BF16)
    pair = lambda t, p: t[:, p * LANES:(p + 1) * LANES]

    def seg_sum(t):
        return _dot_exact(_split(t, 3), ones_w)

    def sequence(bi):
        def shifted(p_ref, buf, mu_ref):
            p = p_ref[bi]
            buf[bi, 8:8 + c, :] = p
            prev = buf[bi, 7:7 + c, :]
            buf[bi, 7:8, :] = p[c - 1:c, :]
            return p + (prev - p) * mu_ref[...]

        r = shifted(pr_ref, br_scr, mur_ref)
        k = shifted(pk_ref, bk_scr, muk_ref)
        v = shifted(pv_ref, bv_scr, muv_ref)
        x = shifted(px_ref, bx_scr, mux_ref)
        yield

        wl = w0_ref[...] + _dot(jnp.tanh(x).astype(BF16), w2b)
        wl = -jax.nn.softplus(-wl) - 0.5
        logw = -jnp.exp(wl)
        a = jax.nn.sigmoid(a0_ref[...] + _dot(x.astype(BF16), a2b))
        gate = _dot(jax.nn.sigmoid(x).astype(BF16), g2b)
        kk = k * kk_ref[...]
        yield
        kk = kk / jnp.maximum(jnp.sqrt(seg_sum(kk * kk)), 1e-12)
        k = k * (1.0 + (a - 1.0) * ka_ref[...])
        be = kk * a
        if l_valid < c * n_chunks:
            ok = (_iota((c, w), 0) + ci * c) < l_valid
            logw = jnp.where(ok, logw, 0.0)
            kk = jnp.where(ok, kk, 0.0)
            be = jnp.where(ok, be, 0.0)
            k = jnp.where(ok, k, 0.0)
            v = jnp.where(ok, v, 0.0)

        yield
        cum = _dot_exact(_split(logw, 3), tri, lambda p, t: _dot(t, p))
        yield
        tot = cum[c - 1:c, :]
        ah = -kk * jnp.exp(cum - logw)
        rh = r * jnp.exp(cum)
        ieg = jnp.exp(-cum)
        bc = be * ieg
        kc = k * ieg
        etail = jnp.exp(tot - cum)
        bt = be * etail
        kt = k * etail

        o_parts = {}

        def system(grp):
            stack = lambda t: jnp.concatenate(
                [pair(t, p) * m2[q] for p in grp for q in range(2)], axis=0)
            ars = _split(jnp.concatenate([stack(ah), stack(rh)], axis=0))
            vs = stack(v)
            yield
            gb = _dot3(ars, _split(stack(bc)), _dot_nt)
            yield
            gk = _dot3(ars, _split(stack(kc)), _dot_nt)
            xm = jnp.where(strict, gb[:n], 0.0)
            tm = eye + xm
            xp = _split(xm)
            yield
            if n_doub > 1:
                xp = _split(_dot3(xp, xp))
                yield
                for _ in range(n_doub - 2):
                    ts = _split(tm)
                    both = _dot3([jnp.concatenate([xp[i], ts[i]], axis=0) for i in range(2)], xp)
                    tm = tm + both[n:]
                    xp = _split(both[:n])
                    yield
                tm = tm + _dot3(_split(tm), xp)
                yield
            lm = jnp.concatenate([jnp.where(strict, gk[:n], 0.0), jnp.where(incl, gk[n:], 0.0)], axis=0)
            lmv = _dot3(_split(lm), _split(vs))
            mrb = jnp.where(incl, gb[n:], 0.0)
            yield

            y0 = [_dot3(_split(jnp.concatenate([pair(ah, p), pair(rh, p)], axis=0)),
                        _split(s_scr[bi, p]), _dot_nt) for p in grp]
            yield
            ys = jnp.concatenate([y[:c] * m2[q] for y in y0 for q in range(2)], axis=0) + lmv[:n]
            ps = _dot3(_split(tm), _split(ys))
            yield
            os_ = _dot3(_split(mrb), _split(ps)) + lmv[n:]
            yield
            for i, p in enumerate(grp):
                lo, mid, hi = 2 * i * c, (2 * i + 1) * c, (2 * i + 2) * c
                o_parts[p] = y0[i][c:] + os_[lo:mid] + os_[mid:hi]
                pv = jnp.concatenate([ps[lo:mid] + ps[mid:hi], pair(v, p)], axis=0)
                bk = jnp.concatenate([pair(bt, p), pair(kt, p)], axis=0)
                upd = _dot3(_split(pv), _split(bk), _dot_tn)
                s_scr[bi, p] = (s_scr[bi, p] * jnp.exp(pair(tot, p)) + upd) * ones_p
                yield

        systems = [system(list(range(s, s + gs))) for s in range(0, g, gs)]
        while systems:
            systems = [s for s in systems if next(s, True) is None]
            yield

        o = jnp.concatenate([o_parts[p] for p in range(g)], axis=1) if g > 1 else o_parts[0]
        inv_n = 1.0 / half
        mean = seg_sum(o) * inv_n
        yield
        d = o - mean
        var = seg_sum(d * d) * inv_n
        yield
        on = d * lax.rsqrt(var + GN_EPS) * gg_ref[...] + gb_ref[...]
        bonus = seg_sum(r * k * rk_ref[...]) * v
        o_ref[bi] = ((on + bonus) * gate).astype(o_ref.dtype)

    live = [sequence(bi) for bi in range(nb)]
    while live:
        live = [s for s in live if next(s, True) is None]

    @pl.when(ci == n_chunks - 1)
    def _():
        st_ref[...] = s_scr[...]


def _rwkv(proj3, shift0, s0, mu, w0, w2, a0, a2, g2, k_k, k_a, r_k, gn_g, gn_b, *, c, l_valid,
          l_out, o_width):
    b, lp, _ = proj3.shape
    cw = w0.shape[-1]
    npair = cw // LANES
    heads = cw // RWKV_HEAD
    g = 2 if npair % 2 == 0 else 1
    w = g * LANES
    ng = npair // g
    lw = mu.shape[-1] - 3 * cw
    assert lw % LANES == 0 and (3 * cw) % lw == 0 and lp % c == 0
    n_chunks = lp // c
    wl_, al_ = w2.shape[0], a2.shape[0]
    w2p = jnp.zeros((lw, cw), F32).at[:wl_].set(w2)
    a2p = jnp.zeros((lw, cw), F32).at[wl_:wl_ + al_].set(a2)
    g2p = jnp.zeros((lw, cw), F32).at[wl_ + al_:].set(g2)
    s0p = s0.reshape(b, npair, 2, RWKV_HEAD, RWKV_HEAD)
    z = jnp.zeros_like(s0p[:, :, 0])
    s0bd = jnp.concatenate([jnp.concatenate([s0p[:, :, 0], z], axis=-1),
                            jnp.concatenate([z, s0p[:, :, 1]], axis=-1)], axis=-2)
    sh3 = shift0.reshape(b, 1, -1)
    mu2 = mu.reshape(1, -1)
    row = lambda t: t.reshape(1, cw)
    xoff = (3 * cw) // lw

    nb = _largest_divisor(b, 4, 1)
    seg = lambda s: pl.BlockSpec((nb, c, w), lambda i, j, t: (i, t, s * ng + j))
    sseg = lambda s: pl.BlockSpec((nb, 1, w), lambda i, j, t: (i, 0, s * ng + j))
    mseg = lambda s: pl.BlockSpec((1, w), lambda i, j, t: (0, s * ng + j))
    vec = pl.BlockSpec((1, w), lambda i, j, t: (0, j))
    lora = pl.BlockSpec((lw, w), lambda i, j, t: (0, j))
    state = pl.BlockSpec((nb, g, LANES, LANES), lambda i, j, t: (i, j, 0, 0))
    in_specs = [seg(0), seg(1), seg(2), pl.BlockSpec((nb, c, lw), lambda i, j, t: (i, t, xoff)),
                sseg(0), sseg(1), sseg(2), pl.BlockSpec((nb, 1, lw), lambda i, j, t: (i, 0, xoff)),
                mseg(0), mseg(1), mseg(2), pl.BlockSpec((1, lw), lambda i, j, t: (0, xoff)),
                vec, vec, vec, vec, vec, vec, vec, lora, lora, lora, state]
    o, st = pl.pallas_call(
        functools.partial(_rwkv_kernel, c=c, g=g, nb=nb, l_valid=l_valid, n_chunks=n_chunks),
        grid=(b // nb, ng, n_chunks),
        in_specs=in_specs,
        out_specs=[pl.BlockSpec((nb, c, w), lambda i, j, t: (i, t, j)), state],
        out_shape=[jax.ShapeDtypeStruct((b, l_out, o_width), BF16),
                   jax.ShapeDtypeStruct((b, npair, LANES, LANES), F32)],
        scratch_shapes=[pltpu.VMEM((nb, g, LANES, LANES), F32),
                        pltpu.VMEM((nb, c + 8, w), F32), pltpu.VMEM((nb, c + 8, w), F32),
                        pltpu.VMEM((nb, c + 8, w), F32), pltpu.VMEM((nb, c + 8, lw), F32)],
        compiler_params=_params("parallel", "parallel", "arbitrary"),
        name="rwkv7_chunk",
    )(proj3, proj3, proj3, proj3, sh3, sh3, sh3, sh3, mu2, mu2, mu2, mu2,
      row(w0), row(a0), row(k_k), row(k_a), row(r_k), row(gn_g), row(gn_b), w2p, a2p, g2p, s0bd)
    st = jnp.stack([st[:, :, :RWKV_HEAD, :RWKV_HEAD], st[:, :, RWKV_HEAD:, RWKV_HEAD:]], axis=2)
    return o, st.reshape(b, heads, RWKV_HEAD, RWKV_HEAD)


def _logf_kernel(z_ref, bf_ref, lf_ref, cc_ref, cr_ref, *, rb):
    l, h = z_ref.shape[1], z_ref.shape[2]
    ti = _iota((rb, rb), 0)
    si = _iota((rb, rb), 1)
    tri = (si <= ti).astype(F32)
    carry = jnp.zeros((1, h), F32)
    for i in range(l // rb):
        sl = slice(i * rb, (i + 1) * rb)
        lf = jax.nn.log_sigmoid(z_ref[0, sl, :] + bf_ref[...])
        lf_ref[0, sl, :] = lf
        cblk = carry + _dot(tri, lf, HI)
        cc_ref[0, sl, :] = cblk
        carry = cblk[rb - 1:rb, :]
    eye = (_iota((h, h), 0) == _iota((h, h), 1)).astype(F32)
    cr_ref[0] = _dot_nt(eye, cc_ref[0], HI)


def _logf(z3, b_forget):
    b, l, h = z3.shape
    rb = _largest_divisor(l, 512, 8)
    blk = pl.BlockSpec((1, l, h), lambda i: (i, 0, 0))
    return pl.pallas_call(
        functools.partial(_logf_kernel, rb=rb),
        grid=(b,),
        in_specs=[blk, pl.BlockSpec((1, h), lambda i: (0, 0))],
        out_specs=[blk, blk, pl.BlockSpec((1, h, l), lambda i: (i, 0, 0))],
        out_shape=[jax.ShapeDtypeStruct((b, l, h), F32), jax.ShapeDtypeStruct((b, l, h), F32),
                   jax.ShapeDtypeStruct((b, h, l), F32)],
        compiler_params=_params("parallel"),
        name="fox_logf_cumsum",
    )(z3, b_forget.reshape(1, h))


def _fox_prompt_kernel(q_ref, k_ref, v_ref, cc_ref, cr_ref, o_ref, *, tq):
    l = q_ref.shape[1]
    nh = cc_ref.shape[2]
    h = pl.program_id(1)
    scale = FOX_HEAD ** -0.5
    sel = (_iota((nh, LANES), 0) == h).astype(F32)
    cq_all = _dot(cc_ref[0], sel, HI)
    ck = cr_ref[0, pl.ds(h, 1), :]
    kb = k_ref[0].astype(BF16)
    vb = v_ref[0].astype(BF16)
    for i in range(l // tq):
        nk = (i + 1) * tq
        rows = slice(i * tq, nk)
        s = _dot_nt(q_ref[0, rows, :].astype(BF16), kb[:nk]) * scale
        s = s + cq_all[rows, 0:1] - ck[:, :nk]
        qpos = _iota((tq, nk), 0) + i * tq
        kpos = _iota((tq, nk), 1)
        s = jnp.where(qpos >= kpos, s, NEG_INF)
        m = jnp.max(s, axis=-1, keepdims=True)
        p = jnp.exp(s - m)
        den = jnp.sum(p, axis=-1, keepdims=True)
        o = _dot(p.astype(BF16), vb[:nk]) / den
        o_ref[0, rows, :] = o.astype(o_ref.dtype)


def _fox_prompt(proj3, q_off, k3, v3, c_col, c_row):
    b, l, fw = k3.shape
    nh = c_col.shape[2]
    tq = _largest_divisor(l, 384, 8)
    blk = lambda off: pl.BlockSpec((1, l, LANES), lambda i, j: (i, 0, off + j))
    return pl.pallas_call(
        functools.partial(_fox_prompt_kernel, tq=tq),
        grid=(b, nh),
        in_specs=[blk(q_off), blk(0), blk(0),
                  pl.BlockSpec((1, l, nh), lambda i, j: (i, 0, 0)),
                  pl.BlockSpec((1, nh, l), lambda i, j: (i, 0, 0))],
        out_specs=blk(0),
        out_shape=jax.ShapeDtypeStruct((b, l, fw), BF16),
        compiler_params=_params("parallel", "parallel"),
        name="fox_prompt_attention",
    )(proj3, k3, v3, c_col, c_row)


def _page_tail_kernel(lf_ref, tail_ref, tot_ref):
    pp = lf_ref.shape[0]
    later = (_iota((PAGE_SIZE, PAGE_SIZE), 0) > _iota((PAGE_SIZE, PAGE_SIZE), 1)).astype(BF16)
    for i in range(pp):
        lf = lf_ref[i]
        tail_ref[i] = _dot_exact(_split(lf, 3), later)
        tot_ref[i] = jnp.broadcast_to(jnp.sum(lf, axis=1, keepdims=True), lf.shape)


def _page_tails(lf_pool_t):
    n_phys, nh, _ = lf_pool_t.shape
    pp = _largest_divisor(n_phys, 32, 1)
    blk = pl.BlockSpec((pp, nh, PAGE_SIZE), lambda i: (i, 0, 0))
    shape = jax.ShapeDtypeStruct(lf_pool_t.shape, F32)
    return pl.pallas_call(
        _page_tail_kernel,
        grid=(n_phys // pp,),
        in_specs=[blk],
        out_specs=[blk, blk],
        out_shape=[shape, shape],
        compiler_params=_params("parallel"),
        name="fox_page_tails",
    )(lf_pool_t)


def _fox_sample_kernel(pt_ref, q_ref, kn_ref, vn_ref, cq_ref, ck_ref, *rest, nh, n_steps, pg):
    kp = rest[:pg]
    vp = rest[pg:2 * pg]
    tl = rest[2 * pg:3 * pg]
    tt = rest[3 * pg:4 * pg]
    o_ref, m_scr, l_scr, acc_scr, suf_scr = rest[4 * pg:]
    pi = pl.program_id(1)
    rows = q_ref.shape[1]
    cols = PAGE_SIZE * nh
    scale = FOX_HEAD ** -0.5

    @pl.when(pi == 0)
    def _():
        m_scr[...] = jnp.full(m_scr.shape, NEG_INF, F32)
        l_scr[...] = jnp.zeros(l_scr.shape, F32)
        acc_scr[...] = jnp.zeros(acc_scr.shape, F32)
        suf_scr[...] = jnp.zeros(suf_scr.shape, F32)

    qb = q_ref[0].astype(BF16)
    cnew = cq_ref[0]

    def online(scores, vals):
        m_old = m_scr[...]
        m_new = m_old
        for s in scores:
            m_new = jnp.maximum(m_new, jnp.max(s, axis=-1, keepdims=True))
        corr = jnp.exp(m_old - m_new)
        den = l_scr[...] * corr
        acc = acc_scr[...] * corr
        for s, val in zip(scores, vals):
            pe = jnp.exp(s - m_new)
            den = den + jnp.sum(pe, axis=-1, keepdims=True)
            acc = acc + _dot(pe.astype(BF16), val)
        l_scr[...] = den
        acc_scr[...] = acc
        m_scr[...] = m_new

    own_head = (_iota((rows, cols), 0) % nh) == (_iota((rows, cols), 1) % nh)
    suf = suf_scr[...]
    bias = [None] * pg
    for gi in reversed(range(pg)):
        bias[gi] = suf + tl[gi][0]
        suf = suf + tt[gi][0]
    suf_scr[...] = suf
    scores = []
    for gi in range(pg):
        kflat = kp[gi][0].reshape(cols, FOX_HEAD).astype(BF16)
        s = _dot_nt(qb, kflat) * scale + bias[gi] + cnew
        scores.append(jnp.where(own_head, s, NEG_INF))
    online(scores, [r[0].reshape(cols, FOX_HEAD).astype(BF16) for r in vp])

    @pl.when(pi == n_steps - 1)
    def _():
        sn = _dot_nt(qb, kn_ref[0].astype(BF16)) * scale + cnew - ck_ref[0]
        ri = _iota((rows, rows), 0)
        cj = _iota((rows, rows), 1)
        ok = jnp.logical_and((ri % nh) == (cj % nh), (cj // nh) <= (ri // nh))
        online([jnp.where(ok, sn, NEG_INF)], [vn_ref[0].astype(BF16)])
        o_ref[0] = (acc_scr[...] / l_scr[...]).astype(o_ref.dtype)


def _fox_sample(q, k_new, v_new, c_col, k_pool, v_pool, lf_pool, page_table):
    bd, t_new, width = q.shape
    nh = c_col.shape[2]
    n_pages = page_table.shape[1]
    n_phys = k_pool.shape[0]
    pg = _largest_divisor(n_pages, 8, 1)
    n_steps = n_pages // pg
    rows = t_new * nh
    cols = PAGE_SIZE * nh
    tail, tot = _page_tails(jnp.swapaxes(lf_pool, 1, 2))
    tail = jnp.swapaxes(tail, 1, 2).reshape(n_phys, 1, cols)
    tot = jnp.swapaxes(tot, 1, 2).reshape(n_phys, 1, cols)
    by_head = lambda a: a.reshape(bd, rows, FOX_HEAD)
    full = lambda shape: pl.BlockSpec((1,) + shape, lambda i, p, pt: (i, 0, 0))

    def page(shape, slot):
        zeros = (0,) * len(shape)
        return pl.BlockSpec((1,) + shape,
                            lambda i, p, pt: (pt[i, n_pages - (p + 1) * pg + slot],) + zeros)

    slots = range(pg)
    kv_page = (PAGE_SIZE, nh, FOX_HEAD)
    grid_spec = pltpu.PrefetchScalarGridSpec(
        num_scalar_prefetch=1,
        grid=(bd, n_steps),
        in_specs=([full((rows, FOX_HEAD)), full((rows, FOX_HEAD)), full((rows, FOX_HEAD)),
                   full((rows, 1)), full((1, rows))]
                  + [page(kv_page, s) for s in slots] + [page(kv_page, s) for s in slots]
                  + [page((1, cols), s) for s in slots] + [page((1, cols), s) for s in slots]),
        out_specs=full((rows, FOX_HEAD)),
        scratch_shapes=[pltpu.VMEM((rows, 1), F32), pltpu.VMEM((rows, 1), F32),
                        pltpu.VMEM((rows, FOX_HEAD), F32), pltpu.VMEM((1, cols), F32)],
    )
    out = pl.pallas_call(
        functools.partial(_fox_sample_kernel, nh=nh, n_steps=n_steps, pg=pg),
        grid_spec=grid_spec,
        out_shape=jax.ShapeDtypeStruct((bd, rows, FOX_HEAD), BF16),
        compiler_params=_params("parallel", "arbitrary"),
        name="fox_sample_attention",
    )(page_table, by_head(q), by_head(k_new), by_head(v_new),
      c_col.reshape(bd, rows, 1), c_col.reshape(bd, 1, rows),
      *([k_pool] * pg), *([v_pool] * pg), *([tail] * pg), *([tot] * pg))
    return out.reshape(bd, t_new, width)


def _conv_kernel(bg_ref, cg_ref, h_ref, w_ref, buf_ref, y_ref, nb_ref, u_scr, *, tt, n_t):
    ti = pl.program_id(2)

    @pl.when(ti == 0)
    def _():
        u_scr[6:8, :] = buf_ref[0]

    u = cg_ref[0] * h_ref[0]
    u_scr[8:8 + tt, :] = u
    y = (w_ref[0:1, :] * u_scr[6:6 + tt, :] + w_ref[1:2, :] * u_scr[7:7 + tt, :]
         + w_ref[2:3, :] * u)
    y_ref[0] = (bg_ref[0] * y).astype(y_ref.dtype)
    tail = u_scr[6 + tt:8 + tt, :]
    u_scr[6:8, :] = tail

    @pl.when(ti == n_t - 1)
    def _():
        nb_ref[0] = tail


def _short_conv(proj3, buf0, conv_w):
    b, l, d3 = proj3.shape
    d = d3 // 3
    tc = _largest_divisor(d, 512, LANES)
    tt = _largest_divisor(l, 704, 8)
    n_t = l // tt
    nc = d // tc
    seg = lambda s: pl.BlockSpec((1, tt, tc), lambda i, j, t: (i, t, s * nc + j))
    return pl.pallas_call(
        functools.partial(_conv_kernel, tt=tt, n_t=n_t),
        grid=(b, nc, n_t),
        in_specs=[seg(0), seg(1), seg(2),
                  pl.BlockSpec((conv_w.shape[0], tc), lambda i, j, t: (0, j)),
                  pl.BlockSpec((1, 2, tc), lambda i, j, t: (i, 0, j))],
        out_specs=[pl.BlockSpec((1, tt, tc), lambda i, j, t: (i, t, j)),
                   pl.BlockSpec((1, 2, tc), lambda i, j, t: (i, 0, j))],
        out_shape=[jax.ShapeDtypeStruct((b, l, d), BF16), jax.ShapeDtypeStruct((b, 2, d), F32)],
        scratch_shapes=[pltpu.VMEM((tt + 8, tc), F32)],
        compiler_params=_params("parallel", "parallel", "arbitrary"),
        name="short_conv",
    )(proj3, proj3, proj3, conv_w, buf0)


def _trunk(hp3, hs3, s0, shift0, conv0, k_pool, v_pool, lf_pool, page_table, wts):
    (norm_mix, norm_mlp, w_in_even, b_forget, rwkv_mu, rwkv_w0, rwkv_w2, rwkv_a0, rwkv_a2, rwkv_g2,
     rwkv_k_k, rwkv_k_a, rwkv_r_k, rwkv_gn_g, rwkv_gn_b, w_out_even, w_in_odd, conv_w, w_out_odd,
     w_up, w_down) = wts
    bp, l, d = hp3.shape
    bs, t, _ = hs3.shape
    hp = hp3.reshape(bp * l, d)
    hs = hs3.reshape(bs * t, d)
    cw = rwkv_w0.shape[-1]
    rproj = rwkv_mu.shape[-1]
    fw = d - cw
    nh = fw // FOX_HEAD
    heads = cw // RWKV_HEAD
    rw = (rwkv_mu[0], rwkv_w0[0], rwkv_w2[0], rwkv_a0[0], rwkv_a2[0], rwkv_g2[0], rwkv_k_k[0],
          rwkv_k_a[0], rwkv_r_k[0], rwkv_gn_g[0], rwkv_gn_b[0])
    norm = lambda a, g: _rmsnorm(a, g, BF16)
    wide = functools.partial(_matmul, tn=512, tk=4096)
    narrow = functools.partial(_matmul, tn=256, tk=4096)

    def down(up, us, layer, hp, hs):
        n_slabs = max(1, w_down.shape[1] // d)
        for slab in range(n_slabs):
            hp, hs = wide(up, us, w_down, layer, epilogue="resid", resid=(hp, hs),
                          k_slab=(slab, n_slabs))
        return hp, hs

    lpad = RWKV_CHUNK * pl.cdiv(l, RWKV_CHUNK)
    if lpad > l:
        hnp, hnp_pad = _rmsnorm_pad(hp3, norm_mix[0], lpad)
        hnp = hnp.reshape(bp * l, d)
        hnp_pad = hnp_pad.reshape(bp * lpad, d)
    else:
        hnp = hnp_pad = norm(hp, norm_mix[0])
    hns = norm(hs, norm_mix[0])
    w_in_t = jnp.swapaxes(w_in_even, 1, 2)
    in_proj = functools.partial(narrow, w_nk=True)
    pq_p, pq_s = in_proj(hnp_pad, hns, w_in_t, 0, n_cols=rproj + fw)
    k_p, k_s = in_proj(hnp, hns, w_in_t, 0, n_lo=rproj + fw, n_cols=fw)
    v_p, v_s = in_proj(hnp, hns, w_in_t, 0, n_lo=rproj + 2 * fw, n_cols=fw)
    w_lf = w_in_t[:, rproj + 3 * fw:, :]
    z_p, _ = _matmul(hnp, None, w_lf, 0, tn=nh, tk=4096, w_nk=True)
    z_s, _ = _matmul(hns, None, w_lf, 0, tn=nh, tk=4096, w_nk=True)
    pq_p = pq_p.reshape(bp, lpad, rproj + fw)
    pq_s = pq_s.reshape(bs, t, rproj + fw)
    k_p, v_p = k_p.reshape(bp, l, fw), v_p.reshape(bp, l, fw)
    k_s, v_s = k_s.reshape(bs, t, fw), v_s.reshape(bs, t, fw)
    lf_p, cc_p, cr_p = _logf(z_p.reshape(bp, l, nh), b_forget[0])
    lf_s, cc_s, _ = _logf(z_s.reshape(bs, t, nh), b_forget[0])

    zero_s = jnp.zeros((bp, heads, RWKV_HEAD, RWKV_HEAD), F32)
    or_p, st_p = _rwkv(pq_p, jnp.zeros((bp, rproj), F32), zero_s, *rw, c=RWKV_CHUNK, l_valid=l,
                       l_out=l, o_width=cw)
    of_p = _fox_prompt(pq_p, rproj // LANES, k_p, v_p, cc_p, cr_p)
    tpad = 8 * pl.cdiv(t, 8)
    pq_s_pad = jnp.pad(pq_s[:, :, :rproj], ((0, 0), (0, tpad - t), (0, 0)))
    or_s, st_s = _rwkv(pq_s_pad, shift0[0], s0[0], *rw, c=tpad, l_valid=t, l_out=tpad, o_width=cw)
    of_s = _fox_sample(pq_s[:, :, rproj:], k_s, v_s, cc_s, k_pool[0], v_pool[0], lf_pool[0],
                       page_table)
    mix_p = (or_p.reshape(bp * l, cw), of_p.reshape(bp * l, fw))
    mix_s = (or_s[:, :t].reshape(bs * t, cw), of_s.reshape(bs * t, fw))
    hp, hs = wide(mix_p, mix_s, w_out_even, 0, epilogue="resid", resid=(hp, hs))
    up, us = wide(norm(hp, norm_mlp[0]), norm(hs, norm_mlp[0]), w_up, 0, epilogue="relu2",
                  out_dtype=BF16)
    hp, hs = down(up, us, 0, hp, hs)

    p1_p, p1_s = wide(norm(hp, norm_mix[1]), norm(hs, norm_mix[1]), w_in_odd, 0)
    y_p, buf_p = _short_conv(p1_p.reshape(bp, l, 3 * d), jnp.zeros((bp,) + conv0.shape[2:], F32),
                             conv_w[0])
    y_s, buf_s = _short_conv(p1_s.reshape(bs, t, 3 * d), conv0[0], conv_w[0])
    hp, hs = wide(y_p.reshape(bp * l, d), y_s.reshape(bs * t, d), w_out_odd, 0, epilogue="resid",
                  resid=(hp, hs))
    up, us = wide(norm(hp, norm_mlp[1]), norm(hs, norm_mlp[1]), w_up, 1, epilogue="relu2",
                  out_dtype=BF16)
    hp, hs = down(up, us, 1, hp, hs)

    head4 = lambda a, b, n: a.reshape(b, n, nh, FOX_HEAD)[None]
    outs_p = (st_p[None], pq_p[:, l - 1, :rproj][None], head4(k_p, bp, l), head4(v_p, bp, l),
              lf_p[None], buf_p[None])
    outs_s = (st_s[None], pq_s[:, t - 1, :rproj][None], head4(k_s, bs, t), head4(v_s, bs, t),
              lf_s[None], buf_s[None])
    return hp.reshape(bp, l, d), hs, outs_p, outs_s


def kernel(x_prompt, x_sample, state_rwkv, state_rwkv_shift, cache_fox_k, cache_fox_v, cache_fox_logf, state_conv, page_table, meta_tokens, norm_mix, norm_mlp, norm_final, w_in_even, b_forget, rwkv_mu, rwkv_w0, rwkv_w2, rwkv_a0, rwkv_a2, rwkv_g2, rwkv_k_k, rwkv_k_a, rwkv_r_k, rwkv_gn_g, rwkv_gn_b, w_out_even, w_in_odd, conv_w, w_out_odd, w_up, w_down):
    wts = (norm_mix, norm_mlp, w_in_even, b_forget, rwkv_mu, rwkv_w0, rwkv_w2, rwkv_a0, rwkv_a2,
           rwkv_g2, rwkv_k_k, rwkv_k_a, rwkv_r_k, rwkv_gn_g, rwkv_gn_b, w_out_even, w_in_odd, conv_w,
           w_out_odd, w_up, w_down)
    bp, _, d = x_prompt.shape
    bs, ts, _ = x_sample.shape
    h0 = jnp.concatenate([jnp.broadcast_to(meta_tokens[None], (bp, N_META, d)), x_prompt], axis=1)
    hp, hs, outs_p, outs_s = _trunk(h0, x_sample, state_rwkv, state_rwkv_shift, state_conv,
                                    cache_fox_k, cache_fox_v, cache_fox_logf, page_table, wts)
    y_prompt = _final_norm_prompt(hp, norm_final)
    y_sample = _rmsnorm(hs, norm_final, F32).reshape(bs, ts, d)
    return (y_prompt, y_sample, *outs_p, *outs_s)
```
